```python
import math
import jax, jax.numpy as jnp
from jax import lax
import numpy as np

D_MODEL = 1024
BATCH = 4
SEQ = 4096
DEPTH = 2

N_META = 16
CHUNK = 64
PAD_FRONT = (-N_META) % CHUNK
CONV_W = 4
RMS_EPS = 1e-6
L2_EPS = 1e-6
LB_FLOOR = 1e-30

A_HEADS = 4
A_DK = 128
A_DV = 128
B_HEADS = 4
B_DK = 128
B_DV = 128
C_HEADS = 4
C_DK = 64
C_DV = 128
C_GK_RANK = 16
C_GK_NORM = 16.0

BRANCH_W = 512
N_BRANCH = 3

N_GROUPS = 4
EXP_PER_GROUP = 8
EXPERT_HIDDEN = 256
TOP_K = 2

IN_SPLITS = (
    A_HEADS * A_DK,
    A_HEADS * A_DK,
    A_HEADS * A_DV,
    A_HEADS * A_DV,
    B_HEADS * (2 * B_DK + B_DV),
    B_HEADS,
    B_HEADS,
    B_HEADS * B_DV,
    C_HEADS * C_DK,
    C_HEADS * C_DK,
    C_HEADS * C_DV,
    C_GK_RANK,
    C_HEADS * C_DV,
    N_BRANCH * D_MODEL,
)
IN_COLS = sum(IN_SPLITS)

kernel_name = "hybrid_hgrn2_gdn_gla_hmoe"


def _rmsnorm(x, w):
    xf = x.astype(jnp.float32)
    y = xf * lax.rsqrt(jnp.mean(xf * xf, axis=-1, keepdims=True) + RMS_EPS)
    return (y * w.astype(jnp.float32)).astype(x.dtype)


def _l2norm(x):
    return x * lax.rsqrt(jnp.sum(x * x, axis=-1, keepdims=True) + L2_EPS)


def _split_heads(x, n):
    b, l, c = x.shape
    return jnp.transpose(x.reshape(b, l, n, c // n), (0, 2, 1, 3))


def _to_chunks(x):
    b, h, l, d = x.shape
    return jnp.moveaxis(x.reshape(b, h, l // CHUNK, CHUNK, d), 2, 0)


def _from_chunks(x):
    n, b, h, c, d = x.shape
    return jnp.moveaxis(x, 0, 2).reshape(b, h, n * c, d)


def _masked_exp(diff, mask):
    return jnp.where(mask, jnp.exp(jnp.where(mask, diff, 0.0)), 0.0)


def _gated_head_norm(o, gate, w):
    b, h, l, dv = o.shape
    o = jnp.transpose(o, (0, 2, 1, 3))
    o = _rmsnorm(o, w) * jax.nn.silu(gate.reshape(b, l, h, dv))
    return o.reshape(b, l, h * dv)


def _causal_conv(x, w):
    return lax.conv_general_dilated(
        x, w[:, None, :].astype(x.dtype), window_strides=(1,), padding=((CONV_W - 1, 0),),
        dimension_numbers=("NWC", "WIO", "NWC"), feature_group_count=x.shape[-1])


def _chunk_gla(q, k, v, log_f):
    b_, h_, _, dk = q.shape
    dv = v.shape[-1]
    causal = jnp.tril(jnp.ones((CHUNK, CHUNK), dtype=bool))[:, :, None]

    def step(S, xs):
        qc, kc, vc, gc = xs
        bcum = jnp.cumsum(gc, axis=-2)
        b_last = bcum[..., -1:, :]
        pair = _masked_exp(bcum[..., :, None, :] - bcum[..., None, :, :], causal)
        scores = jnp.einsum('bhtsd,bhsd->bhts', qc[..., :, None, :] * pair, kc)
        o = (jnp.einsum('bhts,bhse->bhte', scores, vc)
             + jnp.einsum('bhtd,bhde->bhte', qc * jnp.exp(bcum), S))
        S = (jnp.exp(b_last[..., 0, :])[..., None] * S
             + jnp.einsum('bhsd,bhse->bhde', kc * jnp.exp(b_last - bcum), vc))
        return S, o

    S0 = jnp.zeros((b_, h_, dk, dv), q.dtype)
    _, o = lax.scan(step, S0, (_to_chunks(q), _to_chunks(k), _to_chunks(v), _to_chunks(log_f)))
    return _from_chunks(o)


def _chunk_gated_delta(q, k, v, log_a, beta):
    dvv = v.shape[-1]
    qc, kc, vc = _to_chunks(q), _to_chunks(k), _to_chunks(v)
    g = jnp.cumsum(_to_chunks(log_a[..., None])[..., 0], axis=-1)
    bc = _to_chunks(beta[..., None])[..., 0]
    causal = jnp.tril(jnp.ones((CHUNK, CHUNK), dtype=bool))
    strict = jnp.tril(jnp.ones((CHUNK, CHUNK), dtype=bool), -1)
    decay = _masked_exp(g[..., :, None] - g[..., None, :], causal)
    a_mat = jnp.where(strict, bc[..., :, None] * jnp.einsum('nbhtd,nbhsd->nbhts', kc, kc) * decay, 0.0)
    eye = jnp.eye(CHUNK, dtype=q.dtype)
    rhs = jnp.concatenate([vc * bc[..., None], kc * (bc * jnp.exp(g))[..., None]], axis=-1)
    sol = lax.linalg.triangular_solve(a_mat + eye, rhs, left_side=True, lower=True, unit_diagonal=True)
    u, w = sol[..., :dvv], sol[..., dvv:]
    qk = jnp.einsum('nbhtd,nbhsd->nbhts', qc, kc) * decay

    def step(S, xs):
        q_, k_, u_, w_, g_, qk_ = xs
        v_new = u_ - jnp.einsum('bhcd,bhde->bhce', w_, S)
        o = (jnp.einsum('bhcd,bhde->bhce', q_ * jnp.exp(g_)[..., None], S)
             + jnp.einsum('bhts,bhse->bhte', qk_, v_new))
        g_last = g_[..., -1:]
        S = (jnp.exp(g_last)[..., None] * S
             + jnp.einsum('bhsd,bhse->bhde', k_ * jnp.exp(g_last - g_)[..., None], v_new))
        return S, o

    S0 = jnp.zeros((q.shape[0], q.shape[1], q.shape[-1], dvv), q.dtype)
    _, o = lax.scan(step, S0, (qc, kc, u, w, g, qk))
    return _from_chunks(o)


def _hgrn2(q_in, f_in, i_in, g_in, lb, norm_w, valid):
    q = jax.nn.silu(q_in) * (A_DK ** -0.5)
    log_f = jnp.logaddexp(jnp.log(jnp.maximum(lb, LB_FLOOR)), jnp.log1p(-lb) + jax.nn.log_sigmoid(f_in))
    k = (1.0 - lb) * jax.nn.sigmoid(-f_in) * valid
    o = _chunk_gla(_split_heads(q, A_HEADS), _split_heads(k, A_HEADS),
                   _split_heads(i_in, A_HEADS), _split_heads(log_f, A_HEADS))
    return _gated_head_norm(o, g_in, norm_w)


def _gated_deltanet(qkv_in, beta_in, decay_in, g_in, conv_w, a_log, dt_bias, norm_w, valid):
    qkv = jax.nn.silu(_causal_conv(qkv_in, conv_w))
    q, k, v = jnp.split(qkv, [B_HEADS * B_DK, 2 * B_HEADS * B_DK], axis=-1)
    q = _l2norm(_split_heads(q, B_HEADS)) * (B_DK ** -0.5)
    k = _l2norm(_split_heads(k * valid, B_HEADS))
    v = _split_heads(v, B_HEADS)
    beta = jnp.transpose(jax.nn.sigmoid(beta_in) * valid, (0, 2, 1))
    log_a = jnp.transpose(-jnp.exp(a_log) * jax.nn.softplus(decay_in + dt_bias), (0, 2, 1))
    o = _chunk_gated_delta(q, k, v, log_a, beta)
    return _gated_head_norm(o, g_in, norm_w)


def _gla(q_in, k_in, v_in, gk_low, g_in, gk_w2, gk_b, norm_w, valid):
    q = _split_heads(q_in, C_HEADS) * (C_DK ** -0.5)
    k = _split_heads(k_in * valid, C_HEADS)
    v = _split_heads(v_in, C_HEADS)
    log_f = _split_heads(jax.nn.log_sigmoid(gk_low @ gk_w2 + gk_b) / C_GK_NORM, C_HEADS)
    o = _chunk_gla(q, k, v, log_f)
    return _gated_head_norm(o, g_in, norm_w)


def _hier_moe(x, wg, bg, we, be, w_gate, w_up, w_down):
    xf = x.astype(jnp.float32)
    b, l, _ = xf.shape
    p_group = jax.nn.softmax(xf @ wg + bg, axis=-1)
    g_val, g_idx = lax.top_k(p_group, 1)
    e_logits = (xf @ we + be).reshape(b, l, N_GROUPS, EXP_PER_GROUP)
    e_logits = jnp.take_along_axis(e_logits, g_idx[..., None], axis=2)[:, :, 0, :]
    p_exp = jax.nn.softmax(e_logits, axis=-1)
    e_val, e_idx = lax.top_k(p_exp, TOP_K)
    e_w = e_val / jnp.sum(e_val, axis=-1, keepdims=True)
    within = jnp.einsum('blk,blke->ble', e_w, jax.nn.one_hot(e_idx, EXP_PER_GROUP, dtype=jnp.float32))
    combine = (jax.nn.one_hot(g_idx[..., 0], N_GROUPS, dtype=jnp.float32)[..., None]
               * (g_val * within)[:, :, None, :])
    y = jnp.zeros_like(xf)
    for grp in range(N_GROUPS):
        hid = (jax.nn.silu(jnp.einsum('bld,edh->bleh', xf, w_gate[grp]))
               * jnp.einsum('bld,edh->bleh', xf, w_up[grp]))
        y = y + jnp.einsum('bleh,ble,ehd->bld', hid, combine[:, :, grp], w_down[grp])
    return y


def setup_inputs(seed: int = 0) -> dict:
    key = jax.random.key(seed)
    ks = jax.random.split(key, 24)
    f32 = jnp.float32

    def nrm(k, shape, scale):
        return jax.random.normal(k, shape, f32) * scale

    dt = jnp.exp(jax.random.uniform(ks[8], (DEPTH, B_HEADS), f32, math.log(1e-3), math.log(1e-1)))
    return {
        "x": nrm(ks[0], (BATCH, SEQ, D_MODEL), 1.0),
        "meta_tokens": nrm(ks[1], (N_META, D_MODEL), 1.0),
        "norm_mix": 1.0 + nrm(ks[2], (DEPTH, D_MODEL), 0.02),
        "w_in": nrm(ks[3], (DEPTH, D_MODEL, IN_COLS), D_MODEL ** -0.5),
        "hgrn_lb_logits": nrm(ks[4], (DEPTH, A_HEADS * A_DK), 0.5),
        "hgrn_norm": 1.0 + nrm(ks[5], (DEPTH, A_DV), 0.02),
        "gdn_conv": nrm(ks[6], (DEPTH, CONV_W, B_HEADS * (2 * B_DK + B_DV)), CONV_W ** -0.5),
        "gdn_a_log": jnp.log(jax.random.uniform(ks[7], (DEPTH, B_HEADS), f32, 1.0, 16.0)),
        "gdn_dt_bias": dt + jnp.log(-jnp.expm1(-dt)),
        "gdn_norm": 1.0 + nrm(ks[9], (DEPTH, B_DV), 0.02),
        "gla_gk_w2": nrm(ks[10], (DEPTH, C_GK_RANK, C_HEADS * C_DK), C_GK_RANK ** -0.5),
        "gla_gk_b": nrm(ks[11], (DEPTH, C_HEADS * C_DK), 0.1),
        "gla_norm": 1.0 + nrm(ks[12], (DEPTH, C_DV), 0.02),
        "w_branch": nrm(ks[13], (DEPTH, N_BRANCH, BRANCH_W, D_MODEL), BRANCH_W ** -0.5),
        "w_out": nrm(ks[14], (DEPTH, D_MODEL, D_MODEL), D_MODEL ** -0.5),
        "norm_ffn": 1.0 + nrm(ks[15], (DEPTH, D_MODEL), 0.02),
        "router_group_w": nrm(ks[16], (DEPTH, D_MODEL, N_GROUPS), D_MODEL ** -0.5),
        "router_group_b": nrm(ks[17], (DEPTH, N_GROUPS), 0.01),
        "router_expert_w": nrm(ks[18], (DEPTH, D_MODEL, N_GROUPS * EXP_PER_GROUP), D_MODEL ** -0.5),
        "router_expert_b": nrm(ks[19], (DEPTH, N_GROUPS * EXP_PER_GROUP), 0.01),
        "expert_w_gate": nrm(ks[20], (DEPTH, N_GROUPS, EXP_PER_GROUP, D_MODEL, EXPERT_HIDDEN), D_MODEL ** -0.5),
        "expert_w_up": nrm(ks[21], (DEPTH, N_GROUPS, EXP_PER_GROUP, D_MODEL, EXPERT_HIDDEN), D_MODEL ** -0.5),
        "expert_w_down": nrm(ks[22], (DEPTH, N_GROUPS, EXP_PER_GROUP, EXPERT_HIDDEN, D_MODEL), EXPERT_HIDDEN ** -0.5),
        "norm_final": 1.0 + nrm(ks[23], (D_MODEL,), 0.02),
    }


def reference(x, meta_tokens, norm_mix, w_in, hgrn_lb_logits, hgrn_norm, gdn_conv, gdn_a_log,
              gdn_dt_bias, gdn_norm, gla_gk_w2, gla_gk_b, gla_norm, w_branch, w_out, norm_ffn,
              router_group_w, router_group_b, router_expert_w, router_expert_b,
              expert_w_gate, expert_w_up, expert_w_down, norm_final):
    b, s, d = x.shape
    dtype = x.dtype
    total = PAD_FRONT + N_META + s
    pad = jnp.zeros((b, PAD_FRONT, d), dtype)
    meta = jnp.broadcast_to(meta_tokens.astype(dtype)[None], (b, N_META, d))
    h = jnp.concatenate([pad, meta, x], axis=1)
    valid = (jnp.arange(total) >= PAD_FRONT).astype(jnp.float32)[None, :, None]

    lb_p = jax.nn.softmax(hgrn_lb_logits.astype(jnp.float32), axis=0)
    lb_all = jnp.maximum(jnp.cumsum(lb_p, axis=0) - lb_p[0:1], 0.0)

    split_points = np.cumsum(IN_SPLITS)[:-1].tolist()
    for layer in range(DEPTH):
        u = _rmsnorm(h, norm_mix[layer]).astype(jnp.float32) * valid
        (a_q, a_f, a_i, a_g, b_qkv, b_beta, b_decay, b_g,
         c_q, c_k, c_v, c_gk, c_g, gate_logits) = jnp.split(u @ w_in[layer], split_points, axis=-1)
        y_a = _hgrn2(a_q, a_f, a_i, a_g, lb_all[layer], hgrn_norm[layer], valid)
        y_b = _gated_deltanet(b_qkv, b_beta, b_decay, b_g, gdn_conv[layer], gdn_a_log[layer],
                              gdn_dt_bias[layer], gdn_norm[layer], valid)
        y_c = _gla(c_q, c_k, c_v, c_gk, c_g, gla_gk_w2[layer], gla_gk_b[layer], gla_norm[layer], valid)
        ys = jnp.stack([y_a, y_b, y_c], axis=2)
        gates = jax.nn.sigmoid(gate_logits.reshape(b, total, N_BRANCH, d))
        proj = jnp.einsum('blne,ned->blnd', ys, w_branch[layer])
        merged = jnp.sum(gates * proj, axis=2)
        h = h + (merged @ w_out[layer]).astype(dtype)
        u2 = _rmsnorm(h, norm_ffn[layer])
        h = h + _hier_moe(u2, router_group_w[layer], router_group_b[layer], router_expert_w[layer],
                          router_expert_b[layer], expert_w_gate[layer], expert_w_up[layer],
                          expert_w_down[layer]).astype(dtype)

    out = _rmsnorm(h, norm_final)
    return out[:, PAD_FRONT + N_META:, :]
```

```python
import functools
import math

import numpy as np
import jax
import jax.numpy as jnp
from jax import lax
from jax.experimental import pallas as pl
from jax.experimental.pallas import tpu as pltpu

F32 = jnp.float32
BF16 = jnp.bfloat16

D_MODEL = 1024
N_META = 16
CONV_W = 4
RMS_EPS = 1e-6
L2_EPS = 1e-6
LB_FLOOR = 1e-30
HEADS = 4
HEAD_W = 128
BRANCH_W = HEADS * HEAD_W
C_DK = 64
C_GK_RANK = 16
C_GK_NORM = 16.0
N_GROUPS = 4
EXP_PER_GROUP = 8
N_EXPERTS = N_GROUPS * EXP_PER_GROUP
EXPERT_HIDDEN = 256

R = 256
N_LEVELS = 8
LV_DIAG = N_LEVELS
W_CUM = N_LEVELS
W_SFX = N_LEVELS + 1
LANES = 128

P16_GATES = 0
P16_AQ, P16_AI, P16_AG = 6, 7, 8
P16_BQ, P16_BK, P16_BV, P16_BG = 9, 10, 11, 12
P16_CQ, P16_CK, P16_CV, P16_CG = 13, 14, 15, 16
P16_COLS = 17 * BRANCH_W
P32_AF = 0
P32_SMALL_COL = BRANCH_W
P32_COLS = BRANCH_W + LANES
SM_BETA, SM_DECAY, SM_GK = 0, 4, 8

VMEM_LIMIT = 56 * 1024 * 1024
MOE_ROWS = 640


def _cparams(sem):
    return pltpu.CompilerParams(dimension_semantics=sem, vmem_limit_bytes=VMEM_LIMIT)


def _const_spec(shape):
    nd = len(shape)
    return pl.BlockSpec(shape, lambda *_: (0,) * nd)


@functools.lru_cache(maxsize=None)
def _level_constants():
    t = np.arange(R)[:, None]
    s = np.arange(R)[None, :]
    x = np.maximum(t ^ s, 1)
    lv = np.where(s < t, np.floor(np.log2(x)).astype(np.int32), np.where(s == t, LV_DIAG, -1)).astype(np.int32)
    w = np.zeros((N_LEVELS + 2, R, R), np.float32)
    for l in range(N_LEVELS):
        hsz = 1 << l
        for r in range(R):
            hb = (r // hsz) * hsz
            if (r >> l) & 1:
                w[l, r, hb:r + 1] = 1.0
            else:
                w[l, r, r + 1:hb + hsz] = 1.0
    w[W_CUM] = np.tril(np.ones((R, R), np.float32))
    w[W_SFX] = np.triu(np.ones((R, R), np.float32), 1)
    return lv, w


def _dot(a, b):
    return jnp.dot(a, b, preferred_element_type=F32)


def _dot_nt(a, b):
    return lax.dot_general(a, b, (((1,), (1,)), ((), ())), preferred_element_type=F32)


def _dot_tn(a, b):
    return lax.dot_general(a, b, (((0,), (0,)), ((), ())), preferred_element_type=F32)


def _sigmoid(x):
    return 1.0 / (1.0 + jnp.exp(-x))


def _silu(x):
    return x * _sigmoid(x)


def _softplus(x):
    return jnp.maximum(x, 0.0) + jnp.log1p(jnp.exp(-jnp.abs(x)))


def _log_sigmoid(x):
    return -_softplus(-x)


def _valid_rows(i):
    row = lax.broadcasted_iota(jnp.int32, (R, 1), 0)
    first_valid = jnp.where(i > 0, 0, R - N_META)
    return jnp.where(row >= first_valid, 1.0, 0.0).astype(F32)


def _inproj_kernel(h_ref, nw_ref, w_ref, o16_ref, o32_ref):
    i = pl.program_id(0)
    x = h_ref[...]
    y = x * lax.rsqrt(jnp.mean(x * x, axis=-1, keepdims=True) + RMS_EPS) * nw_ref[...]
    u = (y * _valid_rows(i)).astype(BF16)
    for c in range(P16_COLS // BRANCH_W):
        sl = slice(c * BRANCH_W, (c + 1) * BRANCH_W)
        o16_ref[:, sl] = _dot(u, w_ref[:, sl]).astype(BF16)
    o32_ref[:, :BRANCH_W] = _dot(u, w_ref[:, P16_COLS:P16_COLS + BRANCH_W])
    o32_ref[:, BRANCH_W:] = _dot(u, w_ref[:, P16_COLS + BRANCH_W:])


def _inproj(h, norm_w, w):
    t = h.shape[0]
    return pl.pallas_call(
        _inproj_kernel,
        grid=(t // R,),
        in_specs=[pl.BlockSpec((R, D_MODEL), lambda i: (i, 0)),
                  _const_spec((1, D_MODEL)),
                  _const_spec((D_MODEL, P16_COLS + P32_COLS))],
        out_specs=[pl.BlockSpec((R, P16_COLS), lambda i: (i, 0)),
                   pl.BlockSpec((R, P32_COLS), lambda i: (i, 0))],
        out_shape=[jax.ShapeDtypeStruct((t, P16_COLS), BF16),
                   jax.ShapeDtypeStruct((t, P32_COLS), F32)],
        compiler_params=_cparams(("arbitrary",)),
        name="inproj",
    )(h, norm_w, w)


def _state_in(i, nb, st_ref, stm_ref):
    @pl.when(i == 0)
    def _():
        st_ref[...] = jnp.zeros(st_ref.shape, st_ref.dtype)

    @pl.when((i >= 1) & ((i - 1) % nb == 0))
    def _():
        st_ref[...] = stm_ref[...]


def _state_out(i, st_ref, stm_ref):
    @pl.when(i == 0)
    def _():
        stm_ref[...] = st_ref[...]


def _head_out(o, gate, nw):
    y = o * lax.rsqrt(jnp.mean(o * o, axis=-1, keepdims=True) + RMS_EPS) * nw
    return (y * _silu(gate)).astype(BF16)


def _gla_block(q, k, v, g, gate, nw, wl_ref, lv, st_ref, p_ref, y_ref):
    g16 = g.astype(BF16)
    for lvl in range(N_LEVELS + 1):
        if lvl < N_LEVELS:
            f = jnp.exp(_dot(wl_ref[lvl], g16))
            qs = (q * f).astype(BF16)
            ks = (k * f).astype(BF16)
        else:
            qs = q.astype(BF16)
            ks = k.astype(BF16)
        m = lv == lvl
        for h in range(HEADS):
            sl = slice(h * HEAD_W, (h + 1) * HEAD_W)
            full = _dot_nt(qs[:, sl], ks[:, sl])
            if lvl == 0:
                p_ref[h] = jnp.where(m, full, 0.0)
            else:
                p_ref[h] = jnp.where(m, full, p_ref[h])
    b = _dot(wl_ref[W_CUM], g16)
    sfx = _dot(wl_ref[W_SFX], g16)
    qe = (q * jnp.exp(b)).astype(BF16)
    ke = (k * jnp.exp(sfx)).astype(BF16)
    dec = jnp.exp(b[R - 1:R, :])
    v16 = v.astype(BF16)
    for h in range(HEADS):
        sl = slice(h * HEAD_W, (h + 1) * HEAD_W)
        st = st_ref[h]
        o = _dot(p_ref[h].astype(BF16), v16[:, sl]) + _dot_nt(qe[:, sl], st.astype(BF16))
        st_ref[h] = st * dec[:, sl] + _dot_tn(v16[:, sl], ke[:, sl])
        y_ref[:, sl] = _head_out(o, gate[:, sl], nw)


def _mixer_scratch():
    return [pltpu.VMEM((HEADS, HEAD_W, HEAD_W), F32),
            pltpu.VMEM((HEADS, HEAD_W, HEAD_W), F32),
            pltpu.VMEM((HEADS, R, R), F32)]


def _col_spec(c, width=BRANCH_W):
    return pl.BlockSpec((R, width), lambda i: (i, c))


def _hgrn_kernel(nb, aq_ref, af_ref, ai_ref, ag_ref, lb_ref, nw_ref, wl_ref, lv_ref,
                 y_ref, st_ref, stm_ref, p_ref):
    i = pl.program_id(0)
    _state_in(i, nb, st_ref, stm_ref)
    valid = _valid_rows(i)
    f_in = af_ref[...]
    lb = lb_ref[0:1, :]
    log_lb = lb_ref[1:2, :]
    log1m_lb = lb_ref[2:3, :]
    q = _silu(aq_ref[...].astype(F32)) * (HEAD_W ** -0.5)
    c = log1m_lb + _log_sigmoid(f_in)
    g = jnp.maximum(log_lb, c) + jnp.log1p(jnp.exp(-jnp.abs(log_lb - c)))
    k = (1.0 - lb) * _sigmoid(-f_in) * valid
    _gla_block(q, k, ai_ref[...].astype(F32), g, ag_ref[...].astype(F32), nw_ref[...],
               wl_ref, lv_ref[...], st_ref, p_ref, y_ref)
    _state_out(i, st_ref, stm_ref)


def _hgrn(p16, p32, lb_rows, norm_w, wl, lv, nb):
    t = p16.shape[0]
    return pl.pallas_call(
        functools.partial(_hgrn_kernel, nb),
        grid=(t // R,),
        in_specs=[_col_spec(P16_AQ), _col_spec(P32_AF), _col_spec(P16_AI), _col_spec(P16_AG),
                  _const_spec((8, BRANCH_W)), _const_spec((1, HEAD_W)),
                  _const_spec(wl.shape), _const_spec((R, R))],
        out_specs=pl.BlockSpec((R, BRANCH_W), lambda i: (i, 0)),
        out_shape=jax.ShapeDtypeStruct((t, BRANCH_W), BF16),
        scratch_shapes=_mixer_scratch(),
        compiler_params=_cparams(("arbitrary",)),
        name="hgrn2",
    )(p16, p32, p16, p16, lb_rows, norm_w, wl, lv)


def _gla_kernel(nb, cq_ref, ck_ref, cv_ref, cg_ref, sm_ref, w2_ref, b2_ref, nw_ref, wl_ref, lv_ref,
                y_ref, st_ref, stm_ref, p_ref):
    i = pl.program_id(0)
    _state_in(i, nb, st_ref, stm_ref)
    valid = _valid_rows(i)
    z = _dot(sm_ref[...].astype(BF16), w2_ref[...]) + b2_ref[...]
    g = _log_sigmoid(z) * (1.0 / C_GK_NORM)
    q = cq_ref[...].astype(F32) * (C_DK ** -0.5)
    k = ck_ref[...].astype(F32) * valid
    _gla_block(q, k, cv_ref[...].astype(F32), g, cg_ref[...].astype(F32), nw_ref[...],
               wl_ref, lv_ref[...], st_ref, p_ref, y_ref)
    _state_out(i, st_ref, stm_ref)


def _gla(p16, p32, w2, b2, norm_w, wl, lv, nb):
    t = p16.shape[0]
    return pl.pallas_call(
        functools.partial(_gla_kernel, nb),
        grid=(t // R,),
        in_specs=[_col_spec(P16_CQ), _col_spec(P16_CK), _col_spec(P16_CV), _col_spec(P16_CG),
                  _col_spec(P32_SMALL_COL // LANES, LANES),
                  _const_spec((LANES, BRANCH_W)), _const_spec((1, BRANCH_W)), _const_spec((1, HEAD_W)),
                  _const_spec(wl.shape), _const_spec((R, R))],
        out_specs=pl.BlockSpec((R, BRANCH_W), lambda i: (i, 0)),
        out_shape=jax.ShapeDtypeStruct((t, BRANCH_W), BF16),
        scratch_shapes=_mixer_scratch(),
        compiler_params=_cparams(("arbitrary",)),
        name="gla",
    )(p16, p16, p16, p16, p32, w2, b2, norm_w, wl, lv)


TAIL = 8


def _gdn_kernel(nb, bq_ref, bk_ref, bv_ref, bg_ref, sm_ref, cw_ref, hp_ref, nw_ref, wl_ref, lv_ref,
                y_ref, st_ref, stm_ref, xx_ref, tailm_ref, x_ref):
    i = pl.program_id(0)
    _state_in(i, nb, st_ref, stm_ref)
    valid = _valid_rows(i)
    lv = lv_ref[...]

    @pl.when(i == 0)
    def _():
        xx_ref[0:TAIL, :] = jnp.zeros((TAIL, 3 * BRANCH_W), F32)

    @pl.when((i >= 1) & ((i - 1) % nb == 0))
    def _():
        xx_ref[0:TAIL, :] = tailm_ref[...]

    for c, ref in enumerate((bq_ref, bk_ref, bv_ref)):
        xx_ref[TAIL:, c * BRANCH_W:(c + 1) * BRANCH_W] = ref[...].astype(F32)
    conv = jnp.zeros((R, 3 * BRANCH_W), F32)
    for j in range(CONV_W):
        off = TAIL - (CONV_W - 1) + j
        conv = conv + xx_ref[off:off + R, :] * cw_ref[j:j + 1, :]
    qkv = _silu(conv)
    new_tail = xx_ref[R:R + TAIL, :]
    xx_ref[0:TAIL, :] = new_tail

    @pl.when(i == 0)
    def _():
        tailm_ref[...] = new_tail

    sm = sm_ref[...]
    a_neg = hp_ref[0:1, :]
    dt_b = hp_ref[1:2, :]
    log_a = a_neg * _softplus(sm + dt_b)
    la_hi = log_a.astype(BF16)
    la_lo = (log_a - la_hi.astype(F32)).astype(BF16)
    gcum = _dot(wl_ref[W_CUM], la_hi) + _dot(wl_ref[W_CUM], la_lo)
    gcum_t = gcum.T
    beta_all = _sigmoid(sm) * valid

    strict = (lv >= 0) & (lv < LV_DIAG)
    causal = lv >= 0
    eye = (lv == LV_DIAG).astype(F32)

    for h in range(HEADS):
        sl = slice(h * HEAD_W, (h + 1) * HEAD_W)
        qh = qkv[:, sl]
        kh = qkv[:, BRANCH_W + h * HEAD_W:BRANCH_W + (h + 1) * HEAD_W] * valid
        vh = qkv[:, 2 * BRANCH_W + h * HEAD_W:2 * BRANCH_W + (h + 1) * HEAD_W]
        qh = qh * lax.rsqrt(jnp.sum(qh * qh, axis=-1, keepdims=True) + L2_EPS) * (HEAD_W ** -0.5)
        kh = kh * lax.rsqrt(jnp.sum(kh * kh, axis=-1, keepdims=True) + L2_EPS)
        beta = beta_all[:, SM_BETA + h:SM_BETA + h + 1]
        gcol = gcum[:, SM_DECAY + h:SM_DECAY + h + 1]
        grow = gcum_t[SM_DECAY + h:SM_DECAY + h + 1, :]
        dm = jnp.exp(jnp.minimum(gcol - grow, 0.0))
        q16 = qh.astype(BF16)
        k16 = kh.astype(BF16)
        a_mat = jnp.where(strict, beta * _dot_nt(k16, k16) * dm, 0.0)
        qk = jnp.where(causal, _dot_nt(q16, k16) * dm, 0.0)
        x_ref[...] = eye
        for lvl in range(N_LEVELS):
            xcur = x_ref[...].astype(BF16)
            l_lvl = jnp.where(lv == lvl, a_mat, 0.0).astype(BF16)
            x_ref[...] = x_ref[...] - _dot(_dot(xcur, l_lvl).astype(BF16), xcur)
        eg = jnp.exp(gcol)
        rhs = jnp.concatenate([vh * beta, kh * (beta * eg)], axis=-1).astype(BF16)
        sol = _dot(x_ref[...].astype(BF16), rhs)
        u = sol[:, :HEAD_W]
        w = sol[:, HEAD_W:]
        st = st_ref[h]
        st16 = st.astype(BF16)
        v_new = u - _dot_nt(w.astype(BF16), st16)
        vn16 = v_new.astype(BF16)
        o = _dot_nt((qh * eg).astype(BF16), st16) + _dot(qk.astype(BF16), vn16)
        g_last = gcol[R - 1:R, :]
        ke = (kh * jnp.exp(g_last - gcol)).astype(BF16)
        st_ref[h] = st * jnp.exp(g_last) + _dot_tn(vn16, ke)
        y_ref[:, sl] = _head_out(o, bg_ref[:, sl].astype(F32), nw_ref[...])
    _state_out(i, st_ref, stm_ref)


def _gdn(p16, p32, conv_w, head_params, norm_w, wl, lv, nb):
    t = p16.shape[0]
    return pl.pallas_call(
        functools.partial(_gdn_kernel, nb),
        grid=(t // R,),
        in_specs=[_col_spec(P16_BQ), _col_spec(P16_BK), _col_spec(P16_BV), _col_spec(P16_BG),
                  _col_spec(P32_SMALL_COL // LANES, LANES),
                  _const_spec((CONV_W, 3 * BRANCH_W)), _const_spec((8, LANES)), _const_spec((1, HEAD_W)),
                  _const_spec(wl.shape), _const_spec((R, R))],
        out_specs=pl.BlockSpec((R, BRANCH_W), lambda i: (i, 0)),
        out_shape=jax.ShapeDtypeStruct((t, BRANCH_W), BF16),
        scratch_shapes=[pltpu.VMEM((HEADS, HEAD_W, HEAD_W), F32),
                        pltpu.VMEM((HEADS, HEAD_W, HEAD_W), F32),
                        pltpu.VMEM((R + TAIL, 3 * BRANCH_W), F32),
                        pltpu.VMEM((TAIL, 3 * BRANCH_W), F32),
                        pltpu.VMEM((R, R), F32)],
        compiler_params=_cparams(("arbitrary",)),
        name="gdn",
    )(p16, p16, p16, p16, p32, conv_w, head_params, norm_w, wl, lv)


def _split_hi_lo(x):
    hi = x.astype(BF16)
    return hi, (x - hi.astype(F32)).astype(BF16)


def _merge_kernel(ya_ref, yb_ref, yc_ref, g0_ref, g1_ref, g2_ref, h_ref, wb_ref, wo_ref, nf_ref,
                  wr_hi_ref, wr_lo_ref, br_ref, hn_ref, u2_ref, cmb_ref):
    merged = jnp.zeros((R, D_MODEL), F32)
    for n, (y_ref, g_ref) in enumerate(((ya_ref, g0_ref), (yb_ref, g1_ref), (yc_ref, g2_ref))):
        merged = merged + _sigmoid(g_ref[...].astype(F32)) * _dot(y_ref[...], wb_ref[n])
    hn = h_ref[...] + _dot(merged.astype(BF16), wo_ref[...])
    hn_ref[...] = hn
    u2 = hn * lax.rsqrt(jnp.mean(hn * hn, axis=-1, keepdims=True) + RMS_EPS) * nf_ref[...]
    u2_ref[...] = u2.astype(BF16)

    u_hi, u_lo = _split_hi_lo(u2)
    logits = (_dot(u_hi, wr_hi_ref[...]) + _dot(u_hi, wr_lo_ref[...]) + _dot(u_lo, wr_hi_ref[...])
              + br_ref[...])
    lane_i = lax.broadcasted_iota(jnp.int32, (R, LANES), 1)
    lane = lane_i.astype(F32)
    lane_grp = (lane_i // EXP_PER_GROUP).astype(F32)
    neg = jnp.float32(-jnp.inf)
    big = jnp.float32(1e9)
    is_g = (lane_i >= N_EXPERTS) & (lane_i < N_EXPERTS + N_GROUPS)
    lg = jnp.where(is_g, logits, neg)
    mg = jnp.max(lg, axis=-1, keepdims=True)
    zg = jnp.sum(jnp.exp(lg - mg), axis=-1, keepdims=True)
    g_val = 1.0 / zg
    g_idx = jnp.min(jnp.where(lg == mg, lane, big), axis=-1, keepdims=True) - N_EXPERTS
    in_grp = (lane_i < N_EXPERTS) & (lane_grp == g_idx)
    le = jnp.where(in_grp, logits, neg)
    m1 = jnp.max(le, axis=-1, keepdims=True)
    ze = jnp.sum(jnp.exp(le - m1), axis=-1, keepdims=True)
    i1 = jnp.min(jnp.where(le == m1, lane, big), axis=-1, keepdims=True)
    le2 = jnp.where(lane == i1, neg, le)
    m2 = jnp.max(le2, axis=-1, keepdims=True)
    i2 = jnp.min(jnp.where(le2 == m2, lane, big), axis=-1, keepdims=True)
    p1 = 1.0 / ze
    p2 = jnp.exp(m2 - m1) / ze
    den = p1 + p2
    cmb_ref[...] = g_val * jnp.where(lane == i1, p1 / den, jnp.where(lane == i2, p2 / den, 0.0))


def _merge(ya, yb, yc, p16, h, wb, wo, nf, wr_hi, wr_lo, br):
    t = h.shape[0]
    row = lambda w: pl.BlockSpec((R, w), lambda i: (i, 0))
    gate = lambda n: pl.BlockSpec((R, D_MODEL), lambda i: (i, n))
    return pl.pallas_call(
        _merge_kernel,
        grid=(t // R,),
        in_specs=[row(BRANCH_W), row(BRANCH_W), row(BRANCH_W), gate(0), gate(1), gate(2), row(D_MODEL),
                  _const_spec((3, BRANCH_W, D_MODEL)), _const_spec((D_MODEL, D_MODEL)),
                  _const_spec((1, D_MODEL)), _const_spec((D_MODEL, LANES)), _const_spec((D_MODEL, LANES)),
                  _const_spec((1, LANES))],
        out_specs=[row(D_MODEL), row(D_MODEL), row(LANES)],
        out_shape=[jax.ShapeDtypeStruct((t, D_MODEL), F32),
                   jax.ShapeDtypeStruct((t, D_MODEL), BF16),
                   jax.ShapeDtypeStruct((t, LANES), F32)],
        compiler_params=_cparams(("arbitrary",)),
        name="merge_router",
    )(ya, yb, yc, p16, p16, p16, h, wb, wo, nf, wr_hi, wr_lo, br)


def _moe_kernel(u_ref, c_ref, h_ref, wg_ref, wu_ref, wd_ref, o_ref):
    g = pl.program_id(1)

    @pl.when(g == 0)
    def _():
        o_ref[...] = h_ref[...]

    x = u_ref[...]
    cmb = c_ref[...]
    lane = lax.broadcasted_iota(jnp.int32, cmb.shape, 1)
    acc = jnp.zeros(o_ref.shape, F32)
    for e in range(EXP_PER_GROUP):
        ce = jnp.sum(jnp.where(lane == g * EXP_PER_GROUP + e, cmb, 0.0), axis=-1, keepdims=True)
        hid = _silu(_dot(x, wg_ref[0, e])) * _dot(x, wu_ref[0, e]) * ce
        acc = acc + _dot(hid.astype(BF16), wd_ref[0, e])
    o_ref[...] += acc


def _moe(u2, cmb, h, wg, wu, wd):
    t = h.shape[0]
    tm = MOE_ROWS if t % MOE_ROWS == 0 else R
    row = lambda w: pl.BlockSpec((tm, w), lambda i, g: (i, 0))
    return pl.pallas_call(
        _moe_kernel,
        grid=(t // tm, N_GROUPS),
        in_specs=[row(D_MODEL), row(LANES), row(D_MODEL),
                  pl.BlockSpec((1, EXP_PER_GROUP, D_MODEL, EXPERT_HIDDEN), lambda i, g: (g, 0, 0, 0)),
                  pl.BlockSpec((1, EXP_PER_GROUP, D_MODEL, EXPERT_HIDDEN), lambda i, g: (g, 0, 0, 0)),
                  pl.BlockSpec((1, EXP_PER_GROUP, EXPERT_HIDDEN, D_MODEL), lambda i, g: (g, 0, 0, 0))],
        out_specs=row(D_MODEL),
        out_shape=jax.ShapeDtypeStruct((t, D_MODEL), F32),
        compiler_params=_cparams(("arbitrary", "arbitrary")),
        name="moe",
    )(u2, cmb, h, wg, wu, wd)


def _final_norm_kernel(h_ref, nw_ref, o_ref):
    x = h_ref[...]
    o_ref[...] = x * lax.rsqrt(jnp.mean(x * x, axis=-1, keepdims=True) + RMS_EPS) * nw_ref[...]


def _final_norm(h, nw):
    t = h.shape[0]
    return pl.pallas_call(
        _final_norm_kernel,
        grid=(t // R - 1,),
        in_specs=[pl.BlockSpec((R, D_MODEL), lambda i: (i + 1, 0)), _const_spec((1, D_MODEL))],
        out_specs=pl.BlockSpec((R, D_MODEL), lambda i: (i, 0)),
        out_shape=jax.ShapeDtypeStruct((t - R, D_MODEL), F32),
        compiler_params=_cparams(("arbitrary",)),
        name="final_norm",
    )(h, nw)


IN_SPLITS = (512, 512, 512, 512, 1536, 4, 4, 512, 256, 256, 512, 16, 512, 3072)


def _pad_heads(w):
    lead = w.shape[:-1]
    w = w.reshape(lead + (HEADS, C_DK))
    w = jnp.concatenate([w, jnp.zeros_like(w)], axis=-1)
    return w.reshape(lead + (BRANCH_W,))


def _inproj_weight(w_in):
    offs = np.cumsum((0,) + IN_SPLITS)
    (a_q, a_f, a_i, a_g, b_qkv, b_beta, b_decay, b_g, c_q, c_k, c_v, c_gk, c_g, gates) = [
        w_in[:, offs[j]:offs[j + 1]] for j in range(len(IN_SPLITS))]
    small = jnp.concatenate(
        [b_beta, b_decay, c_gk, jnp.zeros((D_MODEL, LANES - SM_GK - C_GK_RANK), w_in.dtype)], axis=-1)
    cols = [gates, a_q, a_i, a_g, b_qkv, b_g, _pad_heads(c_q), _pad_heads(c_k), c_v, c_g, a_f, small]
    return jnp.concatenate(cols, axis=-1).astype(BF16)


def kernel(x, meta_tokens, norm_mix, w_in, hgrn_lb_logits, hgrn_norm, gdn_conv, gdn_a_log,
           gdn_dt_bias, gdn_norm, gla_gk_w2, gla_gk_b, gla_norm, w_branch, w_out, norm_ffn,
           router_group_w, router_group_b, router_expert_w, router_expert_b,
           expert_w_gate, expert_w_up, expert_w_down, norm_final):
    batch, seq, d = x.shape
    assert d == D_MODEL and seq % R == 0
    depth = w_in.shape[0]
    nb = seq // R
    lv_np, wl_np = _level_constants()
    lv = jnp.asarray(lv_np)
    wl = jnp.asarray(wl_np, dtype=BF16)

    h = jnp.concatenate([jnp.zeros((R - N_META, d), F32), meta_tokens.astype(F32),
                         x.reshape(batch * seq, d)], axis=0)

    lb_p = jax.nn.softmax(hgrn_lb_logits.astype(F32), axis=0)
    lb_all = jnp.maximum(jnp.cumsum(lb_p, axis=0) - lb_p[0:1], 0.0)

    for layer in range(depth):
        p16, p32 = _inproj(h, norm_mix[layer][None, :], _inproj_weight(w_in[layer]))

        lb = lb_all[layer]
        lb_rows = jnp.zeros((8, BRANCH_W), F32)
        lb_rows = lb_rows.at[0].set(lb).at[1].set(jnp.log(jnp.maximum(lb, LB_FLOOR))).at[2].set(jnp.log1p(-lb))
        y_a = _hgrn(p16, p32, lb_rows, hgrn_norm[layer][None, :], wl, lv, nb)

        head_params = jnp.zeros((8, LANES), F32)
        head_params = head_params.at[0, SM_DECAY:SM_DECAY + HEADS].set(-jnp.exp(gdn_a_log[layer]))
        head_params = head_params.at[1, SM_DECAY:SM_DECAY + HEADS].set(gdn_dt_bias[layer])
        y_b = _gdn(p16, p32, gdn_conv[layer], head_params, gdn_norm[layer][None, :], wl, lv, nb)

        w2 = jnp.zeros((LANES, BRANCH_W), F32).at[SM_GK:SM_GK + C_GK_RANK].set(_pad_heads(gla_gk_w2[layer]))
        y_c = _gla(p16, p32, w2.astype(BF16), _pad_heads(gla_gk_b[layer])[None, :],
                   gla_norm[layer][None, :], wl, lv, nb)

        wr = jnp.zeros((D_MODEL, LANES), F32)
        wr = wr.at[:, :N_EXPERTS].set(router_expert_w[layer])
        wr = wr.at[:, N_EXPERTS:N_EXPERTS + N_GROUPS].set(router_group_w[layer])
        br = jnp.zeros((1, LANES), F32)
        br = br.at[0, :N_EXPERTS].set(router_expert_b[layer])
        br = br.at[0, N_EXPERTS:N_EXPERTS + N_GROUPS].set(router_group_b[layer])
        wr_hi = wr.astype(BF16)
        wr_lo = (wr - wr_hi.astype(F32)).astype(BF16)
        hn, u2, cmb = _merge(y_a, y_b, y_c, p16, h, w_branch[layer].astype(BF16), w_out[layer].astype(BF16),
                             norm_ffn[layer][None, :], wr_hi, wr_lo, br)

        h = _moe(u2, cmb, hn, expert_w_gate[layer].astype(BF16), expert_w_up[layer].astype(BF16),
                 expert_w_down[layer].astype(BF16))

    out = _final_norm(h, norm_final[None, :])
    return out.reshape(batch, seq, d)
```

```python
import functools
import math

import numpy as np
import jax
import jax.numpy as jnp
from jax import lax
from jax.experimental import pallas as pl
from jax.experimental.pallas import tpu as pltpu

F32 = jnp.float32
BF16 = jnp.bfloat16

D_MODEL = 1024
N_META = 16
CONV_W = 4
RMS_EPS = 1e-6
L2_EPS = 1e-6
LB_FLOOR = 1e-30
HEADS = 4
HEAD_W = 128
BRANCH_W = HEADS * HEAD_W
C_DK = 64
C_GK_RANK = 16
C_GK_NORM = 16.0
N_GROUPS = 4
EXP_PER_GROUP = 8
N_EXPERTS = N_GROUPS * EXP_PER_GROUP
EXPERT_HIDDEN = 256

R = 256
N_LEVELS = 8
LV_DIAG = N_LEVELS
W_CUM = N_LEVELS
W_SFX = N_LEVELS + 1
LANES = 128

P16_GATES = 0
P16_AQ, P16_AI, P16_AG = 6, 7, 8
P16_BQ, P16_BK, P16_BV, P16_BG = 9, 10, 11, 12
P16_CQ, P16_CK, P16_CV, P16_CG = 13, 14, 15, 16
P16_COLS = 17 * BRANCH_W
P32_AF = 0
P32_SMALL_COL = BRANCH_W
P32_COLS = BRANCH_W + LANES
SM_BETA, SM_DECAY, SM_GK = 0, 4, 8

VMEM_LIMIT = 56 * 1024 * 1024
MOE_W = 1280
MOE_CAP = 384
CMB_GROUP_LANE = N_EXPERTS


def _cparams(sem):
    return pltpu.CompilerParams(dimension_semantics=sem, vmem_limit_bytes=VMEM_LIMIT)


def _const_spec(shape):
    nd = len(shape)
    return pl.BlockSpec(shape, lambda *_: (0,) * nd)


@functools.lru_cache(maxsize=None)
def _level_constants():
    t = np.arange(R)[:, None]
    s = np.arange(R)[None, :]
    x = np.maximum(t ^ s, 1)
    lv = np.where(s < t, np.floor(np.log2(x)).astype(np.int32), np.where(s == t, LV_DIAG, -1)).astype(np.int32)
    w = np.zeros((N_LEVELS + 2, R, R), np.float32)
    for l in range(N_LEVELS):
        hsz = 1 << l
        for r in range(R):
            hb = (r // hsz) * hsz
            if (r >> l) & 1:
                w[l, r, hb:r + 1] = 1.0
            else:
                w[l, r, r + 1:hb + hsz] = 1.0
    w[W_CUM] = np.tril(np.ones((R, R), np.float32))
    w[W_SFX] = np.triu(np.ones((R, R), np.float32), 1)
    return lv, w


def _dot(a, b):
    return jnp.dot(a, b, preferred_element_type=F32)


def _dot_nt(a, b):
    return lax.dot_general(a, b, (((1,), (1,)), ((), ())), preferred_element_type=F32)


def _dot_tn(a, b):
    return lax.dot_general(a, b, (((0,), (0,)), ((), ())), preferred_element_type=F32)


def _sigmoid(x):
    return 1.0 / (1.0 + jnp.exp(-x))


def _silu(x):
    return x * _sigmoid(x)


def _softplus(x):
    return jnp.maximum(x, 0.0) + jnp.log1p(jnp.exp(-jnp.abs(x)))


def _log_sigmoid(x):
    return -_softplus(-x)


def _valid_rows(i):
    row = lax.broadcasted_iota(jnp.int32, (R, 1), 0)
    first_valid = jnp.where(i > 0, 0, R - N_META)
    return jnp.where(row >= first_valid, 1.0, 0.0).astype(F32)


def _inproj_kernel(h_ref, nw_ref, w_ref, o16_ref, o32_ref):
    i = pl.program_id(0)
    x = h_ref[...]
    y = x * lax.rsqrt(jnp.mean(x * x, axis=-1, keepdims=True) + RMS_EPS) * nw_ref[...]
    u = (y * _valid_rows(i)).astype(BF16)
    for c in range(P16_COLS // BRANCH_W):
        sl = slice(c * BRANCH_W, (c + 1) * BRANCH_W)
        o16_ref[:, sl] = _dot(u, w_ref[:, sl]).astype(BF16)
    o32_ref[:, :BRANCH_W] = _dot(u, w_ref[:, P16_COLS:P16_COLS + BRANCH_W])
    o32_ref[:, BRANCH_W:] = _dot(u, w_ref[:, P16_COLS + BRANCH_W:])


def _inproj(h, norm_w, w):
    t = h.shape[0]
    return pl.pallas_call(
        _inproj_kernel,
        grid=(t // R,),
        in_specs=[pl.BlockSpec((R, D_MODEL), lambda i: (i, 0)),
                  _const_spec((1, D_MODEL)),
                  _const_spec((D_MODEL, P16_COLS + P32_COLS))],
        out_specs=[pl.BlockSpec((R, P16_COLS), lambda i: (i, 0)),
                   pl.BlockSpec((R, P32_COLS), lambda i: (i, 0))],
        out_shape=[jax.ShapeDtypeStruct((t, P16_COLS), BF16),
                   jax.ShapeDtypeStruct((t, P32_COLS), F32)],
        compiler_params=_cparams(("arbitrary",)),
        name="inproj",
    )(h, norm_w, w)


def _state_in(i, nb, st_ref, stm_ref):
    @pl.when(i == 0)
    def _():
        st_ref[...] = jnp.zeros(st_ref.shape, st_ref.dtype)

    @pl.when((i >= 1) & ((i - 1) % nb == 0))
    def _():
        st_ref[...] = stm_ref[...]


def _state_out(i, st_ref, stm_ref):
    @pl.when(i == 0)
    def _():
        stm_ref[...] = st_ref[...]


def _head_out(o, gate, nw):
    y = o * lax.rsqrt(jnp.mean(o * o, axis=-1, keepdims=True) + RMS_EPS) * nw
    return (y * _silu(gate)).astype(BF16)


def _gla_block(q, k, v, g, gate, nw, wl_ref, lv, st_ref, p_ref, y_ref):
    g16 = g.astype(BF16)
    rowi = lax.broadcasted_iota(jnp.int32, (R, 1), 0)
    for lvl in range(N_LEVELS):
        f = jnp.exp(_dot(wl_ref[lvl], g16))
        lower = ((rowi >> lvl) & 1) == 1
        z = (jnp.where(lower, q, k) * f).astype(BF16)
        hsz = 1 << lvl
        if 4 * hsz >= R:
            for h in range(HEADS):
                zh = z[:, h * HEAD_W:(h + 1) * HEAD_W]
                for r0 in range(0, R, 2 * hsz):
                    p_ref[h, r0 + hsz:r0 + 2 * hsz, r0:r0 + hsz] = _dot_nt(
                        zh[r0 + hsz:r0 + 2 * hsz], zh[r0:r0 + hsz])
        else:
            m = lv == lvl
            for h in range(HEADS):
                zh = z[:, h * HEAD_W:(h + 1) * HEAD_W]
                full = _dot_nt(zh, zh)
                if lvl == 0:
                    p_ref[h] = jnp.where(m, full, 0.0)
                else:
                    p_ref[h] = jnp.where(m, full, p_ref[h])
    q16 = q.astype(BF16)
    k16 = k.astype(BF16)
    m = lv == LV_DIAG
    for h in range(HEADS):
        sl = slice(h * HEAD_W, (h + 1) * HEAD_W)
        p_ref[h] = jnp.where(m, _dot_nt(q16[:, sl], k16[:, sl]), p_ref[h])
    b = _dot(wl_ref[W_CUM], g16)
    sfx = _dot(wl_ref[W_SFX], g16)
    qe = (q * jnp.exp(b)).astype(BF16)
    ke = (k * jnp.exp(sfx)).astype(BF16)
    dec = jnp.exp(b[R - 1:R, :])
    v16 = v.astype(BF16)
    for h in range(HEADS):
        sl = slice(h * HEAD_W, (h + 1) * HEAD_W)
        st = st_ref[h]
        o = _dot(p_ref[h].astype(BF16), v16[:, sl]) + _dot_nt(qe[:, sl], st.astype(BF16))
        st_ref[h] = st * dec[:, sl] + _dot_tn(v16[:, sl], ke[:, sl])
        y_ref[:, sl] = _head_out(o, gate[:, sl], nw)


def _mixer_scratch():
    return [pltpu.VMEM((HEADS, HEAD_W, HEAD_W), F32),
            pltpu.VMEM((HEADS, HEAD_W, HEAD_W), F32),
            pltpu.VMEM((HEADS, R, R), F32)]


def _col_spec(c, width=BRANCH_W):
    return pl.BlockSpec((R, width), lambda i: (i, c))


def _hgrn_kernel(nb, aq_ref, af_ref, ai_ref, ag_ref, lb_ref, nw_ref, wl_ref, lv_ref,
                 y_ref, st_ref, stm_ref, p_ref):
    i = pl.program_id(0)
    _state_in(i, nb, st_ref, stm_ref)
    valid = _valid_rows(i)
    f_in = af_ref[...]
    lb_floor = lb_ref[0:1, :]
    one_m_lb = lb_ref[1:2, :]
    q = _silu(aq_ref[...].astype(F32)) * (HEAD_W ** -0.5)
    e = jnp.exp(-jnp.abs(f_in))
    r = 1.0 / (1.0 + e)
    pos = f_in >= 0.0
    g = jnp.log(lb_floor + one_m_lb * jnp.where(pos, r, e * r))
    k = one_m_lb * jnp.where(pos, e * r, r) * valid
    _gla_block(q, k, ai_ref[...].astype(F32), g, ag_ref[...].astype(F32), nw_ref[...],
               wl_ref, lv_ref[...], st_ref, p_ref, y_ref)
    _state_out(i, st_ref, stm_ref)


def _hgrn(p16, p32, lb_rows, norm_w, wl, lv, nb):
    t = p16.shape[0]
    return pl.pallas_call(
        functools.partial(_hgrn_kernel, nb),
        grid=(t // R,),
        in_specs=[_col_spec(P16_AQ), _col_spec(P32_AF), _col_spec(P16_AI), _col_spec(P16_AG),
                  _const_spec((8, BRANCH_W)), _const_spec((1, HEAD_W)),
                  _const_spec(wl.shape), _const_spec((R, R))],
        out_specs=pl.BlockSpec((R, BRANCH_W), lambda i: (i, 0)),
        out_shape=jax.ShapeDtypeStruct((t, BRANCH_W), BF16),
        scratch_shapes=_mixer_scratch(),
        compiler_params=_cparams(("arbitrary",)),
        name="hgrn2",
    )(p16, p32, p16, p16, lb_rows, norm_w, wl, lv)


def _gla_kernel(nb, cq_ref, ck_ref, cv_ref, cg_ref, sm_ref, w2_ref, b2_ref, nw_ref, wl_ref, lv_ref,
                y_ref, st_ref, stm_ref, p_ref):
    i = pl.program_id(0)
    _state_in(i, nb, st_ref, stm_ref)
    valid = _valid_rows(i)
    z = _dot(sm_ref[...].astype(BF16), w2_ref[...]) + b2_ref[...]
    g = _log_sigmoid(z) * (1.0 / C_GK_NORM)
    q = cq_ref[...].astype(F32) * (C_DK ** -0.5)
    k = ck_ref[...].astype(F32) * valid
    _gla_block(q, k, cv_ref[...].astype(F32), g, cg_ref[...].astype(F32), nw_ref[...],
               wl_ref, lv_ref[...], st_ref, p_ref, y_ref)
    _state_out(i, st_ref, stm_ref)


def _gla(p16, p32, w2, b2, norm_w, wl, lv, nb):
    t = p16.shape[0]
    return pl.pallas_call(
        functools.partial(_gla_kernel, nb),
        grid=(t // R,),
        in_specs=[_col_spec(P16_CQ), _col_spec(P16_CK), _col_spec(P16_CV), _col_spec(P16_CG),
                  _col_spec(P32_SMALL_COL // LANES, LANES),
                  _const_spec((LANES, BRANCH_W)), _const_spec((1, BRANCH_W)), _const_spec((1, HEAD_W)),
                  _const_spec(wl.shape), _const_spec((R, R))],
        out_specs=pl.BlockSpec((R, BRANCH_W), lambda i: (i, 0)),
        out_shape=jax.ShapeDtypeStruct((t, BRANCH_W), BF16),
        scratch_shapes=_mixer_scratch(),
        compiler_params=_cparams(("arbitrary",)),
        name="gla",
    )(p16, p16, p16, p16, p32, w2, b2, norm_w, wl, lv)


TAIL = 8


def _gdn_kernel(nb, bq_ref, bk_ref, bv_ref, bg_ref, sm_ref, cw_ref, hp_ref, nw_ref, wl_ref, lv_ref,
                y_ref, st_ref, stm_ref, xx_ref, tailm_ref, x_ref, a_ref):
    i = pl.program_id(0)
    _state_in(i, nb, st_ref, stm_ref)
    valid = _valid_rows(i)
    lv = lv_ref[...]

    @pl.when(i == 0)
    def _():
        xx_ref[0:TAIL, :] = jnp.zeros((TAIL, 3 * BRANCH_W), F32)

    @pl.when((i >= 1) & ((i - 1) % nb == 0))
    def _():
        xx_ref[0:TAIL, :] = tailm_ref[...]

    for c, ref in enumerate((bq_ref, bk_ref, bv_ref)):
        xx_ref[TAIL:, c * BRANCH_W:(c + 1) * BRANCH_W] = ref[...].astype(F32)
    conv = jnp.zeros((R, 3 * BRANCH_W), F32)
    for j in range(CONV_W):
        off = TAIL - (CONV_W - 1) + j
        conv = conv + xx_ref[off:off + R, :] * cw_ref[j:j + 1, :]
    qkv = _silu(conv)
    new_tail = xx_ref[R:R + TAIL, :]
    xx_ref[0:TAIL, :] = new_tail

    @pl.when(i == 0)
    def _():
        tailm_ref[...] = new_tail

    sm = sm_ref[...]
    a_neg = hp_ref[0:1, :]
    dt_b = hp_ref[1:2, :]
    log_a = a_neg * _softplus(sm + dt_b)
    la_hi = log_a.astype(BF16)
    la_lo = (log_a - la_hi.astype(F32)).astype(BF16)
    gcum = _dot(wl_ref[W_CUM], la_hi) + _dot(wl_ref[W_CUM], la_lo)
    gcum_t = gcum.T
    beta_all = _sigmoid(sm) * valid

    strict = (lv >= 0) & (lv < LV_DIAG)
    causal = lv >= 0
    eye = (lv == LV_DIAG).astype(F32)

    heads = []
    for h in range(HEADS):
        qh = qkv[:, h * HEAD_W:(h + 1) * HEAD_W]
        kh = qkv[:, BRANCH_W + h * HEAD_W:BRANCH_W + (h + 1) * HEAD_W] * valid
        vh = qkv[:, 2 * BRANCH_W + h * HEAD_W:2 * BRANCH_W + (h + 1) * HEAD_W]
        qh = qh * lax.rsqrt(jnp.sum(qh * qh, axis=-1, keepdims=True) + L2_EPS) * (HEAD_W ** -0.5)
        kh = kh * lax.rsqrt(jnp.sum(kh * kh, axis=-1, keepdims=True) + L2_EPS)
        beta = beta_all[:, SM_BETA + h:SM_BETA + h + 1]
        gcol = gcum[:, SM_DECAY + h:SM_DECAY + h + 1]
        grow = gcum_t[SM_DECAY + h:SM_DECAY + h + 1, :]
        dm = jnp.exp(jnp.minimum(gcol - grow, 0.0))
        q16 = qh.astype(BF16)
        k16 = kh.astype(BF16)
        a_ref[h] = jnp.where(strict, beta * _dot_nt(k16, k16) * dm, 0.0)
        qk = jnp.where(causal, _dot_nt(q16, k16) * dm, 0.0).astype(BF16)
        heads.append((qh, kh, vh, beta, gcol, qk))

    for lvl in range(N_LEVELS):
        m = lv == lvl
        for h in range(HEADS):
            l_lvl = jnp.where(m, a_ref[h], 0.0)
            if lvl == 0:
                x_ref[h] = eye - l_lvl
            else:
                xcur = x_ref[h].astype(BF16)
                x_ref[h] = x_ref[h] - _dot(_dot(xcur, l_lvl.astype(BF16)).astype(BF16), xcur)

    for h, (qh, kh, vh, beta, gcol, qk) in enumerate(heads):
        sl = slice(h * HEAD_W, (h + 1) * HEAD_W)
        eg = jnp.exp(gcol)
        rhs = jnp.concatenate([vh * beta, kh * (beta * eg)], axis=-1).astype(BF16)
        sol = _dot(x_ref[h].astype(BF16), rhs)
        u = sol[:, :HEAD_W]
        w = sol[:, HEAD_W:]
        st = st_ref[h]
        st16 = st.astype(BF16)
        v_new = u - _dot_nt(w.astype(BF16), st16)
        vn16 = v_new.astype(BF16)
        o = _dot_nt((qh * eg).astype(BF16), st16) + _dot(qk, vn16)
        g_last = gcol[R - 1:R, :]
        ke = (kh * jnp.exp(g_last - gcol)).astype(BF16)
        st_ref[h] = st * jnp.exp(g_last) + _dot_tn(vn16, ke)
        y_ref[:, sl] = _head_out(o, bg_ref[:, sl].astype(F32), nw_ref[...])
    _state_out(i, st_ref, stm_ref)


def _gdn(p16, p32, conv_w, head_params, norm_w, wl, lv, nb):
    t = p16.shape[0]
    return pl.pallas_call(
        functools.partial(_gdn_kernel, nb),
        grid=(t // R,),
        in_specs=[_col_spec(P16_BQ), _col_spec(P16_BK), _col_spec(P16_BV), _col_spec(P16_BG),
                  _col_spec(P32_SMALL_COL // LANES, LANES),
                  _const_spec((CONV_W, 3 * BRANCH_W)), _const_spec((8, LANES)), _const_spec((1, HEAD_W)),
                  _const_spec(wl.shape), _const_spec((R, R))],
        out_specs=pl.BlockSpec((R, BRANCH_W), lambda i: (i, 0)),
        out_shape=jax.ShapeDtypeStruct((t, BRANCH_W), BF16),
        scratch_shapes=[pltpu.VMEM((HEADS, HEAD_W, HEAD_W), F32),
                        pltpu.VMEM((HEADS, HEAD_W, HEAD_W), F32),
                        pltpu.VMEM((R + TAIL, 3 * BRANCH_W), F32),
                        pltpu.VMEM((TAIL, 3 * BRANCH_W), F32),
                        pltpu.VMEM((HEADS, R, R), F32),
                        pltpu.VMEM((HEADS, R, R), F32)],
        compiler_params=_cparams(("arbitrary",)),
        name="gdn",
    )(p16, p16, p16, p16, p32, conv_w, head_params, norm_w, wl, lv)


def _split_hi_lo(x):
    hi = x.astype(BF16)
    return hi, (x - hi.astype(F32)).astype(BF16)


def _merge_kernel(ya_ref, yb_ref, yc_ref, g0_ref, g1_ref, g2_ref, h_ref, wb_ref, wo_ref, nf_ref,
                  wr_hi_ref, wr_lo_ref, br_ref, hn_ref, u2_ref, cmb_ref):
    merged = jnp.zeros((R, D_MODEL), F32)
    for n, (y_ref, g_ref) in enumerate(((ya_ref, g0_ref), (yb_ref, g1_ref), (yc_ref, g2_ref))):
        merged = merged + _sigmoid(g_ref[...].astype(F32)) * _dot(y_ref[...], wb_ref[n])
    hn = h_ref[...] + _dot(merged.astype(BF16), wo_ref[...])
    hn_ref[...] = hn
    u2 = hn * lax.rsqrt(jnp.mean(hn * hn, axis=-1, keepdims=True) + RMS_EPS) * nf_ref[...]
    u2_ref[...] = u2.astype(BF16)

    u_hi, u_lo = _split_hi_lo(u2)
    logits = (_dot(u_hi, wr_hi_ref[...]) + _dot(u_hi, wr_lo_ref[...]) + _dot(u_lo, wr_hi_ref[...])
              + br_ref[...])
    lane_i = lax.broadcasted_iota(jnp.int32, (R, LANES), 1)
    lane = lane_i.astype(F32)
    lane_grp = (lane_i // EXP_PER_GROUP).astype(F32)
    neg = jnp.float32(-jnp.inf)
    big = jnp.float32(1e9)
    is_g = (lane_i >= N_EXPERTS) & (lane_i < N_EXPERTS + N_GROUPS)
    lg = jnp.where(is_g, logits, neg)
    mg = jnp.max(lg, axis=-1, keepdims=True)
    zg = jnp.sum(jnp.exp(lg - mg), axis=-1, keepdims=True)
    g_val = 1.0 / zg
    g_idx = jnp.min(jnp.where(lg == mg, lane, big), axis=-1, keepdims=True) - N_EXPERTS
    in_grp = (lane_i < N_EXPERTS) & (lane_grp == g_idx)
    le = jnp.where(in_grp, logits, neg)
    m1 = jnp.max(le, axis=-1, keepdims=True)
    ze = jnp.sum(jnp.exp(le - m1), axis=-1, keepdims=True)
    i1 = jnp.min(jnp.where(le == m1, lane, big), axis=-1, keepdims=True)
    le2 = jnp.where(lane == i1, neg, le)
    m2 = jnp.max(le2, axis=-1, keepdims=True)
    i2 = jnp.min(jnp.where(le2 == m2, lane, big), axis=-1, keepdims=True)
    p1 = 1.0 / ze
    p2 = jnp.exp(m2 - m1) / ze
    den = p1 + p2
    cmb = g_val * jnp.where(lane == i1, p1 / den, jnp.where(lane == i2, p2 / den, 0.0))
    cmb_ref[...] = jnp.where(lane_i == CMB_GROUP_LANE, g_idx, cmb)


def _merge(ya, yb, yc, p16, h, wb, wo, nf, wr_hi, wr_lo, br):
    t = h.shape[0]
    row = lambda w: pl.BlockSpec((R, w), lambda i: (i, 0))
    gate = lambda n: pl.BlockSpec((R, D_MODEL), lambda i: (i, n))
    return pl.pallas_call(
        _merge_kernel,
        grid=(t // R,),
        in_specs=[row(BRANCH_W), row(BRANCH_W), row(BRANCH_W), gate(0), gate(1), gate(2), row(D_MODEL),
                  _const_spec((3, BRANCH_W, D_MODEL)), _const_spec((D_MODEL, D_MODEL)),
                  _const_spec((1, D_MODEL)), _const_spec((D_MODEL, LANES)), _const_spec((D_MODEL, LANES)),
                  _const_spec((1, LANES))],
        out_specs=[row(D_MODEL), row(D_MODEL), row(LANES)],
        out_shape=[jax.ShapeDtypeStruct((t, D_MODEL), F32),
                   jax.ShapeDtypeStruct((t, D_MODEL), BF16),
                   jax.ShapeDtypeStruct((t, LANES), F32)],
        compiler_params=_cparams(("arbitrary",)),
        name="merge_router",
    )(ya, yb, yc, p16, p16, p16, h, wb, wo, nf, wr_hi, wr_lo, br)


def _moe_kernel(cnt_ref, u_ref, c_ref, h_ref, wg_ref, wu_ref, wd_ref, tri_ref, o_ref):
    w = pl.program_id(0)
    g = pl.program_id(1)
    u = u_ref[...]
    cmb = c_ref[...]
    gid_row = cmb.T[CMB_GROUP_LANE:CMB_GROUP_LANE + 1, :]
    in_row = jnp.where(gid_row == g.astype(F32), 1.0, 0.0)
    ranks = []
    before = jnp.zeros((1, 1), F32)
    for j in range(MOE_W // R):
        seg = in_row[:, j * R:(j + 1) * R]
        ranks.append(_dot(jnp.broadcast_to(seg, (16, R)).astype(BF16), tri_ref[...])[0:1, :] + before)
        before = before + jnp.sum(seg, axis=-1, keepdims=True)
    rank_row = jnp.concatenate(ranks, axis=-1)
    cmb_hi, cmb_lo = _split_hi_lo(cmb)
    slot = lax.broadcasted_iota(jnp.int32, (MOE_CAP, 1), 0).astype(F32)
    lane = lax.broadcasted_iota(jnp.int32, (MOE_CAP, LANES), 1)

    @pl.when(g == 0)
    def _():
        o_ref[...] = h_ref[...]

    n_chunks = (cnt_ref[w * N_GROUPS + g] + MOE_CAP - 1) // MOE_CAP

    def chunk(c, carry):
        base = (c * MOE_CAP).astype(F32)
        sel = jnp.where((in_row > 0.0) & (rank_row - base == slot), 1.0, 0.0).astype(BF16)
        x = _dot(sel, u).astype(BF16)
        cw = _dot(sel, cmb_hi) + _dot(sel, cmb_lo)
        y = jnp.zeros((MOE_CAP, D_MODEL), F32)
        for e in range(EXP_PER_GROUP):
            ce = jnp.sum(jnp.where(lane == g * EXP_PER_GROUP + e, cw, 0.0), axis=-1, keepdims=True)
            hid = _silu(_dot(x, wg_ref[0, e])) * _dot(x, wu_ref[0, e]) * ce
            y = y + _dot(hid.astype(BF16), wd_ref[0, e])
        o_ref[...] += _dot_tn(sel, y.astype(BF16))
        return carry

    lax.fori_loop(0, n_chunks, chunk, 0)


def _moe(u2, cmb, h, wg, wu, wd):
    t = h.shape[0]
    n_win = t // MOE_W
    assert t % MOE_W == 0
    gid = cmb[:, CMB_GROUP_LANE].astype(jnp.int32).reshape(n_win, MOE_W)
    counts = jnp.sum(gid[:, :, None] == jnp.arange(N_GROUPS, dtype=jnp.int32), axis=1, dtype=jnp.int32)
    tri = jnp.asarray(np.triu(np.ones((R, R), np.float32), 1), dtype=BF16)
    row = lambda width: pl.BlockSpec((MOE_W, width), lambda w, g, cnt: (w, 0))
    wspec = lambda a, b: pl.BlockSpec((1, EXP_PER_GROUP, a, b), lambda w, g, cnt: (g, 0, 0, 0))
    return pl.pallas_call(
        _moe_kernel,
        grid_spec=pltpu.PrefetchScalarGridSpec(
            num_scalar_prefetch=1,
            grid=(n_win, N_GROUPS),
            in_specs=[row(D_MODEL), row(LANES),
                      pl.BlockSpec((MOE_W, D_MODEL), lambda w, g, cnt: (w, 0), pipeline_mode=pl.Buffered(1)),
                      wspec(D_MODEL, EXPERT_HIDDEN), wspec(D_MODEL, EXPERT_HIDDEN),
                      wspec(EXPERT_HIDDEN, D_MODEL),
                      pl.BlockSpec((R, R), lambda w, g, cnt: (0, 0))],
            out_specs=row(D_MODEL)),
        out_shape=jax.ShapeDtypeStruct((t, D_MODEL), F32),
        compiler_params=_cparams(("arbitrary", "arbitrary")),
        name="moe",
    )(counts.reshape(-1), u2, cmb, h, wg, wu, wd, tri)


def _final_norm_kernel(h_ref, nw_ref, o_ref):
    x = h_ref[...]
    o_ref[...] = x * lax.rsqrt(jnp.mean(x * x, axis=-1, keepdims=True) + RMS_EPS) * nw_ref[...]


def _final_norm(h, nw):
    t = h.shape[0]
    return pl.pallas_call(
        _final_norm_kernel,
        grid=(t // R - 1,),
        in_specs=[pl.BlockSpec((R, D_MODEL), lambda i: (i + 1, 0)), _const_spec((1, D_MODEL))],
        out_specs=pl.BlockSpec((R, D_MODEL), lambda i: (i, 0)),
        out_shape=jax.ShapeDtypeStruct((t - R, D_MODEL), F32),
        compiler_params=_cparams(("arbitrary",)),
        name="final_norm",
    )(h, nw)


IN_SPLITS = (512, 512, 512, 512, 1536, 4, 4, 512, 256, 256, 512, 16, 512, 3072)


def _pad_heads(w):
    lead = w.shape[:-1]
    w = w.reshape(lead + (HEADS, C_DK))
    w = jnp.concatenate([w, jnp.zeros_like(w)], axis=-1)
    return w.reshape(lead + (BRANCH_W,))


def _inproj_weight(w_in):
    offs = np.cumsum((0,) + IN_SPLITS)
    (a_q, a_f, a_i, a_g, b_qkv, b_beta, b_decay, b_g, c_q, c_k, c_v, c_gk, c_g, gates) = [
        w_in[:, offs[j]:offs[j + 1]] for j in range(len(IN_SPLITS))]
    small = jnp.concatenate(
        [b_beta, b_decay, c_gk, jnp.zeros((D_MODEL, LANES - SM_GK - C_GK_RANK), w_in.dtype)], axis=-1)
    cols = [gates, a_q, a_i, a_g, b_qkv, b_g, _pad_heads(c_q), _pad_heads(c_k), c_v, c_g, a_f, small]
    return jnp.concatenate(cols, axis=-1).astype(BF16)


def kernel(x, meta_tokens, norm_mix, w_in, hgrn_lb_logits, hgrn_norm, gdn_conv, gdn_a_log,
           gdn_dt_bias, gdn_norm, gla_gk_w2, gla_gk_b, gla_norm, w_branch, w_out, norm_ffn,
           router_group_w, router_group_b, router_expert_w, router_expert_b,
           expert_w_gate, expert_w_up, expert_w_down, norm_final):
    batch, seq, d = x.shape
    assert d == D_MODEL and seq % R == 0
    depth = w_in.shape[0]
    nb = seq // R
    lv_np, wl_np = _level_constants()
    lv = jnp.asarray(lv_np)
    wl = jnp.asarray(wl_np, dtype=BF16)

    h = jnp.concatenate([jnp.zeros((R - N_META, d), F32), meta_tokens.astype(F32),
                         x.reshape(batch * seq, d)], axis=0)

    lb_p = jax.nn.softmax(hgrn_lb_logits.astype(F32), axis=0)
    lb_all = jnp.maximum(jnp.cumsum(lb_p, axis=0) - lb_p[0:1], 0.0)

    for layer in range(depth):
        p16, p32 = _inproj(h, norm_mix[layer][None, :], _inproj_weight(w_in[layer]))

        lb = lb_all[layer]
        lb_rows = jnp.zeros((8, BRANCH_W), F32)
        lb_rows = lb_rows.at[0].set(jnp.maximum(lb, LB_FLOOR)).at[1].set(1.0 - lb)
        y_a = _hgrn(p16, p32, lb_rows, hgrn_norm[layer][None, :], wl, lv, nb)

        head_params = jnp.zeros((8, LANES), F32)
        head_params = head_params.at[0, SM_DECAY:SM_DECAY + HEADS].set(-jnp.exp(gdn_a_log[layer]))
        head_params = head_params.at[1, SM_DECAY:SM_DECAY + HEADS].set(gdn_dt_bias[layer])
        y_b = _gdn(p16, p32, gdn_conv[layer], head_params, gdn_norm[layer][None, :], wl, lv, nb)

        w2 = jnp.zeros((LANES, BRANCH_W), F32).at[SM_GK:SM_GK + C_GK_RANK].set(_pad_heads(gla_gk_w2[layer]))
        y_c = _gla(p16, p32, w2.astype(BF16), _pad_heads(gla_gk_b[layer])[None, :],
                   gla_norm[layer][None, :], wl, lv, nb)

        wr = jnp.zeros((D_MODEL, LANES), F32)
        wr = wr.at[:, :N_EXPERTS].set(router_expert_w[layer])
        wr = wr.at[:, N_EXPERTS:N_EXPERTS + N_GROUPS].set(router_group_w[layer])
        br = jnp.zeros((1, LANES), F32)
        br = br.at[0, :N_EXPERTS].set(router_expert_b[layer])
        br = br.at[0, N_EXPERTS:N_EXPERTS + N_GROUPS].set(router_group_b[layer])
        wr_hi = wr.astype(BF16)
        wr_lo = (wr - wr_hi.astype(F32)).astype(BF16)
        hn, u2, cmb = _merge(y_a, y_b, y_c, p16, h, w_branch[layer].astype(BF16), w_out[layer].astype(BF16),
                             norm_ffn[layer][None, :], wr_hi, wr_lo, br)

        h = _moe(u2, cmb, hn, expert_w_gate[layer].astype(BF16), expert_w_up[layer].astype(BF16),
                 expert_w_down[layer].astype(BF16))

    out = _final_norm(h, norm_final[None, :])
    return out.reshape(batch, seq, d)
```

```python
import functools

import numpy as np
import jax
import jax.numpy as jnp
from jax import lax
from jax.experimental import pallas as pl
from jax.experimental.pallas import tpu as pltpu

F32 = jnp.float32
BF16 = jnp.bfloat16

D_MODEL = 1024
N_META = 16
CONV_W = 4
RMS_EPS = 1e-6
L2_EPS = 1e-6
LB_FLOOR = 1e-30
HEADS = 4
HEAD_W = 128
BRANCH_W = HEADS * HEAD_W
C_DK = 64
C_GK_RANK = 16
C_GK_NORM = 16.0
N_BRANCH = 3
N_GROUPS = 4
EXP_PER_GROUP = 8
N_EXPERTS = N_GROUPS * EXP_PER_GROUP
EXPERT_HIDDEN = 256

R = 256
N_LEVELS = 8
LV_DIAG = N_LEVELS
W_CUM = N_LEVELS
W_SFX = N_LEVELS + 1
LANES = 128

SM_BETA, SM_DECAY, SM_GK = 0, 4, 8

VMEM_LIMIT = 56 * 1024 * 1024
MOE_W = 1280
MOE_CAP = 352
CMB_GROUP_LANE = N_EXPERTS


def _cparams(sem):
    return pltpu.CompilerParams(dimension_semantics=sem, vmem_limit_bytes=VMEM_LIMIT)


def _const_spec(shape):
    nd = len(shape)
    return pl.BlockSpec(shape, lambda *_: (0,) * nd)


@functools.lru_cache(maxsize=None)
def _level_constants():
    t = np.arange(R)[:, None]
    s = np.arange(R)[None, :]
    x = np.maximum(t ^ s, 1)
    lv = np.where(s < t, np.floor(np.log2(x)).astype(np.int32), np.where(s == t, LV_DIAG, -1)).astype(np.int32)
    w = np.zeros((N_LEVELS + 2, R, R), np.float32)
    for l in range(N_LEVELS):
        hsz = 1 << l
        for r in range(R):
            hb = (r // hsz) * hsz
            if (r >> l) & 1:
                w[l, r, hb:r + 1] = 1.0
            else:
                w[l, r, r + 1:hb + hsz] = 1.0
    w[W_CUM] = np.tril(np.ones((R, R), np.float32))
    w[W_SFX] = np.triu(np.ones((R, R), np.float32), 1)
    return lv, w


def _dot(a, b):
    return jnp.dot(a, b, preferred_element_type=F32)


def _dot_nt(a, b):
    return lax.dot_general(a, b, (((1,), (1,)), ((), ())), preferred_element_type=F32)


def _dot_tn(a, b):
    return lax.dot_general(a, b, (((0,), (0,)), ((), ())), preferred_element_type=F32)


def _sigmoid(x):
    return 1.0 / (1.0 + jnp.exp(-x))


def _silu(x):
    return x * _sigmoid(x)


def _softplus(x):
    return jnp.maximum(x, 0.0) + jnp.log1p(jnp.exp(-jnp.abs(x)))


def _log_sigmoid(x):
    return -_softplus(-x)


def _rmsnorm(x, w):
    return x * lax.rsqrt(jnp.mean(x * x, axis=-1, keepdims=True) + RMS_EPS) * w


def _valid_rows(is_meta):
    row = lax.broadcasted_iota(jnp.int32, (R, 1), 0)
    first_valid = jnp.where(is_meta, R - N_META, 0)
    return jnp.where(row >= first_valid, 1.0, 0.0).astype(F32)


def _mixer_input(h_ref, nw_ref, valid):
    return (_rmsnorm(h_ref[...], nw_ref[...]) * valid).astype(BF16)


PROJ_CHUNK = 256


def _projection_steps(h_ref, nm_ref, w_ref, out_ref, is_meta):
    u = _mixer_input(h_ref, nm_ref, _valid_rows(is_meta))
    n_cols = w_ref.shape[1]

    def step(c0):
        sl = slice(c0, min(c0 + PROJ_CHUNK, n_cols))
        out_ref[:, sl] = _dot(u, w_ref[:, sl])

    return [functools.partial(step, c0) for c0 in range(0, n_cols, PROJ_CHUNK)]


def _run_one(steps):
    if steps:
        steps.pop(0)()


def _run_all(steps):
    while steps:
        steps.pop(0)()


def _seq_block(n_blocks, lag):
    return lambda i: ((jnp.clip(i - lag, 0, n_blocks - 1) + n_blocks - 1) % n_blocks, 0)


def _state_in(j, nb, st_ref, stm_ref):
    @pl.when(j <= 0)
    def _():
        st_ref[...] = jnp.zeros(st_ref.shape, st_ref.dtype)

    @pl.when((j >= 1) & ((j - 1) % nb == 0))
    def _():
        st_ref[...] = stm_ref[...]


def _state_out(j, st_ref, stm_ref):
    @pl.when(j == 0)
    def _():
        stm_ref[...] = st_ref[...]


def _skewed(i, pa_ref, pb_ref, body):
    @pl.when(i == 0)
    def _():
        pb_ref[...] = jnp.zeros(pb_ref.shape, pb_ref.dtype)

    @pl.when(i % 2 == 0)
    def _():
        body(pb_ref, pa_ref)

    @pl.when(i % 2 == 1)
    def _():
        body(pa_ref, pb_ref)


def _head_out(o, gate, nw):
    return (_rmsnorm(o, nw) * _silu(gate)).astype(BF16)


def _gla_block(q, k, v, g, gate, nw, wl_ref, lv, st_ref, p_ref, y_ref, fill):
    g16 = g.astype(BF16)
    rowi = lax.broadcasted_iota(jnp.int32, (R, 1), 0)
    for lvl in range(N_LEVELS):
        f = jnp.exp(_dot(wl_ref[lvl], g16))
        lower = ((rowi >> lvl) & 1) == 1
        z = (jnp.where(lower, q, k) * f).astype(BF16)
        hsz = 1 << lvl
        if 4 * hsz >= R:
            for h in range(HEADS):
                zh = z[:, h * HEAD_W:(h + 1) * HEAD_W]
                for r0 in range(0, R, 2 * hsz):
                    p_ref[h, r0 + hsz:r0 + 2 * hsz, r0:r0 + hsz] = _dot_nt(
                        zh[r0 + hsz:r0 + 2 * hsz], zh[r0:r0 + hsz])
        else:
            m = lv == lvl
            for h in range(HEADS):
                zh = z[:, h * HEAD_W:(h + 1) * HEAD_W]
                full = _dot_nt(zh, zh)
                if lvl == 0:
                    p_ref[h] = jnp.where(m, full, 0.0)
                else:
                    p_ref[h] = jnp.where(m, full, p_ref[h])
        _run_one(fill)
    q16 = q.astype(BF16)
    k16 = k.astype(BF16)
    m = lv == LV_DIAG
    for h in range(HEADS):
        sl = slice(h * HEAD_W, (h + 1) * HEAD_W)
        p_ref[h] = jnp.where(m, _dot_nt(q16[:, sl], k16[:, sl]), p_ref[h])
    b = _dot(wl_ref[W_CUM], g16)
    sfx = _dot(wl_ref[W_SFX], g16)
    qe = (q * jnp.exp(b)).astype(BF16)
    ke = (k * jnp.exp(sfx)).astype(BF16)
    dec = jnp.exp(b[R - 1:R, :])
    v16 = v.astype(BF16)
    for h in range(HEADS):
        sl = slice(h * HEAD_W, (h + 1) * HEAD_W)
        st = st_ref[h]
        o = _dot(p_ref[h].astype(BF16), v16[:, sl]) + _dot_nt(qe[:, sl], st.astype(BF16))
        st_ref[h] = st * dec[:, sl] + _dot_tn(v16[:, sl], ke[:, sl])
        y_ref[:, sl] = _head_out(o, gate[:, sl], nw)


def _mixer_call(kernel_fn, name, h, operands, operand_specs, proj_cols, scratch):
    t = h.shape[0]
    n_blocks = t // R
    return pl.pallas_call(
        kernel_fn,
        grid=(n_blocks + 1,),
        in_specs=[pl.BlockSpec((R, D_MODEL), _seq_block(n_blocks, 0))] + operand_specs,
        out_specs=pl.BlockSpec((R, BRANCH_W), _seq_block(n_blocks, 1)),
        out_shape=jax.ShapeDtypeStruct((t, BRANCH_W), BF16),
        scratch_shapes=scratch + [pltpu.VMEM((R, proj_cols), F32), pltpu.VMEM((R, proj_cols), F32)],
        compiler_params=_cparams(("arbitrary",)),
        name=name,
    )(h, *operands)


def _state_scratch():
    return [pltpu.VMEM((HEADS, HEAD_W, HEAD_W), F32), pltpu.VMEM((HEADS, HEAD_W, HEAD_W), F32)]


def _hgrn_kernel(nb, h_ref, nm_ref, w_ref, lb_ref, nw_ref, wl_ref, lv_ref,
                 y_ref, st_ref, stm_ref, p_ref, pa_ref, pb_ref):
    i = pl.program_id(0)
    j = i - 1
    _state_in(j, nb, st_ref, stm_ref)

    def body(rd, wr):
        a_q, f_in, a_i, a_g = [rd[:, c * BRANCH_W:(c + 1) * BRANCH_W] for c in range(4)]
        valid = _valid_rows(j == 0)
        lb_floor = lb_ref[0:1, :]
        one_m_lb = lb_ref[1:2, :]
        q = _silu(a_q) * (HEAD_W ** -0.5)
        e = jnp.exp(-jnp.abs(f_in))
        r = 1.0 / (1.0 + e)
        pos = f_in >= 0.0
        g = jnp.log(lb_floor + one_m_lb * jnp.where(pos, r, e * r))
        k = one_m_lb * jnp.where(pos, e * r, r) * valid
        fill = _projection_steps(h_ref, nm_ref, w_ref, wr, i == 0)
        _gla_block(q, k, a_i, g, a_g, nw_ref[...], wl_ref, lv_ref[...], st_ref, p_ref, y_ref, fill)
        _run_all(fill)

    _skewed(i, pa_ref, pb_ref, body)
    _state_out(j, st_ref, stm_ref)


def _hgrn(h, norm_mix, w, lb_rows, norm_w, wl, lv, nb):
    return _mixer_call(
        functools.partial(_hgrn_kernel, nb), "hgrn2", h,
        [norm_mix, w, lb_rows, norm_w, wl, lv],
        [_const_spec((1, D_MODEL)), _const_spec(w.shape), _const_spec((8, BRANCH_W)),
         _const_spec((1, HEAD_W)), _const_spec(wl.shape), _const_spec((R, R))],
        w.shape[1], _state_scratch() + [pltpu.VMEM((HEADS, R, R), F32)])


def _gla_kernel(nb, h_ref, nm_ref, w_ref, w2_ref, b2_ref, nw_ref, wl_ref, lv_ref,
                y_ref, st_ref, stm_ref, p_ref, pa_ref, pb_ref):
    i = pl.program_id(0)
    j = i - 1
    _state_in(j, nb, st_ref, stm_ref)

    def body(rd, wr):
        c_q, c_k, c_v, c_g = [rd[:, c * BRANCH_W:(c + 1) * BRANCH_W] for c in range(4)]
        gk_low = rd[:, 4 * BRANCH_W:]
        z = _dot(gk_low.astype(BF16), w2_ref[...]) + b2_ref[...]
        g = _log_sigmoid(z) * (1.0 / C_GK_NORM)
        fill = _projection_steps(h_ref, nm_ref, w_ref, wr, i == 0)
        _gla_block(c_q * (C_DK ** -0.5), c_k * _valid_rows(j == 0), c_v, g, c_g, nw_ref[...],
                   wl_ref, lv_ref[...], st_ref, p_ref, y_ref, fill)
        _run_all(fill)

    _skewed(i, pa_ref, pb_ref, body)
    _state_out(j, st_ref, stm_ref)


def _gla(h, norm_mix, w, w2, b2, norm_w, wl, lv, nb):
    return _mixer_call(
        functools.partial(_gla_kernel, nb), "gla", h,
        [norm_mix, w, w2, b2, norm_w, wl, lv],
        [_const_spec((1, D_MODEL)), _const_spec(w.shape), _const_spec((LANES, BRANCH_W)),
         _const_spec((1, BRANCH_W)), _const_spec((1, HEAD_W)), _const_spec(wl.shape), _const_spec((R, R))],
        w.shape[1], _state_scratch() + [pltpu.VMEM((HEADS, R, R), F32)])


TAIL = 8


def _gdn_kernel(nb, h_ref, nm_ref, w_ref, cw_ref, hp_ref, nw_ref, wl_ref, lv_ref,
                y_ref, st_ref, stm_ref, xx_ref, tailm_ref, x_ref, a_ref, pa_ref, pb_ref):
    i = pl.program_id(0)
    j = i - 1
    _state_in(j, nb, st_ref, stm_ref)

    @pl.when(j <= 0)
    def _():
        xx_ref[0:TAIL, :] = jnp.zeros((TAIL, 3 * BRANCH_W), F32)

    @pl.when((j >= 1) & ((j - 1) % nb == 0))
    def _():
        xx_ref[0:TAIL, :] = tailm_ref[...]

    def body(rd, wr):
        _gdn_body(i, j, rd, wr, h_ref, nm_ref, w_ref, cw_ref, hp_ref, nw_ref, wl_ref, lv_ref,
                  y_ref, st_ref, xx_ref, x_ref, a_ref)

    _skewed(i, pa_ref, pb_ref, body)

    @pl.when(j == 0)
    def _():
        tailm_ref[...] = xx_ref[0:TAIL, :]

    _state_out(j, st_ref, stm_ref)


def _gdn_body(i, j, rd, wr, h_ref, nm_ref, w_ref, cw_ref, hp_ref, nw_ref, wl_ref, lv_ref,
              y_ref, st_ref, xx_ref, x_ref, a_ref):
    fill = _projection_steps(h_ref, nm_ref, w_ref, wr, i == 0)
    valid = _valid_rows(j == 0)
    lv = lv_ref[...]
    xx_ref[TAIL:, :] = rd[:, :3 * BRANCH_W]
    b_g = rd[:, 3 * BRANCH_W:4 * BRANCH_W]
    sm = rd[:, 4 * BRANCH_W:]
    conv = jnp.zeros((R, 3 * BRANCH_W), F32)
    for tap in range(CONV_W):
        off = TAIL - (CONV_W - 1) + tap
        conv = conv + xx_ref[off:off + R, :] * cw_ref[tap:tap + 1, :]
    qkv = _silu(conv)
    xx_ref[0:TAIL, :] = xx_ref[R:R + TAIL, :]

    a_neg = hp_ref[0:1, :]
    dt_b = hp_ref[1:2, :]
    log_a = a_neg * _softplus(sm + dt_b)
    la_hi = log_a.astype(BF16)
    la_lo = (log_a - la_hi.astype(F32)).astype(BF16)
    gcum = _dot(wl_ref[W_CUM], la_hi) + _dot(wl_ref[W_CUM], la_lo)
    gcum_t = gcum.T
    beta_all = _sigmoid(sm) * valid

    strict = (lv >= 0) & (lv < LV_DIAG)
    causal = lv >= 0
    eye = (lv == LV_DIAG).astype(F32)

    heads = []
    for h in range(HEADS):
        qh = qkv[:, h * HEAD_W:(h + 1) * HEAD_W]
        kh = qkv[:, BRANCH_W + h * HEAD_W:BRANCH_W + (h + 1) * HEAD_W] * valid
        vh = qkv[:, 2 * BRANCH_W + h * HEAD_W:2 * BRANCH_W + (h + 1) * HEAD_W]
        qh = qh * lax.rsqrt(jnp.sum(qh * qh, axis=-1, keepdims=True) + L2_EPS) * (HEAD_W ** -0.5)
        kh = kh * lax.rsqrt(jnp.sum(kh * kh, axis=-1, keepdims=True) + L2_EPS)
        beta = beta_all[:, SM_BETA + h:SM_BETA + h + 1]
        gcol = gcum[:, SM_DECAY + h:SM_DECAY + h + 1]
        grow = gcum_t[SM_DECAY + h:SM_DECAY + h + 1, :]
        dm = jnp.exp(jnp.minimum(gcol - grow, 0.0))
        q16 = qh.astype(BF16)
        k16 = kh.astype(BF16)
        a_ref[h] = jnp.where(strict, beta * _dot_nt(k16, k16) * dm, 0.0)
        qk = jnp.where(causal, _dot_nt(q16, k16) * dm, 0.0).astype(BF16)
        heads.append((qh, kh, vh, beta, gcol, qk))
        _run_one(fill)
        _run_one(fill)

    for lvl in range(N_LEVELS):
        m = lv == lvl
        for h in range(HEADS):
            l_lvl = jnp.where(m, a_ref[h], 0.0)
            if lvl == 0:
                x_ref[h] = eye - l_lvl
            else:
                xcur = x_ref[h].astype(BF16)
                x_ref[h] = x_ref[h] - _dot(_dot(xcur, l_lvl.astype(BF16)).astype(BF16), xcur)

    for h, (qh, kh, vh, beta, gcol, qk) in enumerate(heads):
        sl = slice(h * HEAD_W, (h + 1) * HEAD_W)
        eg = jnp.exp(gcol)
        rhs = jnp.concatenate([vh * beta, kh * (beta * eg)], axis=-1).astype(BF16)
        sol = _dot(x_ref[h].astype(BF16), rhs)
        u_h = sol[:, :HEAD_W]
        w_h = sol[:, HEAD_W:]
        st = st_ref[h]
        st16 = st.astype(BF16)
        v_new = u_h - _dot_nt(w_h.astype(BF16), st16)
        vn16 = v_new.astype(BF16)
        o = _dot_nt((qh * eg).astype(BF16), st16) + _dot(qk, vn16)
        g_last = gcol[R - 1:R, :]
        ke = (kh * jnp.exp(g_last - gcol)).astype(BF16)
        st_ref[h] = st * jnp.exp(g_last) + _dot_tn(vn16, ke)
        y_ref[:, sl] = _head_out(o, b_g[:, sl], nw_ref[...])
    _run_all(fill)


def _gdn(h, norm_mix, w, conv_w, head_params, norm_w, wl, lv, nb):
    return _mixer_call(
        functools.partial(_gdn_kernel, nb), "gdn", h,
        [norm_mix, w, conv_w, head_params, norm_w, wl, lv],
        [_const_spec((1, D_MODEL)), _const_spec(w.shape), _const_spec((CONV_W, 3 * BRANCH_W)),
         _const_spec((8, LANES)), _const_spec((1, HEAD_W)), _const_spec(wl.shape), _const_spec((R, R))],
        w.shape[1], _state_scratch() + [pltpu.VMEM((R + TAIL, 3 * BRANCH_W), F32),
                            pltpu.VMEM((TAIL, 3 * BRANCH_W), F32),
                            pltpu.VMEM((HEADS, R, R), F32),
                            pltpu.VMEM((HEADS, R, R), F32)])


def _split_hi_lo(x):
    hi = x.astype(BF16)
    return hi, (x - hi.astype(F32)).astype(BF16)


def _merge_kernel(ya_ref, yb_ref, yc_ref, h_ref, nm_ref, wgate_ref, wb_ref, wo_ref, nf_ref,
                  wr_hi_ref, wr_lo_ref, br_ref, hn_ref, u2_ref, cmb_ref):
    i = pl.program_id(0)
    h = h_ref[...]
    u = (_rmsnorm(h, nm_ref[...]) * _valid_rows(i == pl.num_programs(0) - 1)).astype(BF16)
    merged = jnp.zeros((R, D_MODEL), F32)
    for n, y_ref in enumerate((ya_ref, yb_ref, yc_ref)):
        gate = _sigmoid(_dot(u, wgate_ref[:, n * D_MODEL:(n + 1) * D_MODEL]))
        merged = merged + gate * _dot(y_ref[...], wb_ref[n])
    hn = h + _dot(merged.astype(BF16), wo_ref[...])
    hn_ref[...] = hn
    u2 = _rmsnorm(hn, nf_ref[...])
    u2_ref[...] = u2.astype(BF16)

    u_hi, u_lo = _split_hi_lo(u2)
    logits = (_dot(u_hi, wr_hi_ref[...]) + _dot(u_hi, wr_lo_ref[...]) + _dot(u_lo, wr_hi_ref[...])
              + br_ref[...])
    lane_i = lax.broadcasted_iota(jnp.int32, (R, LANES), 1)
    lane = lane_i.astype(F32)
    lane_grp = (lane_i // EXP_PER_GROUP).astype(F32)
    neg = jnp.float32(-jnp.inf)
    big = jnp.float32(1e9)
    is_g = (lane_i >= N_EXPERTS) & (lane_i < N_EXPERTS + N_GROUPS)
    lg = jnp.where(is_g, logits, neg)
    mg = jnp.max(lg, axis=-1, keepdims=True)
    zg = jnp.sum(jnp.exp(lg - mg), axis=-1, keepdims=True)
    g_val = 1.0 / zg
    g_idx = jnp.min(jnp.where(lg == mg, lane, big), axis=-1, keepdims=True) - N_EXPERTS
    in_grp = (lane_i < N_EXPERTS) & (lane_grp == g_idx)
    le = jnp.where(in_grp, logits, neg)
    m1 = jnp.max(le, axis=-1, keepdims=True)
    ze = jnp.sum(jnp.exp(le - m1), axis=-1, keepdims=True)
    i1 = jnp.min(jnp.where(le == m1, lane, big), axis=-1, keepdims=True)
    le2 = jnp.where(lane == i1, neg, le)
    m2 = jnp.max(le2, axis=-1, keepdims=True)
    i2 = jnp.min(jnp.where(le2 == m2, lane, big), axis=-1, keepdims=True)
    p1 = 1.0 / ze
    p2 = jnp.exp(m2 - m1) / ze
    den = p1 + p2
    cmb = g_val * jnp.where(lane == i1, p1 / den, jnp.where(lane == i2, p2 / den, 0.0))
    cmb_ref[...] = jnp.where(lane_i == CMB_GROUP_LANE, g_idx, cmb)


def _merge(ya, yb, yc, h, norm_mix, w_gate, wb, wo, nf, wr_hi, wr_lo, br):
    t = h.shape[0]
    row = lambda w: pl.BlockSpec((R, w), lambda i: (i, 0))
    return pl.pallas_call(
        _merge_kernel,
        grid=(t // R,),
        in_specs=[row(BRANCH_W), row(BRANCH_W), row(BRANCH_W), row(D_MODEL),
                  _const_spec((1, D_MODEL)), _const_spec((D_MODEL, N_BRANCH * D_MODEL)),
                  _const_spec((N_BRANCH, BRANCH_W, D_MODEL)), _const_spec((D_MODEL, D_MODEL)),
                  _const_spec((1, D_MODEL)), _const_spec((D_MODEL, LANES)), _const_spec((D_MODEL, LANES)),
                  _const_spec((1, LANES))],
        out_specs=[row(D_MODEL), row(D_MODEL), row(LANES)],
        out_shape=[jax.ShapeDtypeStruct((t, D_MODEL), F32),
                   jax.ShapeDtypeStruct((t, D_MODEL), BF16),
                   jax.ShapeDtypeStruct((t, LANES), F32)],
        compiler_params=_cparams(("arbitrary",)),
        name="merge_router",
    )(ya, yb, yc, h, norm_mix, w_gate, wb, wo, nf, wr_hi, wr_lo, br)


def _moe_kernel(final, cnt_ref, u_ref, c_ref, h_ref, wg_ref, wu_ref, wd_ref, tri_ref, nfin_ref, o_ref):
    w = pl.program_id(0)
    g = pl.program_id(1)
    u = u_ref[...]
    cmb = c_ref[...]
    gid_row = cmb.T[CMB_GROUP_LANE:CMB_GROUP_LANE + 1, :]
    in_row = jnp.where(gid_row == g.astype(F32), 1.0, 0.0)
    ranks = []
    before = jnp.zeros((1, 1), F32)
    for j in range(MOE_W // R):
        seg = in_row[:, j * R:(j + 1) * R]
        ranks.append(_dot(jnp.broadcast_to(seg, (16, R)).astype(BF16), tri_ref[...])[0:1, :] + before)
        before = before + jnp.sum(seg, axis=-1, keepdims=True)
    rank_row = jnp.concatenate(ranks, axis=-1)
    cmb_hi, cmb_lo = _split_hi_lo(cmb)
    slot = lax.broadcasted_iota(jnp.int32, (MOE_CAP, 1), 0).astype(F32)
    lane = lax.broadcasted_iota(jnp.int32, (MOE_CAP, LANES), 1)

    @pl.when(g == 0)
    def _():
        o_ref[...] = h_ref[...]

    n_chunks = (cnt_ref[w * N_GROUPS + g] + MOE_CAP - 1) // MOE_CAP

    def chunk(c, carry):
        base = (c * MOE_CAP).astype(F32)
        sel = jnp.where((in_row > 0.0) & (rank_row - base == slot), 1.0, 0.0).astype(BF16)
        x = _dot(sel, u).astype(BF16)
        cw = _dot(sel, cmb_hi) + _dot(sel, cmb_lo)
        y = jnp.zeros((MOE_CAP, D_MODEL), F32)
        for e in range(EXP_PER_GROUP):
            ce = jnp.sum(jnp.where(lane == g * EXP_PER_GROUP + e, cw, 0.0), axis=-1, keepdims=True)
            hid = _silu(_dot(x, wg_ref[0, e])) * _dot(x, wu_ref[0, e]) * ce
            y = y + _dot(hid.astype(BF16), wd_ref[0, e])
        o_ref[...] += _dot_tn(sel, y.astype(BF16))
        return carry

    lax.fori_loop(0, n_chunks, chunk, 0)

    if final:
        @pl.when(g == N_GROUPS - 1)
        def _():
            o_ref[...] = _rmsnorm(o_ref[...], nfin_ref[...])


def _moe(u2, cmb, h, wg, wu, wd, norm_final, final):
    t = h.shape[0]
    n_win = t // MOE_W
    assert t % MOE_W == 0
    gid = cmb[:, CMB_GROUP_LANE].astype(jnp.int32).reshape(n_win, MOE_W)
    counts = jnp.sum(gid[:, :, None] == jnp.arange(N_GROUPS, dtype=jnp.int32), axis=1, dtype=jnp.int32)
    tri = jnp.asarray(np.triu(np.ones((R, R), np.float32), 1), dtype=BF16)
    row = lambda width: pl.BlockSpec((MOE_W, width), lambda w, g, cnt: (w, 0))
    wspec = lambda a, b: pl.BlockSpec((1, EXP_PER_GROUP, a, b), lambda w, g, cnt: (g, 0, 0, 0))
    return pl.pallas_call(
        functools.partial(_moe_kernel, final),
        grid_spec=pltpu.PrefetchScalarGridSpec(
            num_scalar_prefetch=1,
            grid=(n_win, N_GROUPS),
            in_specs=[row(D_MODEL), row(LANES),
                      pl.BlockSpec((MOE_W, D_MODEL), lambda w, g, cnt: (w, 0), pipeline_mode=pl.Buffered(1)),
                      wspec(D_MODEL, EXPERT_HIDDEN), wspec(D_MODEL, EXPERT_HIDDEN),
                      wspec(EXPERT_HIDDEN, D_MODEL),
                      pl.BlockSpec((R, R), lambda w, g, cnt: (0, 0)),
                      pl.BlockSpec((1, D_MODEL), lambda w, g, cnt: (0, 0))],
            out_specs=row(D_MODEL)),
        out_shape=jax.ShapeDtypeStruct((t - R if final else t, D_MODEL), F32),
        compiler_params=_cparams(("arbitrary", "arbitrary")),
        name="moe",
    )(counts.reshape(-1), u2, cmb, h, wg, wu, wd, tri, norm_final)


IN_SPLITS = (512, 512, 512, 512, 1536, 4, 4, 512, 256, 256, 512, 16, 512, 3072)


def _pad_heads(w):
    lead = w.shape[:-1]
    w = w.reshape(lead + (HEADS, C_DK))
    w = jnp.concatenate([w, jnp.zeros_like(w)], axis=-1)
    return w.reshape(lead + (BRANCH_W,))


def _inproj_weights(w_in):
    offs = np.cumsum((0,) + IN_SPLITS)
    (a_q, a_f, a_i, a_g, b_qkv, b_beta, b_decay, b_g, c_q, c_k, c_v, c_gk, c_g, gates) = [
        w_in[:, offs[j]:offs[j + 1]] for j in range(len(IN_SPLITS))]
    zeros = lambda n: jnp.zeros((D_MODEL, n), w_in.dtype)
    w_a = w_in[:, offs[0]:offs[4]]
    w_b = jnp.concatenate([b_qkv, b_g, b_beta, b_decay, zeros(LANES - SM_GK)], axis=-1)
    w_c = jnp.concatenate([_pad_heads(c_q), _pad_heads(c_k), c_v, c_g,
                           zeros(SM_GK), c_gk, zeros(LANES - SM_GK - C_GK_RANK)], axis=-1)
    return [w.astype(BF16) for w in (w_a, w_b, w_c, gates)]


def kernel(x, meta_tokens, norm_mix, w_in, hgrn_lb_logits, hgrn_norm, gdn_conv, gdn_a_log,
           gdn_dt_bias, gdn_norm, gla_gk_w2, gla_gk_b, gla_norm, w_branch, w_out, norm_ffn,
           router_group_w, router_group_b, router_expert_w, router_expert_b,
           expert_w_gate, expert_w_up, expert_w_down, norm_final):
    batch, seq, d = x.shape
    assert d == D_MODEL and seq % R == 0
    depth = w_in.shape[0]
    nb = seq // R
    lv_np, wl_np = _level_constants()
    lv = jnp.asarray(lv_np)
    wl = jnp.asarray(wl_np, dtype=BF16)

    h = jnp.concatenate([x.reshape(batch * seq, d), jnp.zeros((R - N_META, d), F32),
                         meta_tokens.astype(F32)], axis=0)

    lb_p = jax.nn.softmax(hgrn_lb_logits.astype(F32), axis=0)
    lb_all = jnp.maximum(jnp.cumsum(lb_p, axis=0) - lb_p[0:1], 0.0)

    for layer in range(depth):
        w_a, w_b, w_c, w_gate = _inproj_weights(w_in[layer])
        nm = norm_mix[layer][None, :]

        lb = lb_all[layer]
        lb_rows = jnp.zeros((8, BRANCH_W), F32)
        lb_rows = lb_rows.at[0].set(jnp.maximum(lb, LB_FLOOR)).at[1].set(1.0 - lb)
        y_a = _hgrn(h, nm, w_a, lb_rows, hgrn_norm[layer][None, :], wl, lv, nb)

        head_params = jnp.zeros((8, LANES), F32)
        head_params = head_params.at[0, SM_DECAY:SM_DECAY + HEADS].set(-jnp.exp(gdn_a_log[layer]))
        head_params = head_params.at[1, SM_DECAY:SM_DECAY + HEADS].set(gdn_dt_bias[layer])
        y_b = _gdn(h, nm, w_b, gdn_conv[layer], head_params, gdn_norm[layer][None, :], wl, lv, nb)

        w2 = jnp.zeros((LANES, BRANCH_W), F32).at[SM_GK:SM_GK + C_GK_RANK].set(_pad_heads(gla_gk_w2[layer]))
        y_c = _gla(h, nm, w_c, w2.astype(BF16), _pad_heads(gla_gk_b[layer])[None, :],
                   gla_norm[layer][None, :], wl, lv, nb)

        wr = jnp.zeros((D_MODEL, LANES), F32)
        wr = wr.at[:, :N_EXPERTS].set(router_expert_w[layer])
        wr = wr.at[:, N_EXPERTS:N_EXPERTS + N_GROUPS].set(router_group_w[layer])
        br = jnp.zeros((1, LANES), F32)
        br = br.at[0, :N_EXPERTS].set(router_expert_b[layer])
        br = br.at[0, N_EXPERTS:N_EXPERTS + N_GROUPS].set(router_group_b[layer])
        wr_hi = wr.astype(BF16)
        wr_lo = (wr - wr_hi.astype(F32)).astype(BF16)
        hn, u2, cmb = _merge(y_a, y_b, y_c, h, nm, w_gate, w_branch[layer].astype(BF16),
                             w_out[layer].astype(BF16), norm_ffn[layer][None, :], wr_hi, wr_lo, br)

        h = _moe(u2, cmb, hn, expert_w_gate[layer].astype(BF16), expert_w_up[layer].astype(BF16),
                 expert_w_down[layer].astype(BF16), norm_final[None, :], final=(layer == depth - 1))

    return h.reshape(batch, seq, d)
```

```python
import functools

import numpy as np
import jax
import jax.numpy as jnp
from jax import lax
from jax.experimental import pallas as pl
from jax.experimental.pallas import tpu as pltpu

F32 = jnp.float32
BF16 = jnp.bfloat16

D_MODEL = 1024
N_META = 16
CONV_W = 4
RMS_EPS = 1e-6
L2_EPS = 1e-6
LB_FLOOR = 1e-30
HEADS = 4
HEAD_W = 128
BRANCH_W = HEADS * HEAD_W
C_DK = 64
C_GK_RANK = 16
C_GK_NORM = 16.0
N_BRANCH = 3
N_GROUPS = 4
EXP_PER_GROUP = 8
N_EXPERTS = N_GROUPS * EXP_PER_GROUP
EXPERT_HIDDEN = 256

R = 256
N_LEVELS = 8
LV_DIAG = N_LEVELS
W_CUM = N_LEVELS
W_SFX = N_LEVELS + 1
LANES = 128

SM_BETA, SM_DECAY, SM_GK = 0, 4, 8

VMEM_LIMIT = 56 * 1024 * 1024
MOE_W = 1280
MOE_CAPS = (320, 384, 448)
CMB_GROUP_LANE = N_EXPERTS


def _cparams(sem):
    return pltpu.CompilerParams(dimension_semantics=sem, vmem_limit_bytes=VMEM_LIMIT)


def _const_spec(shape):
    nd = len(shape)
    return pl.BlockSpec(shape, lambda *_: (0,) * nd)


@functools.lru_cache(maxsize=None)
def _level_constants():
    t = np.arange(R)[:, None]
    s = np.arange(R)[None, :]
    x = np.maximum(t ^ s, 1)
    lv = np.where(s < t, np.floor(np.log2(x)).astype(np.int32), np.where(s == t, LV_DIAG, -1)).astype(np.int32)
    w = np.zeros((N_LEVELS + 2, R, R), np.float32)
    for l in range(N_LEVELS):
        hsz = 1 << l
        for r in range(R):
            hb = (r // hsz) * hsz
            if (r >> l) & 1:
                w[l, r, hb:r + 1] = 1.0
            else:
                w[l, r, r + 1:hb + hsz] = 1.0
    w[W_CUM] = np.tril(np.ones((R, R), np.float32))
    w[W_SFX] = np.triu(np.ones((R, R), np.float32), 1)
    return lv, w


def _dot(a, b):
    return jnp.dot(a, b, preferred_element_type=F32)


def _dot_nt(a, b):
    return lax.dot_general(a, b, (((1,), (1,)), ((), ())), preferred_element_type=F32)


def _dot_tn(a, b):
    return lax.dot_general(a, b, (((0,), (0,)), ((), ())), preferred_element_type=F32)


def _sigmoid(x):
    return 1.0 / (1.0 + jnp.exp(-x))


def _silu(x):
    return x * _sigmoid(x)


def _softplus(x):
    return jnp.maximum(x, 0.0) + jnp.log1p(jnp.exp(-jnp.abs(x)))


def _log_sigmoid(x):
    return -_softplus(-x)


def _rmsnorm(x, w):
    return x * lax.rsqrt(jnp.mean(x * x, axis=-1, keepdims=True) + RMS_EPS) * w


def _valid_rows(is_meta):
    row = lax.broadcasted_iota(jnp.int32, (R, 1), 0)
    first_valid = jnp.where(is_meta, R - N_META, 0)
    return jnp.where(row >= first_valid, 1.0, 0.0).astype(F32)


def _residual_block(hx_ref, hm_ref, is_meta):
    return jnp.where(is_meta, hm_ref[...], hx_ref[...])


def _mixer_input(h, nw_ref, valid):
    return (_rmsnorm(h, nw_ref[...]) * valid).astype(BF16)


PROJ_CHUNK = 256


def _projection_steps(hx_ref, hm_ref, nm_ref, w_ref, out_ref, is_meta):
    u = _mixer_input(_residual_block(hx_ref, hm_ref, is_meta), nm_ref, _valid_rows(is_meta))
    n_cols = w_ref.shape[1]

    def step(c0):
        sl = slice(c0, min(c0 + PROJ_CHUNK, n_cols))
        out_ref[:, sl] = _dot(u, w_ref[:, sl])

    return [functools.partial(step, c0) for c0 in range(0, n_cols, PROJ_CHUNK)]


def _run_one(steps):
    if steps:
        steps.pop(0)()


def _run_all(steps):
    while steps:
        steps.pop(0)()


def _seq_block(n_blocks, lag):
    return lambda i: ((jnp.clip(i - lag, 0, n_blocks - 1) + n_blocks - 1) % n_blocks, 0)


def _state_in(j, nb, st_ref, stm_ref):
    @pl.when(j <= 0)
    def _():
        st_ref[...] = jnp.zeros(st_ref.shape, st_ref.dtype)

    @pl.when((j >= 1) & ((j - 1) % nb == 0))
    def _():
        st_ref[...] = stm_ref[...]


def _state_out(j, st_ref, stm_ref):
    @pl.when(j == 0)
    def _():
        stm_ref[...] = st_ref[...]


def _skewed(i, pa_ref, pb_ref, body):
    @pl.when(i == 0)
    def _():
        pb_ref[...] = jnp.zeros(pb_ref.shape, pb_ref.dtype)

    @pl.when(i % 2 == 0)
    def _():
        body(pb_ref, pa_ref)

    @pl.when(i % 2 == 1)
    def _():
        body(pa_ref, pb_ref)


def _head_out(o, gate, nw):
    return (_rmsnorm(o, nw) * _silu(gate)).astype(BF16)


def _gla_block(q, k, v, g, gate, nw, wl_ref, lv, st_ref, p_ref, y_ref, fill):
    g16 = g.astype(BF16)
    rowi = lax.broadcasted_iota(jnp.int32, (R, 1), 0)
    for lvl in range(N_LEVELS):
        f = jnp.exp(_dot(wl_ref[lvl], g16))
        lower = ((rowi >> lvl) & 1) == 1
        z = (jnp.where(lower, q, k) * f).astype(BF16)
        hsz = 1 << lvl
        if 4 * hsz >= R:
            for h in range(HEADS):
                zh = z[:, h * HEAD_W:(h + 1) * HEAD_W]
                for r0 in range(0, R, 2 * hsz):
                    p_ref[h, r0 + hsz:r0 + 2 * hsz, r0:r0 + hsz] = _dot_nt(
                        zh[r0 + hsz:r0 + 2 * hsz], zh[r0:r0 + hsz])
        else:
            m = lv == lvl
            for h in range(HEADS):
                zh = z[:, h * HEAD_W:(h + 1) * HEAD_W]
                full = _dot_nt(zh, zh)
                if lvl == 0:
                    p_ref[h] = jnp.where(m, full, 0.0)
                else:
                    p_ref[h] = jnp.where(m, full, p_ref[h])
        _run_one(fill)
    q16 = q.astype(BF16)
    k16 = k.astype(BF16)
    m = lv == LV_DIAG
    for h in range(HEADS):
        sl = slice(h * HEAD_W, (h + 1) * HEAD_W)
        p_ref[h] = jnp.where(m, _dot_nt(q16[:, sl], k16[:, sl]), p_ref[h])
    b = _dot(wl_ref[W_CUM], g16)
    sfx = _dot(wl_ref[W_SFX], g16)
    qe = (q * jnp.exp(b)).astype(BF16)
    ke = (k * jnp.exp(sfx)).astype(BF16)
    dec = jnp.exp(b[R - 1:R, :])
    v16 = v.astype(BF16)
    for h in range(HEADS):
        sl = slice(h * HEAD_W, (h + 1) * HEAD_W)
        st = st_ref[h]
        o = _dot(p_ref[h].astype(BF16), v16[:, sl]) + _dot_nt(qe[:, sl], st.astype(BF16))
        st_ref[h] = st * dec[:, sl] + _dot_tn(v16[:, sl], ke[:, sl])
        y_ref[:, sl] = _head_out(o, gate[:, sl], nw)


def _residual_specs(hx, hm, block_of_step):
    hx_last = hx.shape[0] // R - 1
    hm_last = hm.shape[0] // R - 1
    return [pl.BlockSpec((R, D_MODEL), lambda i: (jnp.minimum(block_of_step(i)[0], hx_last), 0)),
            pl.BlockSpec((R, D_MODEL), lambda i: (hm_last, 0))]


def _mixer_call(kernel_fn, name, hx, hm, n_blocks, operands, operand_specs, proj_cols, scratch):
    return pl.pallas_call(
        kernel_fn,
        grid=(n_blocks + 1,),
        in_specs=_residual_specs(hx, hm, _seq_block(n_blocks, 0)) + operand_specs,
        out_specs=pl.BlockSpec((R, BRANCH_W), _seq_block(n_blocks, 1)),
        out_shape=jax.ShapeDtypeStruct((n_blocks * R, BRANCH_W), BF16),
        scratch_shapes=scratch + [pltpu.VMEM((R, proj_cols), F32), pltpu.VMEM((R, proj_cols), F32)],
        compiler_params=_cparams(("arbitrary",)),
        name=name,
    )(hx, hm, *operands)


def _state_scratch():
    return [pltpu.VMEM((HEADS, HEAD_W, HEAD_W), F32), pltpu.VMEM((HEADS, HEAD_W, HEAD_W), F32)]


def _hgrn_kernel(nb, hx_ref, hm_ref, nm_ref, w_ref, lb_ref, nw_ref, wl_ref, lv_ref,
                 y_ref, st_ref, stm_ref, p_ref, pa_ref, pb_ref):
    i = pl.program_id(0)
    j = i - 1
    _state_in(j, nb, st_ref, stm_ref)

    def body(rd, wr):
        a_q, f_in, a_i, a_g = [rd[:, c * BRANCH_W:(c + 1) * BRANCH_W] for c in range(4)]
        valid = _valid_rows(j == 0)
        lb_floor = lb_ref[0:1, :]
        one_m_lb = lb_ref[1:2, :]
        q = _silu(a_q) * (HEAD_W ** -0.5)
        e = jnp.exp(-jnp.abs(f_in))
        r = 1.0 / (1.0 + e)
        pos = f_in >= 0.0
        g = jnp.log(lb_floor + one_m_lb * jnp.where(pos, r, e * r))
        k = one_m_lb * jnp.where(pos, e * r, r) * valid
        fill = _projection_steps(hx_ref, hm_ref, nm_ref, w_ref, wr, i == 0)
        _gla_block(q, k, a_i, g, a_g, nw_ref[...], wl_ref, lv_ref[...], st_ref, p_ref, y_ref, fill)
        _run_all(fill)

    _skewed(i, pa_ref, pb_ref, body)
    _state_out(j, st_ref, stm_ref)


def _hgrn(hx, hm, n_blocks, norm_mix, w, lb_rows, norm_w, wl, lv, nb):
    return _mixer_call(
        functools.partial(_hgrn_kernel, nb), "hgrn2", hx, hm, n_blocks,
        [norm_mix, w, lb_rows, norm_w, wl, lv],
        [_const_spec((1, D_MODEL)), _const_spec(w.shape), _const_spec((8, BRANCH_W)),
         _const_spec((1, HEAD_W)), _const_spec(wl.shape), _const_spec((R, R))],
        w.shape[1], _state_scratch() + [pltpu.VMEM((HEADS, R, R), F32)])


def _gla_kernel(nb, hx_ref, hm_ref, nm_ref, w_ref, w2_ref, b2_ref, nw_ref, wl_ref, lv_ref,
                y_ref, st_ref, stm_ref, p_ref, pa_ref, pb_ref):
    i = pl.program_id(0)
    j = i - 1
    _state_in(j, nb, st_ref, stm_ref)

    def body(rd, wr):
        c_q, c_k, c_v, c_g = [rd[:, c * BRANCH_W:(c + 1) * BRANCH_W] for c in range(4)]
        gk_low = rd[:, 4 * BRANCH_W:]
        z = _dot(gk_low.astype(BF16), w2_ref[...]) + b2_ref[...]
        g = _log_sigmoid(z) * (1.0 / C_GK_NORM)
        fill = _projection_steps(hx_ref, hm_ref, nm_ref, w_ref, wr, i == 0)
        _gla_block(c_q * (C_DK ** -0.5), c_k * _valid_rows(j == 0), c_v, g, c_g, nw_ref[...],
                   wl_ref, lv_ref[...], st_ref, p_ref, y_ref, fill)
        _run_all(fill)

    _skewed(i, pa_ref, pb_ref, body)
    _state_out(j, st_ref, stm_ref)


def _gla(hx, hm, n_blocks, norm_mix, w, w2, b2, norm_w, wl, lv, nb):
    return _mixer_call(
        functools.partial(_gla_kernel, nb), "gla", hx, hm, n_blocks,
        [norm_mix, w, w2, b2, norm_w, wl, lv],
        [_const_spec((1, D_MODEL)), _const_spec(w.shape), _const_spec((LANES, BRANCH_W)),
         _const_spec((1, BRANCH_W)), _const_spec((1, HEAD_W)), _const_spec(wl.shape), _const_spec((R, R))],
        w.shape[1], _state_scratch() + [pltpu.VMEM((HEADS, R, R), F32)])


TAIL = 8


def _gdn_kernel(nb, hx_ref, hm_ref, nm_ref, w_ref, cw_ref, hp_ref, nw_ref, wl_ref, lv_ref,
                y_ref, st_ref, stm_ref, xx_ref, tailm_ref, x_ref, a_ref, pa_ref, pb_ref):
    i = pl.program_id(0)
    j = i - 1
    _state_in(j, nb, st_ref, stm_ref)

    @pl.when(j <= 0)
    def _():
        xx_ref[0:TAIL, :] = jnp.zeros((TAIL, 3 * BRANCH_W), F32)

    @pl.when((j >= 1) & ((j - 1) % nb == 0))
    def _():
        xx_ref[0:TAIL, :] = tailm_ref[...]

    def body(rd, wr):
        _gdn_body(i, j, rd, wr, hx_ref, hm_ref, nm_ref, w_ref, cw_ref, hp_ref, nw_ref, wl_ref, lv_ref,
                  y_ref, st_ref, xx_ref, x_ref, a_ref)

    _skewed(i, pa_ref, pb_ref, body)

    @pl.when(j == 0)
    def _():
        tailm_ref[...] = xx_ref[0:TAIL, :]

    _state_out(j, st_ref, stm_ref)


def _gdn_body(i, j, rd, wr, hx_ref, hm_ref, nm_ref, w_ref, cw_ref, hp_ref, nw_ref, wl_ref, lv_ref,
              y_ref, st_ref, xx_ref, x_ref, a_ref):
    fill = _projection_steps(hx_ref, hm_ref, nm_ref, w_ref, wr, i == 0)
    valid = _valid_rows(j == 0)
    lv = lv_ref[...]
    xx_ref[TAIL:, :] = rd[:, :3 * BRANCH_W]
    b_g = rd[:, 3 * BRANCH_W:4 * BRANCH_W]
    sm = rd[:, 4 * BRANCH_W:]
    conv = jnp.zeros((R, 3 * BRANCH_W), F32)
    for tap in range(CONV_W):
        off = TAIL - (CONV_W - 1) + tap
        conv = conv + xx_ref[off:off + R, :] * cw_ref[tap:tap + 1, :]
    qkv = _silu(conv)
    xx_ref[0:TAIL, :] = xx_ref[R:R + TAIL, :]

    a_neg = hp_ref[0:1, :]
    dt_b = hp_ref[1:2, :]
    log_a = a_neg * _softplus(sm + dt_b)
    la_hi = log_a.astype(BF16)
    la_lo = (log_a - la_hi.astype(F32)).astype(BF16)
    gcum = _dot(wl_ref[W_CUM], la_hi) + _dot(wl_ref[W_CUM], la_lo)
    gcum_t = gcum.T
    beta_all = _sigmoid(sm) * valid

    strict = (lv >= 0) & (lv < LV_DIAG)
    causal = lv >= 0
    eye = (lv == LV_DIAG).astype(F32)

    heads = []
    for h in range(HEADS):
        qh = qkv[:, h * HEAD_W:(h + 1) * HEAD_W]
        kh = qkv[:, BRANCH_W + h * HEAD_W:BRANCH_W + (h + 1) * HEAD_W] * valid
        vh = qkv[:, 2 * BRANCH_W + h * HEAD_W:2 * BRANCH_W + (h + 1) * HEAD_W]
        qh = qh * lax.rsqrt(jnp.sum(qh * qh, axis=-1, keepdims=True) + L2_EPS) * (HEAD_W ** -0.5)
        kh = kh * lax.rsqrt(jnp.sum(kh * kh, axis=-1, keepdims=True) + L2_EPS)
        beta = beta_all[:, SM_BETA + h:SM_BETA + h + 1]
        gcol = gcum[:, SM_DECAY + h:SM_DECAY + h + 1]
        grow = gcum_t[SM_DECAY + h:SM_DECAY + h + 1, :]
        dm = jnp.exp(jnp.minimum(gcol - grow, 0.0))
        q16 = qh.astype(BF16)
        k16 = kh.astype(BF16)
        a_ref[h] = jnp.where(strict, beta * _dot_nt(k16, k16) * dm, 0.0)
        qk = jnp.where(causal, _dot_nt(q16, k16) * dm, 0.0).astype(BF16)
        heads.append((qh, kh, vh, beta, gcol, qk))
        _run_one(fill)
        _run_one(fill)

    for lvl in range(N_LEVELS):
        m = lv == lvl
        for h in range(HEADS):
            l_lvl = jnp.where(m, a_ref[h], 0.0)
            if lvl == 0:
                x_ref[h] = eye - l_lvl
            else:
                xcur = x_ref[h].astype(BF16)
                x_ref[h] = x_ref[h] - _dot(_dot(xcur, l_lvl.astype(BF16)).astype(BF16), xcur)

    for h, (qh, kh, vh, beta, gcol, qk) in enumerate(heads):
        sl = slice(h * HEAD_W, (h + 1) * HEAD_W)
        eg = jnp.exp(gcol)
        rhs = jnp.concatenate([vh * beta, kh * (beta * eg)], axis=-1).astype(BF16)
        sol = _dot(x_ref[h].astype(BF16), rhs)
        u_h = sol[:, :HEAD_W]
        w_h = sol[:, HEAD_W:]
        st = st_ref[h]
        st16 = st.astype(BF16)
        v_new = u_h - _dot_nt(w_h.astype(BF16), st16)
        vn16 = v_new.astype(BF16)
        o = _dot_nt((qh * eg).astype(BF16), st16) + _dot(qk, vn16)
        g_last = gcol[R - 1:R, :]
        ke = (kh * jnp.exp(g_last - gcol)).astype(BF16)
        st_ref[h] = st * jnp.exp(g_last) + _dot_tn(vn16, ke)
        y_ref[:, sl] = _head_out(o, b_g[:, sl], nw_ref[...])
    _run_all(fill)


def _gdn(hx, hm, n_blocks, norm_mix, w, conv_w, head_params, norm_w, wl, lv, nb):
    return _mixer_call(
        functools.partial(_gdn_kernel, nb), "gdn", hx, hm, n_blocks,
        [norm_mix, w, conv_w, head_params, norm_w, wl, lv],
        [_const_spec((1, D_MODEL)), _const_spec(w.shape), _const_spec((CONV_W, 3 * BRANCH_W)),
         _const_spec((8, LANES)), _const_spec((1, HEAD_W)), _const_spec(wl.shape), _const_spec((R, R))],
        w.shape[1], _state_scratch() + [pltpu.VMEM((R + TAIL, 3 * BRANCH_W), F32),
                            pltpu.VMEM((TAIL, 3 * BRANCH_W), F32),
                            pltpu.VMEM((HEADS, R, R), F32),
                            pltpu.VMEM((HEADS, R, R), F32)])


def _split_hi_lo(x):
    hi = x.astype(BF16)
    return hi, (x - hi.astype(F32)).astype(BF16)


def _merge_kernel(ya_ref, yb_ref, yc_ref, hx_ref, hm_ref, nm_ref, wgate_ref, wb_ref, wo_ref, nf_ref,
                  wr_hi_ref, wr_lo_ref, br_ref, hn_ref, u2_ref, cmb_ref):
    is_meta = pl.program_id(0) == pl.num_programs(0) - 1
    h = _residual_block(hx_ref, hm_ref, is_meta)
    u = _mixer_input(h, nm_ref, _valid_rows(is_meta))
    merged = jnp.zeros((R, D_MODEL), F32)
    for n, y_ref in enumerate((ya_ref, yb_ref, yc_ref)):
        gate = _sigmoid(_dot(u, wgate_ref[:, n * D_MODEL:(n + 1) * D_MODEL]))
        merged = merged + gate * _dot(y_ref[...], wb_ref[0, n])
    hn = h + _dot(merged.astype(BF16), wo_ref[0])
    hn_ref[...] = hn
    u2 = _rmsnorm(hn, nf_ref[...])
    u2_ref[...] = u2.astype(BF16)

    u_hi, u_lo = _split_hi_lo(u2)
    logits = (_dot(u_hi, wr_hi_ref[...]) + _dot(u_hi, wr_lo_ref[...]) + _dot(u_lo, wr_hi_ref[...])
              + br_ref[...])
    lane_i = lax.broadcasted_iota(jnp.int32, (R, LANES), 1)
    lane = lane_i.astype(F32)
    lane_grp = (lane_i // EXP_PER_GROUP).astype(F32)
    neg = jnp.float32(-jnp.inf)
    big = jnp.float32(1e9)
    is_g = (lane_i >= N_EXPERTS) & (lane_i < N_EXPERTS + N_GROUPS)
    lg = jnp.where(is_g, logits, neg)
    mg = jnp.max(lg, axis=-1, keepdims=True)
    zg = jnp.sum(jnp.exp(lg - mg), axis=-1, keepdims=True)
    g_val = 1.0 / zg
    g_idx = jnp.min(jnp.where(lg == mg, lane, big), axis=-1, keepdims=True) - N_EXPERTS
    in_grp = (lane_i < N_EXPERTS) & (lane_grp == g_idx)
    le = jnp.where(in_grp, logits, neg)
    m1 = jnp.max(le, axis=-1, keepdims=True)
    ze = jnp.sum(jnp.exp(le - m1), axis=-1, keepdims=True)
    i1 = jnp.min(jnp.where(le == m1, lane, big), axis=-1, keepdims=True)
    le2 = jnp.where(lane == i1, neg, le)
    m2 = jnp.max(le2, axis=-1, keepdims=True)
    i2 = jnp.min(jnp.where(le2 == m2, lane, big), axis=-1, keepdims=True)
    p1 = 1.0 / ze
    p2 = jnp.exp(m2 - m1) / ze
    den = p1 + p2
    cmb = g_val * jnp.where(lane == i1, p1 / den, jnp.where(lane == i2, p2 / den, 0.0))
    cmb_ref[...] = jnp.where(lane_i == CMB_GROUP_LANE, g_idx, cmb)


def _merge(ya, yb, yc, hx, hm, norm_mix, w_gate, wb, wo, layer, nf, wr_hi, wr_lo, br):
    t = ya.shape[0]
    row = lambda w: pl.BlockSpec((R, w), lambda i: (i, 0))
    return pl.pallas_call(
        _merge_kernel,
        grid=(t // R,),
        in_specs=[row(BRANCH_W), row(BRANCH_W), row(BRANCH_W)]
        + _residual_specs(hx, hm, lambda i: (i, 0))
        + [_const_spec((1, D_MODEL)), _const_spec((D_MODEL, N_BRANCH * D_MODEL)),
                  pl.BlockSpec((1, N_BRANCH, BRANCH_W, D_MODEL), lambda i: (layer, 0, 0, 0)),
                  pl.BlockSpec((1, D_MODEL, D_MODEL), lambda i: (layer, 0, 0)),
                  _const_spec((1, D_MODEL)), _const_spec((D_MODEL, LANES)), _const_spec((D_MODEL, LANES)),
                  _const_spec((1, LANES))],
        out_specs=[row(D_MODEL), row(D_MODEL), row(LANES)],
        out_shape=[jax.ShapeDtypeStruct((t, D_MODEL), F32),
                   jax.ShapeDtypeStruct((t, D_MODEL), BF16),
                   jax.ShapeDtypeStruct((t, LANES), F32)],
        compiler_params=_cparams(("arbitrary",)),
        name="merge_router",
    )(ya, yb, yc, hx, hm, norm_mix, w_gate, wb, wo, nf, wr_hi, wr_lo, br)


def _moe_kernel(final, cnt_ref, u_ref, c_ref, h_ref, wg_ref, wu_ref, wd_ref, tri_ref, nfin_ref, o_ref):
    w = pl.program_id(0)
    g = pl.program_id(1)
    u = u_ref[...]
    cmb = c_ref[...]
    gid_row = cmb.T[CMB_GROUP_LANE:CMB_GROUP_LANE + 1, :]
    in_row = jnp.where(gid_row == g.astype(F32), 1.0, 0.0)
    ranks = []
    before = jnp.zeros((1, 1), F32)
    for j in range(MOE_W // R):
        seg = in_row[:, j * R:(j + 1) * R]
        ranks.append(_dot(jnp.broadcast_to(seg, (16, R)).astype(BF16), tri_ref[...])[0:1, :] + before)
        before = before + jnp.sum(seg, axis=-1, keepdims=True)
    rank_row = jnp.concatenate(ranks, axis=-1)
    cmb_hi, cmb_lo = _split_hi_lo(cmb)

    @pl.when(g == 0)
    def _():
        o_ref[...] = h_ref[...]

    count = cnt_ref[w * N_GROUPS + g]
    n_pass = (count + MOE_CAPS[-1] - 1) // MOE_CAPS[-1]
    per_pass = (count + jnp.maximum(n_pass, 1) - 1) // jnp.maximum(n_pass, 1)

    def run(cap):
        slot = lax.broadcasted_iota(jnp.int32, (cap, 1), 0).astype(F32)
        lane = lax.broadcasted_iota(jnp.int32, (cap, LANES), 1)

        def one_pass(c, carry):
            base = (c * cap).astype(F32)
            sel = jnp.where((in_row > 0.0) & (rank_row - base == slot), 1.0, 0.0).astype(BF16)
            x = _dot(sel, u).astype(BF16)
            cw = _dot(sel, cmb_hi) + _dot(sel, cmb_lo)
            y = jnp.zeros((cap, D_MODEL), F32)
            for e in range(EXP_PER_GROUP):
                ce = jnp.sum(jnp.where(lane == g * EXP_PER_GROUP + e, cw, 0.0), axis=-1, keepdims=True)
                hid = _silu(_dot(x, wg_ref[0, 0, e])) * _dot(x, wu_ref[0, 0, e]) * ce
                y = y + _dot(hid.astype(BF16), wd_ref[0, 0, e])
            o_ref[...] += _dot_tn(sel, y.astype(BF16))
            return carry

        lax.fori_loop(0, n_pass, one_pass, 0)

    below = 0
    for cap in MOE_CAPS:
        pl.when((per_pass > below) & (per_pass <= cap))(functools.partial(run, cap))
        below = cap

    if final:
        @pl.when(g == N_GROUPS - 1)
        def _():
            o_ref[...] = _rmsnorm(o_ref[...], nfin_ref[...])


def _moe(u2, cmb, h, wg, wu, wd, layer, norm_final, final):
    t = h.shape[0]
    n_win = t // MOE_W
    assert t % MOE_W == 0
    gid = cmb[:, CMB_GROUP_LANE].astype(jnp.int32).reshape(n_win, MOE_W)
    counts = jnp.sum(gid[:, :, None] == jnp.arange(N_GROUPS, dtype=jnp.int32), axis=1, dtype=jnp.int32)
    tri = jnp.asarray(np.triu(np.ones((R, R), np.float32), 1), dtype=BF16)
    row = lambda width: pl.BlockSpec((MOE_W, width), lambda w, g, cnt: (w, 0))
    wspec = lambda a, b: pl.BlockSpec((1, 1, EXP_PER_GROUP, a, b), lambda w, g, cnt: (layer, g, 0, 0, 0))
    return pl.pallas_call(
        functools.partial(_moe_kernel, final),
        grid_spec=pltpu.PrefetchScalarGridSpec(
            num_scalar_prefetch=1,
            grid=(n_win, N_GROUPS),
            in_specs=[row(D_MODEL), row(LANES),
                      pl.BlockSpec((MOE_W, D_MODEL), lambda w, g, cnt: (w, 0), pipeline_mode=pl.Buffered(1)),
                      wspec(D_MODEL, EXPERT_HIDDEN), wspec(D_MODEL, EXPERT_HIDDEN),
                      wspec(EXPERT_HIDDEN, D_MODEL),
                      pl.BlockSpec((R, R), lambda w, g, cnt: (0, 0)),
                      pl.BlockSpec((1, D_MODEL), lambda w, g, cnt: (0, 0))],
            out_specs=row(D_MODEL)),
        out_shape=jax.ShapeDtypeStruct((t - R if final else t, D_MODEL), F32),
        compiler_params=_cparams(("arbitrary", "arbitrary")),
        name="moe",
    )(counts.reshape(-1), u2, cmb, h, wg, wu, wd, tri, norm_final)


IN_SPLITS = (512, 512, 512, 512, 1536, 4, 4, 512, 256, 256, 512, 16, 512, 3072)


def _pad_heads(w):
    lead = w.shape[:-1]
    w = w.reshape(lead + (HEADS, C_DK))
    w = jnp.concatenate([w, jnp.zeros_like(w)], axis=-1)
    return w.reshape(lead + (BRANCH_W,))


def _inproj_weights(w_in):
    offs = np.cumsum((0,) + IN_SPLITS)
    (a_q, a_f, a_i, a_g, b_qkv, b_beta, b_decay, b_g, c_q, c_k, c_v, c_gk, c_g, gates) = [
        w_in[:, offs[j]:offs[j + 1]] for j in range(len(IN_SPLITS))]
    zeros = lambda n: jnp.zeros((D_MODEL, n), w_in.dtype)
    w_a = w_in[:, offs[0]:offs[4]]
    w_b = jnp.concatenate([b_qkv, b_g, b_beta, b_decay, zeros(LANES - SM_GK)], axis=-1)
    w_c = jnp.concatenate([_pad_heads(c_q), _pad_heads(c_k), c_v, c_g,
                           zeros(SM_GK), c_gk, zeros(LANES - SM_GK - C_GK_RANK)], axis=-1)
    return [w.astype(BF16) for w in (w_a, w_b, w_c, gates)]


def kernel(x, meta_tokens, norm_mix, w_in, hgrn_lb_logits, hgrn_norm, gdn_conv, gdn_a_log,
           gdn_dt_bias, gdn_norm, gla_gk_w2, gla_gk_b, gla_norm, w_branch, w_out, norm_ffn,
           router_group_w, router_group_b, router_expert_w, router_expert_b,
           expert_w_gate, expert_w_up, expert_w_down, norm_final):
    batch, seq, d = x.shape
    assert d == D_MODEL and seq % R == 0
    depth = w_in.shape[0]
    nb = seq // R
    lv_np, wl_np = _level_constants()
    lv = jnp.asarray(lv_np)
    wl = jnp.asarray(wl_np, dtype=BF16)

    n_blocks = batch * nb + 1
    hx = x.reshape(batch * seq, d)
    hm = jnp.concatenate([jnp.zeros((R - N_META, d), F32), meta_tokens.astype(F32)], axis=0)

    wg16, wu16, wd16, wb16, wo16 = (w.astype(BF16) for w in (
        expert_w_gate, expert_w_up, expert_w_down, w_branch, w_out))

    lb_p = jax.nn.softmax(hgrn_lb_logits.astype(F32), axis=0)
    lb_all = jnp.maximum(jnp.cumsum(lb_p, axis=0) - lb_p[0:1], 0.0)

    for layer in range(depth):
        w_a, w_b, w_c, w_gate = _inproj_weights(w_in[layer])
        nm = norm_mix[layer][None, :]

        lb = lb_all[layer]
        lb_rows = jnp.zeros((8, BRANCH_W), F32)
        lb_rows = lb_rows.at[0].set(jnp.maximum(lb, LB_FLOOR)).at[1].set(1.0 - lb)
        y_a = _hgrn(hx, hm, n_blocks, nm, w_a, lb_rows, hgrn_norm[layer][None, :], wl, lv, nb)

        head_params = jnp.zeros((8, LANES), F32)
        head_params = head_params.at[0, SM_DECAY:SM_DECAY + HEADS].set(-jnp.exp(gdn_a_log[layer]))
        head_params = head_params.at[1, SM_DECAY:SM_DECAY + HEADS].set(gdn_dt_bias[layer])
        y_b = _gdn(hx, hm, n_blocks, nm, w_b, gdn_conv[layer], head_params, gdn_norm[layer][None, :], wl, lv, nb)

        w2 = jnp.zeros((LANES, BRANCH_W), F32).at[SM_GK:SM_GK + C_GK_RANK].set(_pad_heads(gla_gk_w2[layer]))
        y_c = _gla(hx, hm, n_blocks, nm, w_c, w2.astype(BF16), _pad_heads(gla_gk_b[layer])[None, :],
                   gla_norm[layer][None, :], wl, lv, nb)

        wr = jnp.zeros((D_MODEL, LANES), F32)
        wr = wr.at[:, :N_EXPERTS].set(router_expert_w[layer])
        wr = wr.at[:, N_EXPERTS:N_EXPERTS + N_GROUPS].set(router_group_w[layer])
        br = jnp.zeros((1, LANES), F32)
        br = br.at[0, :N_EXPERTS].set(router_expert_b[layer])
        br = br.at[0, N_EXPERTS:N_EXPERTS + N_GROUPS].set(router_group_b[layer])
        wr_hi = wr.astype(BF16)
        wr_lo = (wr - wr_hi.astype(F32)).astype(BF16)
        hn, u2, cmb = _merge(y_a, y_b, y_c, hx, hm, nm, w_gate, wb16, wo16, layer,
                             norm_ffn[layer][None, :], wr_hi, wr_lo, br)

        hx = hm = _moe(u2, cmb, hn, wg16, wu16, wd16, layer, norm_final[None, :], final=(layer == depth - 1))

    return hx.reshape(batch, seq, d)
```

```python
import functools

import numpy as np
import jax
import jax.numpy as jnp
from jax import lax
from jax.experimental import pallas as pl
from jax.experimental.pallas import tpu as pltpu

F32 = jnp.float32
BF16 = jnp.bfloat16

D_MODEL = 1024
N_META = 16
CONV_W = 4
RMS_EPS = 1e-6
L2_EPS = 1e-6
LB_FLOOR = 1e-30
HEADS = 4
HEAD_W = 128
BRANCH_W = HEADS * HEAD_W
C_DK = 64
C_GK_RANK = 16
C_GK_NORM = 16.0
N_BRANCH = 3
N_GROUPS = 4
EXP_PER_GROUP = 8
N_EXPERTS = N_GROUPS * EXP_PER_GROUP
EXPERT_HIDDEN = 256

R = 256
N_LEVELS = 8
LV_DIAG = N_LEVELS
W_CUM = N_LEVELS
W_SFX = N_LEVELS + 1
LANES = 128

SM_BETA, SM_DECAY, SM_GK = 0, 4, 8

VMEM_LIMIT = 56 * 1024 * 1024
MOE_W = 1280
MOE_CAPS = (320, 384, 448)
CMB_GROUP_LANE = N_EXPERTS


def _cparams(sem):
    return pltpu.CompilerParams(dimension_semantics=sem, vmem_limit_bytes=VMEM_LIMIT)


def _const_spec(shape):
    nd = len(shape)
    return pl.BlockSpec(shape, lambda *_: (0,) * nd)


@functools.lru_cache(maxsize=None)
def _level_constants():
    t = np.arange(R)[:, None]
    s = np.arange(R)[None, :]
    x = np.maximum(t ^ s, 1)
    lv = np.where(s < t, np.floor(np.log2(x)).astype(np.int32), np.where(s == t, LV_DIAG, -1)).astype(np.int32)
    w = np.zeros((N_LEVELS + 2, R, R), np.float32)
    for l in range(N_LEVELS):
        hsz = 1 << l
        for r in range(R):
            hb = (r // hsz) * hsz
            if (r >> l) & 1:
                w[l, r, hb:r + 1] = 1.0
            else:
                w[l, r, r + 1:hb + hsz] = 1.0
    w[W_CUM] = np.tril(np.ones((R, R), np.float32))
    w[W_SFX] = np.triu(np.ones((R, R), np.float32), 1)
    return lv, w


def _dot(a, b):
    return jnp.dot(a, b, preferred_element_type=F32)


def _dot_nt(a, b):
    return lax.dot_general(a, b, (((1,), (1,)), ((), ())), preferred_element_type=F32)


def _dot_tn(a, b):
    return lax.dot_general(a, b, (((0,), (0,)), ((), ())), preferred_element_type=F32)


def _sigmoid(x):
    return 1.0 / (1.0 + jnp.exp(-x))


def _silu(x):
    return x * _sigmoid(x)


def _softplus(x):
    return jnp.maximum(x, 0.0) + jnp.log1p(jnp.exp(-jnp.abs(x)))


def _log_sigmoid(x):
    return -_softplus(-x)


def _rmsnorm(x, w):
    return x * lax.rsqrt(jnp.mean(x * x, axis=-1, keepdims=True) + RMS_EPS) * w


def _valid_rows(is_meta):
    row = lax.broadcasted_iota(jnp.int32, (R, 1), 0)
    first_valid = jnp.where(is_meta, R - N_META, 0)
    return jnp.where(row >= first_valid, 1.0, 0.0).astype(F32)


def _residual_block(hx_ref, hm_ref, is_meta):
    return jnp.where(is_meta, hm_ref[...], hx_ref[...])


def _mixer_input(h, nw_ref, valid):
    return (_rmsnorm(h, nw_ref[...]) * valid).astype(BF16)


PROJ_CHUNK = 256


def _projection_steps(u, w_ref, out_ref):
    n_cols = w_ref.shape[1]

    def step(c0):
        sl = slice(c0, min(c0 + PROJ_CHUNK, n_cols))
        out_ref[:, sl] = _dot(u, w_ref[:, sl])

    return [functools.partial(step, c0) for c0 in range(0, n_cols, PROJ_CHUNK)]


def _run_one(steps):
    if steps:
        steps.pop(0)()


def _run_all(steps):
    while steps:
        steps.pop(0)()


def _seq_block(n_blocks, lag):
    return lambda i: ((jnp.clip(i - lag, 0, n_blocks - 1) + n_blocks - 1) % n_blocks, 0)


def _state_in(j, nb, st_ref, stm_ref):
    @pl.when(j <= 0)
    def _():
        st_ref[...] = jnp.zeros(st_ref.shape, st_ref.dtype)

    @pl.when((j >= 1) & ((j - 1) % nb == 0))
    def _():
        st_ref[...] = stm_ref[...]


def _state_out(j, st_ref, stm_ref):
    @pl.when(j == 0)
    def _():
        stm_ref[...] = st_ref[...]


def _skewed(i, pa_ref, pb_ref, body):
    @pl.when(i == 0)
    def _():
        pb_ref[...] = jnp.zeros(pb_ref.shape, pb_ref.dtype)

    @pl.when(i % 2 == 0)
    def _():
        body(pb_ref, pa_ref)

    @pl.when(i % 2 == 1)
    def _():
        body(pa_ref, pb_ref)


def _head_out(o, gate, nw):
    return (_rmsnorm(o, nw) * _silu(gate)).astype(BF16)


def _gla_block(q, k, v, g, gate, nw, wl_ref, lv, st_ref, p_ref, y_ref, fill):
    width = q.shape[1]
    per_tile = LANES * HEADS // width
    lane = lax.broadcasted_iota(jnp.int32, (1, LANES), 1)
    own_lanes = [jnp.where(lane // (LANES // per_tile) == j, 1.0, 0.0).astype(BF16) for j in range(per_tile)]

    def tile(a, h):
        t0 = (h // per_tile) * LANES
        return a[:, t0:t0 + LANES]

    def own(a_tile, h):
        return a_tile if per_tile == 1 else a_tile * own_lanes[h % per_tile]

    g16 = g.astype(BF16)
    b = _dot(wl_ref[W_CUM], g16)
    rowi = lax.broadcasted_iota(jnp.int32, (R, 1), 0)
    for lvl in range(N_LEVELS):
        hsz = 1 << lvl
        if hsz >= 8:
            b3 = b.reshape(R // (2 * hsz), 2 * hsz, width)
            b_m = jnp.broadcast_to(b3[:, hsz - 1:hsz, :], b3.shape).reshape(R, width)
            f = jnp.exp(-jnp.abs(b - b_m))
        else:
            f = jnp.exp(_dot(wl_ref[lvl], g16))
        lower = ((rowi >> lvl) & 1) == 1
        z = (jnp.where(lower, q, k) * f).astype(BF16)
        if 4 * hsz >= R:
            for h in range(HEADS):
                zt = tile(z, h)
                for r0 in range(0, R, 2 * hsz):
                    p_ref[h, r0 + hsz:r0 + 2 * hsz, r0:r0 + hsz] = _dot_nt(
                        own(zt[r0 + hsz:r0 + 2 * hsz], h), zt[r0:r0 + hsz])
        else:
            m = lv == lvl
            for h in range(HEADS):
                zt = tile(z, h)
                full = _dot_nt(own(zt, h), zt)
                if lvl == 0:
                    p_ref[h] = jnp.where(m, full, 0.0)
                else:
                    p_ref[h] = jnp.where(m, full, p_ref[h])
        _run_one(fill)
    q16 = q.astype(BF16)
    k16 = k.astype(BF16)
    m = lv == LV_DIAG
    for h in range(HEADS):
        p_ref[h] = jnp.where(m, _dot_nt(own(tile(q16, h), h), tile(k16, h)), p_ref[h])
    sfx = _dot(wl_ref[W_SFX], g16)
    qe = (q * jnp.exp(b)).astype(BF16)
    ke = (k * jnp.exp(sfx)).astype(BF16)
    dec = jnp.exp(b[R - 1:R, :])
    v16 = v.astype(BF16)
    for h in range(HEADS):
        sl = slice(h * HEAD_W, (h + 1) * HEAD_W)
        st = st_ref[h]
        o = _dot(p_ref[h].astype(BF16), v16[:, sl]) + _dot_nt(tile(qe, h), st.astype(BF16))
        st_ref[h] = st * tile(dec, h) + _dot_tn(v16[:, sl], own(tile(ke, h), h))
        y_ref[:, sl] = _head_out(o, gate[:, sl], nw)


def _residual_specs(hx, hm, block_of_step):
    hx_last = hx.shape[0] // R - 1
    hm_last = hm.shape[0] // R - 1
    return [pl.BlockSpec((R, D_MODEL), lambda i: (jnp.minimum(block_of_step(i)[0], hx_last), 0)),
            pl.BlockSpec((R, D_MODEL), lambda i: (hm_last, 0))]


def _mixer_call(kernel_fn, name, stream, stream_specs, n_blocks, operands, operand_specs, proj_cols, scratch,
                emit_u=False):
    y_spec = pl.BlockSpec((R, BRANCH_W), _seq_block(n_blocks, 1))
    y_shape = jax.ShapeDtypeStruct((n_blocks * R, BRANCH_W), BF16)
    u_spec = pl.BlockSpec((R, D_MODEL), _seq_block(n_blocks, 0))
    u_shape = jax.ShapeDtypeStruct((n_blocks * R, D_MODEL), BF16)
    return pl.pallas_call(
        kernel_fn,
        grid=(n_blocks + 1,),
        in_specs=stream_specs + operand_specs,
        out_specs=[y_spec, u_spec] if emit_u else y_spec,
        out_shape=[y_shape, u_shape] if emit_u else y_shape,
        scratch_shapes=scratch + [pltpu.VMEM((R, proj_cols), F32), pltpu.VMEM((R, proj_cols), F32)],
        compiler_params=_cparams(("arbitrary",)),
        name=name,
    )(*stream, *operands)


def _state_scratch():
    return [pltpu.VMEM((HEADS, HEAD_W, HEAD_W), F32), pltpu.VMEM((HEADS, HEAD_W, HEAD_W), F32)]


def _hgrn_kernel(nb, hx_ref, hm_ref, nm_ref, w_ref, lb_ref, nw_ref, wl_ref, lv_ref,
                 y_ref, u_ref, st_ref, stm_ref, p_ref, pa_ref, pb_ref):
    i = pl.program_id(0)
    j = i - 1
    _state_in(j, nb, st_ref, stm_ref)

    def body(rd, wr):
        a_q, f_in, a_i, a_g = [rd[:, c * BRANCH_W:(c + 1) * BRANCH_W] for c in range(4)]
        valid = _valid_rows(j == 0)
        lb_floor = lb_ref[0:1, :]
        one_m_lb = lb_ref[1:2, :]
        q = _silu(a_q) * (HEAD_W ** -0.5)
        e = jnp.exp(-jnp.abs(f_in))
        r = 1.0 / (1.0 + e)
        pos = f_in >= 0.0
        g = jnp.log(lb_floor + one_m_lb * jnp.where(pos, r, e * r))
        k = one_m_lb * jnp.where(pos, e * r, r) * valid
        u = _mixer_input(_residual_block(hx_ref, hm_ref, i == 0), nm_ref, _valid_rows(i == 0))
        u_ref[...] = u
        fill = _projection_steps(u, w_ref, wr)
        _gla_block(q, k, a_i, g, a_g, nw_ref[...], wl_ref, lv_ref[...], st_ref, p_ref, y_ref, fill)
        _run_all(fill)

    _skewed(i, pa_ref, pb_ref, body)
    _state_out(j, st_ref, stm_ref)


def _hgrn(hx, hm, n_blocks, norm_mix, w, lb_rows, norm_w, wl, lv, nb):
    return _mixer_call(
        functools.partial(_hgrn_kernel, nb), "hgrn2",
        [hx, hm], _residual_specs(hx, hm, _seq_block(n_blocks, 0)), n_blocks,
        [norm_mix, w, lb_rows, norm_w, wl, lv],
        [_const_spec((1, D_MODEL)), _const_spec(w.shape), _const_spec((8, BRANCH_W)),
         _const_spec((1, HEAD_W)), _const_spec(wl.shape), _const_spec((R, R))],
        w.shape[1], _state_scratch() + [pltpu.VMEM((HEADS, R, R), F32)], emit_u=True)


def _gla_kernel(nb, u_ref, w_ref, w2_ref, b2_ref, nw_ref, wl_ref, lv_ref,
                y_ref, st_ref, stm_ref, p_ref, pa_ref, pb_ref):
    i = pl.program_id(0)
    j = i - 1
    _state_in(j, nb, st_ref, stm_ref)

    def body(rd, wr):
        qk_w = HEADS * C_DK
        c_q = rd[:, :qk_w]
        c_k = rd[:, qk_w:2 * qk_w]
        c_v = rd[:, 2 * qk_w:2 * qk_w + BRANCH_W]
        c_g = rd[:, 2 * qk_w + BRANCH_W:2 * qk_w + 2 * BRANCH_W]
        gk_low = rd[:, 2 * qk_w + 2 * BRANCH_W:]
        z = _dot(gk_low.astype(BF16), w2_ref[...]) + b2_ref[...]
        g = _log_sigmoid(z) * (1.0 / C_GK_NORM)
        fill = _projection_steps(u_ref[...], w_ref, wr)
        _gla_block(c_q * (C_DK ** -0.5), c_k * _valid_rows(j == 0), c_v, g, c_g, nw_ref[...],
                   wl_ref, lv_ref[...], st_ref, p_ref, y_ref, fill)
        _run_all(fill)

    _skewed(i, pa_ref, pb_ref, body)
    _state_out(j, st_ref, stm_ref)


def _u_spec(n_blocks):
    return [pl.BlockSpec((R, D_MODEL), _seq_block(n_blocks, 0))]


def _gla(u, n_blocks, w, w2, b2, norm_w, wl, lv, nb):
    return _mixer_call(
        functools.partial(_gla_kernel, nb), "gla", [u], _u_spec(n_blocks), n_blocks,
        [w, w2, b2, norm_w, wl, lv],
        [_const_spec(w.shape), _const_spec(w2.shape),
         _const_spec(b2.shape), _const_spec((1, HEAD_W)), _const_spec(wl.shape), _const_spec((R, R))],
        w.shape[1], _state_scratch() + [pltpu.VMEM((HEADS, R, R), F32)])


TAIL = 8


def _gdn_kernel(nb, u_ref, w_ref, cw_ref, hp_ref, nw_ref, wl_ref, lv_ref,
                y_ref, st_ref, stm_ref, xx_ref, tailm_ref, x_ref, a_ref, pa_ref, pb_ref):
    i = pl.program_id(0)
    j = i - 1
    _state_in(j, nb, st_ref, stm_ref)

    @pl.when(j <= 0)
    def _():
        xx_ref[0:TAIL, :] = jnp.zeros((TAIL, 3 * BRANCH_W), F32)

    @pl.when((j >= 1) & ((j - 1) % nb == 0))
    def _():
        xx_ref[0:TAIL, :] = tailm_ref[...]

    def body(rd, wr):
        _gdn_body(j, rd, wr, u_ref, w_ref, cw_ref, hp_ref, nw_ref, wl_ref, lv_ref,
                  y_ref, st_ref, xx_ref, x_ref, a_ref)

    _skewed(i, pa_ref, pb_ref, body)

    @pl.when(j == 0)
    def _():
        tailm_ref[...] = xx_ref[0:TAIL, :]

    _state_out(j, st_ref, stm_ref)


def _gdn_body(j, rd, wr, u_ref, w_ref, cw_ref, hp_ref, nw_ref, wl_ref, lv_ref,
              y_ref, st_ref, xx_ref, x_ref, a_ref):
    fill = _projection_steps(u_ref[...], w_ref, wr)
    valid = _valid_rows(j == 0)
    lv = lv_ref[...]
    xx_ref[TAIL:, :] = rd[:, :3 * BRANCH_W]
    b_g = rd[:, 3 * BRANCH_W:4 * BRANCH_W]
    sm = rd[:, 4 * BRANCH_W:]
    conv = jnp.zeros((R, 3 * BRANCH_W), F32)
    for tap in range(CONV_W):
        off = TAIL - (CONV_W - 1) + tap
        conv = conv + xx_ref[off:off + R, :] * cw_ref[tap:tap + 1, :]
    qkv = _silu(conv)
    xx_ref[0:TAIL, :] = xx_ref[R:R + TAIL, :]

    a_neg = hp_ref[0:1, :]
    dt_b = hp_ref[1:2, :]
    log_a = a_neg * _softplus(sm + dt_b)
    la_hi = log_a.astype(BF16)
    la_lo = (log_a - la_hi.astype(F32)).astype(BF16)
    gcum = _dot(wl_ref[W_CUM], la_hi) + _dot(wl_ref[W_CUM], la_lo)
    gcum_t = gcum.T
    beta_all = _sigmoid(sm) * valid

    strict = (lv >= 0) & (lv < LV_DIAG)
    causal = lv >= 0
    eye = (lv == LV_DIAG).astype(F32)

    heads = []
    for h in range(HEADS):
        qh = qkv[:, h * HEAD_W:(h + 1) * HEAD_W]
        kh = qkv[:, BRANCH_W + h * HEAD_W:BRANCH_W + (h + 1) * HEAD_W] * valid
        vh = qkv[:, 2 * BRANCH_W + h * HEAD_W:2 * BRANCH_W + (h + 1) * HEAD_W]
        qh = qh * lax.rsqrt(jnp.sum(qh * qh, axis=-1, keepdims=True) + L2_EPS) * (HEAD_W ** -0.5)
        kh = kh * lax.rsqrt(jnp.sum(kh * kh, axis=-1, keepdims=True) + L2_EPS)
        beta = beta_all[:, SM_BETA + h:SM_BETA + h + 1]
        gcol = gcum[:, SM_DECAY + h:SM_DECAY + h + 1]
        grow = gcum_t[SM_DECAY + h:SM_DECAY + h + 1, :]
        dm = jnp.exp(jnp.minimum(gcol - grow, 0.0))
        q16 = qh.astype(BF16)
        k16 = kh.astype(BF16)
        a_ref[h] = jnp.where(strict, beta * _dot_nt(k16, k16) * dm, 0.0)
        qk = jnp.where(causal, _dot_nt(q16, k16) * dm, 0.0).astype(BF16)
        heads.append((qh, kh, vh, beta, gcol, qk))
        _run_one(fill)
        _run_one(fill)

    for lvl in range(N_LEVELS):
        m = lv == lvl
        for h in range(HEADS):
            l_lvl = jnp.where(m, a_ref[h], 0.0)
            if lvl == 0:
                x_ref[h] = eye - l_lvl
            else:
                xcur = x_ref[h].astype(BF16)
                x_ref[h] = x_ref[h] - _dot(_dot(xcur, l_lvl.astype(BF16)).astype(BF16), xcur)

    for h, (qh, kh, vh, beta, gcol, qk) in enumerate(heads):
        sl = slice(h * HEAD_W, (h + 1) * HEAD_W)
        eg = jnp.exp(gcol)
        rhs = jnp.concatenate([vh * beta, kh * (beta * eg)], axis=-1).astype(BF16)
        sol = _dot(x_ref[h].astype(BF16), rhs)
        u_h = sol[:, :HEAD_W]
        w_h = sol[:, HEAD_W:]
        st = st_ref[h]
        st16 = st.astype(BF16)
        v_new = u_h - _dot_nt(w_h.astype(BF16), st16)
        vn16 = v_new.astype(BF16)
        o = _dot_nt((qh * eg).astype(BF16), st16) + _dot(qk, vn16)
        g_last = gcol[R - 1:R, :]
        ke = (kh * jnp.exp(g_last - gcol)).astype(BF16)
        st_ref[h] = st * jnp.exp(g_last) + _dot_tn(vn16, ke)
        y_ref[:, sl] = _head_out(o, b_g[:, sl], nw_ref[...])
    _run_all(fill)


def _gdn(u, n_blocks, w, conv_w, head_params, norm_w, wl, lv, nb):
    return _mixer_call(
        functools.partial(_gdn_kernel, nb), "gdn", [u], _u_spec(n_blocks), n_blocks,
        [w, conv_w, head_params, norm_w, wl, lv],
        [_const_spec(w.shape), _const_spec((CONV_W, 3 * BRANCH_W)),
         _const_spec((8, LANES)), _const_spec((1, HEAD_W)), _const_spec(wl.shape), _const_spec((R, R))],
        w.shape[1], _state_scratch() + [pltpu.VMEM((R + TAIL, 3 * BRANCH_W), F32),
                            pltpu.VMEM((TAIL, 3 * BRANCH_W), F32),
                            pltpu.VMEM((HEADS, R, R), F32),
                            pltpu.VMEM((HEADS, R, R), F32)])


def _split_hi_lo(x):
    hi = x.astype(BF16)
    return hi, (x - hi.astype(F32)).astype(BF16)


def _merge_kernel(ya_ref, yb_ref, yc_ref, u_ref, hx_ref, hm_ref, wgate_ref, wb_ref, wo_ref, nf_ref,
                  wr_hi_ref, wr_lo_ref, br_ref, hn_ref, u2_ref, cmb_ref):
    is_meta = pl.program_id(0) == pl.num_programs(0) - 1
    h = _residual_block(hx_ref, hm_ref, is_meta)
    u = u_ref[...]
    merged = jnp.zeros((R, D_MODEL), F32)
    for n, y_ref in enumerate((ya_ref, yb_ref, yc_ref)):
        gate = _sigmoid(_dot(u, wgate_ref[:, n * D_MODEL:(n + 1) * D_MODEL]))
        merged = merged + gate * _dot(y_ref[...], wb_ref[0, n])
    hn = h + _dot(merged.astype(BF16), wo_ref[0])
    hn_ref[...] = hn
    u2 = _rmsnorm(hn, nf_ref[...])
    u2_ref[...] = u2.astype(BF16)

    u_hi, u_lo = _split_hi_lo(u2)
    logits = (_dot(u_hi, wr_hi_ref[...]) + _dot(u_hi, wr_lo_ref[...]) + _dot(u_lo, wr_hi_ref[...])
              + br_ref[...])
    lane_i = lax.broadcasted_iota(jnp.int32, (R, LANES), 1)
    lane = lane_i.astype(F32)
    lane_grp = (lane_i // EXP_PER_GROUP).astype(F32)
    neg = jnp.float32(-jnp.inf)
    big = jnp.float32(1e9)
    is_g = (lane_i >= N_EXPERTS) & (lane_i < N_EXPERTS + N_GROUPS)
    lg = jnp.where(is_g, logits, neg)
    mg = jnp.max(lg, axis=-1, keepdims=True)
    zg = jnp.sum(jnp.exp(lg - mg), axis=-1, keepdims=True)
    g_val = 1.0 / zg
    g_idx = jnp.min(jnp.where(lg == mg, lane, big), axis=-1, keepdims=True) - N_EXPERTS
    in_grp = (lane_i < N_EXPERTS) & (lane_grp == g_idx)
    le = jnp.where(in_grp, logits, neg)
    m1 = jnp.max(le, axis=-1, keepdims=True)
    ze = jnp.sum(jnp.exp(le - m1), axis=-1, keepdims=True)
    i1 = jnp.min(jnp.where(le == m1, lane, big), axis=-1, keepdims=True)
    le2 = jnp.where(lane == i1, neg, le)
    m2 = jnp.max(le2, axis=-1, keepdims=True)
    i2 = jnp.min(jnp.where(le2 == m2, lane, big), axis=-1, keepdims=True)
    p1 = 1.0 / ze
    p2 = jnp.exp(m2 - m1) / ze
    den = p1 + p2
    cmb = g_val * jnp.where(lane == i1, p1 / den, jnp.where(lane == i2, p2 / den, 0.0))
    cmb_ref[...] = jnp.where(lane_i == CMB_GROUP_LANE, g_idx, cmb)


def _merge(ya, yb, yc, u, hx, hm, w_gate, wb, wo, layer, nf, wr_hi, wr_lo, br):
    t = ya.shape[0]
    row = lambda w: pl.BlockSpec((R, w), lambda i: (i, 0))
    return pl.pallas_call(
        _merge_kernel,
        grid=(t // R,),
        in_specs=[row(BRANCH_W), row(BRANCH_W), row(BRANCH_W), row(D_MODEL)]
        + _residual_specs(hx, hm, lambda i: (i, 0))
        + [_const_spec((D_MODEL, N_BRANCH * D_MODEL)),
                  pl.BlockSpec((1, N_BRANCH, BRANCH_W, D_MODEL), lambda i: (layer, 0, 0, 0)),
                  pl.BlockSpec((1, D_MODEL, D_MODEL), lambda i: (layer, 0, 0)),
                  _const_spec((1, D_MODEL)), _const_spec((D_MODEL, LANES)), _const_spec((D_MODEL, LANES)),
                  _const_spec((1, LANES))],
        out_specs=[row(D_MODEL), row(D_MODEL), row(LANES)],
        out_shape=[jax.ShapeDtypeStruct((t, D_MODEL), F32),
                   jax.ShapeDtypeStruct((t, D_MODEL), BF16),
                   jax.ShapeDtypeStruct((t, LANES), F32)],
        compiler_params=_cparams(("arbitrary",)),
        name="merge_router",
    )(ya, yb, yc, u, hx, hm, w_gate, wb, wo, nf, wr_hi, wr_lo, br)


def _moe_kernel(final, cnt_ref, u_ref, c_ref, h_ref, wg_ref, wu_ref, wd_ref, tri_ref, nfin_ref, o_ref):
    w = pl.program_id(0)
    g = pl.program_id(1)
    u = u_ref[...]
    cmb = c_ref[...]
    gid_row = cmb.T[CMB_GROUP_LANE:CMB_GROUP_LANE + 1, :]
    in_row = jnp.where(gid_row == g.astype(F32), 1.0, 0.0)
    ranks = []
    before = jnp.zeros((1, 1), F32)
    for j in range(MOE_W // R):
        seg = in_row[:, j * R:(j + 1) * R]
        ranks.append(_dot(jnp.broadcast_to(seg, (16, R)).astype(BF16), tri_ref[...])[0:1, :] + before)
        before = before + jnp.sum(seg, axis=-1, keepdims=True)
    rank_row = jnp.concatenate(ranks, axis=-1)
    cmb_hi, cmb_lo = _split_hi_lo(cmb)

    @pl.when(g == 0)
    def _():
        o_ref[...] = h_ref[...]

    count = cnt_ref[w * N_GROUPS + g]
    n_pass = (count + MOE_CAPS[-1] - 1) // MOE_CAPS[-1]
    per_pass = (count + jnp.maximum(n_pass, 1) - 1) // jnp.maximum(n_pass, 1)

    def run(cap):
        slot = lax.broadcasted_iota(jnp.int32, (cap, 1), 0).astype(F32)
        lane = lax.broadcasted_iota(jnp.int32, (cap, LANES), 1)

        def one_pass(c, carry):
            base = (c * cap).astype(F32)
            sel = jnp.where((in_row > 0.0) & (rank_row - base == slot), 1.0, 0.0).astype(BF16)
            x = _dot(sel, u).astype(BF16)
            cw = _dot(sel, cmb_hi) + _dot(sel, cmb_lo)
            y = jnp.zeros((cap, D_MODEL), F32)
            for e in range(EXP_PER_GROUP):
                ce = jnp.sum(jnp.where(lane == g * EXP_PER_GROUP + e, cw, 0.0), axis=-1, keepdims=True)
                hid = _silu(_dot(x, wg_ref[0, 0, e])) * _dot(x, wu_ref[0, 0, e]) * ce
                y = y + _dot(hid.astype(BF16), wd_ref[0, 0, e])
            o_ref[...] += _dot_tn(sel, y.astype(BF16))
            return carry

        lax.fori_loop(0, n_pass, one_pass, 0)

    below = 0
    for cap in MOE_CAPS:
        pl.when((per_pass > below) & (per_pass <= cap))(functools.partial(run, cap))
        below = cap

    if final:
        @pl.when(g == N_GROUPS - 1)
        def _():
            o_ref[...] = _rmsnorm(o_ref[...], nfin_ref[...])


def _moe(u2, cmb, h, wg, wu, wd, layer, norm_final, final):
    t = h.shape[0]
    n_win = t // MOE_W
    assert t % MOE_W == 0
    gid = cmb[:, CMB_GROUP_LANE].astype(jnp.int32).reshape(n_win, MOE_W)
    counts = jnp.sum(gid[:, :, None] == jnp.arange(N_GROUPS, dtype=jnp.int32), axis=1, dtype=jnp.int32)
    tri = jnp.asarray(np.triu(np.ones((R, R), np.float32), 1), dtype=BF16)
    row = lambda width: pl.BlockSpec((MOE_W, width), lambda w, g, cnt: (w, 0))
    wspec = lambda a, b: pl.BlockSpec((1, 1, EXP_PER_GROUP, a, b), lambda w, g, cnt: (layer, g, 0, 0, 0))
    return pl.pallas_call(
        functools.partial(_moe_kernel, final),
        grid_spec=pltpu.PrefetchScalarGridSpec(
            num_scalar_prefetch=1,
            grid=(n_win, N_GROUPS),
            in_specs=[row(D_MODEL), row(LANES),
                      pl.BlockSpec((MOE_W, D_MODEL), lambda w, g, cnt: (w, 0), pipeline_mode=pl.Buffered(1)),
                      wspec(D_MODEL, EXPERT_HIDDEN), wspec(D_MODEL, EXPERT_HIDDEN),
                      wspec(EXPERT_HIDDEN, D_MODEL),
                      pl.BlockSpec((R, R), lambda w, g, cnt: (0, 0)),
                      pl.BlockSpec((1, D_MODEL), lambda w, g, cnt: (0, 0))],
            out_specs=row(D_MODEL)),
        out_shape=jax.ShapeDtypeStruct((t - R if final else t, D_MODEL), F32),
        compiler_params=_cparams(("arbitrary", "arbitrary")),
        name="moe",
    )(counts.reshape(-1), u2, cmb, h, wg, wu, wd, tri, norm_final)


IN_SPLITS = (512, 512, 512, 512, 1536, 4, 4, 512, 256, 256, 512, 16, 512, 3072)


def _inproj_weights(w_in):
    offs = np.cumsum((0,) + IN_SPLITS)
    (a_q, a_f, a_i, a_g, b_qkv, b_beta, b_decay, b_g, c_q, c_k, c_v, c_gk, c_g, gates) = [
        w_in[:, offs[j]:offs[j + 1]] for j in range(len(IN_SPLITS))]
    zeros = lambda n: jnp.zeros((D_MODEL, n), w_in.dtype)
    w_a = w_in[:, offs[0]:offs[4]]
    w_b = jnp.concatenate([b_qkv, b_g, b_beta, b_decay, zeros(LANES - SM_GK)], axis=-1)
    w_c = jnp.concatenate([c_q, c_k, c_v, c_g,
                           zeros(SM_GK), c_gk, zeros(LANES - SM_GK - C_GK_RANK)], axis=-1)
    return [w.astype(BF16) for w in (w_a, w_b, w_c, gates)]


def kernel(x, meta_tokens, norm_mix, w_in, hgrn_lb_logits, hgrn_norm, gdn_conv, gdn_a_log,
           gdn_dt_bias, gdn_norm, gla_gk_w2, gla_gk_b, gla_norm, w_branch, w_out, norm_ffn,
           router_group_w, router_group_b, router_expert_w, router_expert_b,
           expert_w_gate, expert_w_up, expert_w_down, norm_final):
    batch, seq, d = x.shape
    assert d == D_MODEL and seq % R == 0
    depth = w_in.shape[0]
    nb = seq // R
    lv_np, wl_np = _level_constants()
    lv = jnp.asarray(lv_np)
    wl = jnp.asarray(wl_np, dtype=BF16)

    n_blocks = batch * nb + 1
    hx = x.reshape(batch * seq, d)
    hm = jnp.concatenate([jnp.zeros((R - N_META, d), F32), meta_tokens.astype(F32)], axis=0)

    wg16, wu16, wd16, wb16, wo16 = (w.astype(BF16) for w in (
        expert_w_gate, expert_w_up, expert_w_down, w_branch, w_out))

    lb_p = jax.nn.softmax(hgrn_lb_logits.astype(F32), axis=0)
    lb_all = jnp.maximum(jnp.cumsum(lb_p, axis=0) - lb_p[0:1], 0.0)

    for layer in range(depth):
        w_a, w_b, w_c, w_gate = _inproj_weights(w_in[layer])
        nm = norm_mix[layer][None, :]

        lb = lb_all[layer]
        lb_rows = jnp.zeros((8, BRANCH_W), F32)
        lb_rows = lb_rows.at[0].set(jnp.maximum(lb, LB_FLOOR)).at[1].set(1.0 - lb)
        y_a, u = _hgrn(hx, hm, n_blocks, nm, w_a, lb_rows, hgrn_norm[layer][None, :], wl, lv, nb)

        head_params = jnp.zeros((8, LANES), F32)
        head_params = head_params.at[0, SM_DECAY:SM_DECAY + HEADS].set(-jnp.exp(gdn_a_log[layer]))
        head_params = head_params.at[1, SM_DECAY:SM_DECAY + HEADS].set(gdn_dt_bias[layer])
        y_b = _gdn(u, n_blocks, w_b, gdn_conv[layer], head_params, gdn_norm[layer][None, :], wl, lv, nb)

        w2 = jnp.zeros((LANES, HEADS * C_DK), F32).at[SM_GK:SM_GK + C_GK_RANK].set(gla_gk_w2[layer])
        y_c = _gla(u, n_blocks, w_c, w2.astype(BF16), gla_gk_b[layer][None, :],
                   gla_norm[layer][None, :], wl, lv, nb)

        wr = jnp.zeros((D_MODEL, LANES), F32)
        wr = wr.at[:, :N_EXPERTS].set(router_expert_w[layer])
        wr = wr.at[:, N_EXPERTS:N_EXPERTS + N_GROUPS].set(router_group_w[layer])
        br = jnp.zeros((1, LANES), F32)
        br = br.at[0, :N_EXPERTS].set(router_expert_b[layer])
        br = br.at[0, N_EXPERTS:N_EXPERTS + N_GROUPS].set(router_group_b[layer])
        wr_hi = wr.astype(BF16)
        wr_lo = (wr - wr_hi.astype(F32)).astype(BF16)
        hn, u2, cmb = _merge(y_a, y_b, y_c, u, hx, hm, w_gate, wb16, wo16, layer,
                             norm_ffn[layer][None, :], wr_hi, wr_lo, br)

        hx = hm = _moe(u2, cmb, hn, wg16, wu16, wd16, layer, norm_final[None, :], final=(layer == depth - 1))

    return hx.reshape(batch, seq, d)
```

```python
import functools

import numpy as np
import jax
import jax.numpy as jnp
from jax import lax
from jax.experimental import pallas as pl
from jax.experimental.pallas import tpu as pltpu

F32 = jnp.float32
BF16 = jnp.bfloat16

D_MODEL = 1024
N_META = 16
CONV_W = 4
RMS_EPS = 1e-6
L2_EPS = 1e-6
LB_FLOOR = 1e-30
HEADS = 4
HEAD_W = 128
BRANCH_W = HEADS * HEAD_W
C_DK = 64
C_GK_RANK = 16
C_GK_NORM = 16.0
N_BRANCH = 3
N_GROUPS = 4
EXP_PER_GROUP = 8
N_EXPERTS = N_GROUPS * EXP_PER_GROUP
EXPERT_HIDDEN = 256

R = 256
N_LEVELS = 8
LV_DIAG = N_LEVELS
W_CUM = N_LEVELS
W_SFX = N_LEVELS + 1
LANES = 128

SM_BETA, SM_DECAY, SM_GK = 0, 4, 8

VMEM_LIMIT = 56 * 1024 * 1024
MOE_W = 1280
MOE_CAPS = (320, 384, 448)
CMB_GROUP_LANE = N_EXPERTS


def _cparams(sem):
    return pltpu.CompilerParams(dimension_semantics=sem, vmem_limit_bytes=VMEM_LIMIT)


def _const_spec(shape):
    nd = len(shape)
    return pl.BlockSpec(shape, lambda *_: (0,) * nd)


@functools.lru_cache(maxsize=None)
def _level_constants():
    t = np.arange(R)[:, None]
    s = np.arange(R)[None, :]
    x = np.maximum(t ^ s, 1)
    lv = np.where(s < t, np.floor(np.log2(x)).astype(np.int32), np.where(s == t, LV_DIAG, -1)).astype(np.int32)
    w = np.zeros((N_LEVELS + 2, R, R), np.float32)
    for l in range(N_LEVELS):
        hsz = 1 << l
        for r in range(R):
            hb = (r // hsz) * hsz
            if (r >> l) & 1:
                w[l, r, hb:r + 1] = 1.0
            else:
                w[l, r, r + 1:hb + hsz] = 1.0
    w[W_CUM] = np.tril(np.ones((R, R), np.float32))
    w[W_SFX] = np.triu(np.ones((R, R), np.float32), 1)
    return lv, w


def _dot(a, b):
    return jnp.dot(a, b, preferred_element_type=F32)


def _dot_nt(a, b):
    return lax.dot_general(a, b, (((1,), (1,)), ((), ())), preferred_element_type=F32)


def _dot_tn(a, b):
    return lax.dot_general(a, b, (((0,), (0,)), ((), ())), preferred_element_type=F32)


def _sigmoid(x):
    return 1.0 / (1.0 + jnp.exp(-x))


def _silu(x):
    return x * _sigmoid(x)


def _softplus(x):
    return jnp.maximum(x, 0.0) + jnp.log1p(jnp.exp(-jnp.abs(x)))


def _log_sigmoid(x):
    return -_softplus(-x)


def _rmsnorm(x, w):
    return x * lax.rsqrt(jnp.mean(x * x, axis=-1, keepdims=True) + RMS_EPS) * w


def _valid_rows(is_meta):
    row = lax.broadcasted_iota(jnp.int32, (R, 1), 0)
    first_valid = jnp.where(is_meta, R - N_META, 0)
    return jnp.where(row >= first_valid, 1.0, 0.0).astype(F32)


def _residual_block(hx_ref, hm_ref, is_meta):
    return jnp.where(is_meta, hm_ref[...], hx_ref[...])


def _mixer_input(h, nw_ref, valid):
    return (_rmsnorm(h, nw_ref[...]) * valid).astype(BF16)


PROJ_CHUNK = 256


def _projection_steps(u, w_ref, out_ref):
    n_cols = w_ref.shape[1]

    def step(c0):
        sl = slice(c0, min(c0 + PROJ_CHUNK, n_cols))
        out_ref[:, sl] = _dot(u, w_ref[:, sl])

    return [functools.partial(step, c0) for c0 in range(0, n_cols, PROJ_CHUNK)]


def _run_one(steps):
    if steps:
        steps.pop(0)()


def _run_all(steps):
    while steps:
        steps.pop(0)()


def _seq_block(n_blocks, lag):
    return lambda i: ((jnp.clip(i - lag, 0, n_blocks - 1) + n_blocks - 1) % n_blocks, 0)


def _state_in(j, nb, st_ref, stm_ref):
    @pl.when(j <= 0)
    def _():
        st_ref[...] = jnp.zeros(st_ref.shape, st_ref.dtype)

    @pl.when((j >= 1) & ((j - 1) % nb == 0))
    def _():
        st_ref[...] = stm_ref[...]


def _state_out(j, st_ref, stm_ref):
    @pl.when(j == 0)
    def _():
        stm_ref[...] = st_ref[...]


def _skewed(i, pa_ref, pb_ref, body):
    @pl.when(i == 0)
    def _():
        pb_ref[...] = jnp.zeros(pb_ref.shape, pb_ref.dtype)

    @pl.when(i % 2 == 0)
    def _():
        body(pb_ref, pa_ref)

    @pl.when(i % 2 == 1)
    def _():
        body(pa_ref, pb_ref)


def _head_out(o, gate, nw):
    return (_rmsnorm(o, nw) * _silu(gate)).astype(BF16)


def _gla_block(q, k, v, g, gate, nw, wl_ref, lv, st_ref, p_ref, y_ref, fill):
    width = q.shape[1]
    per_tile = LANES * HEADS // width
    lane = lax.broadcasted_iota(jnp.int32, (1, LANES), 1)
    own_lanes = [jnp.where(lane // (LANES // per_tile) == j, 1.0, 0.0).astype(BF16) for j in range(per_tile)]

    def tile(a, h):
        t0 = (h // per_tile) * LANES
        return a[:, t0:t0 + LANES]

    def own(a_tile, h):
        return a_tile if per_tile == 1 else a_tile * own_lanes[h % per_tile]

    g16 = g.astype(BF16)
    b = _dot(wl_ref[W_CUM], g16)
    rowi = lax.broadcasted_iota(jnp.int32, (R, 1), 0)
    for lvl in range(N_LEVELS):
        hsz = 1 << lvl
        if hsz >= 8:
            b3 = b.reshape(R // (2 * hsz), 2 * hsz, width)
            b_m = jnp.broadcast_to(b3[:, hsz - 1:hsz, :], b3.shape).reshape(R, width)
            f = jnp.exp(-jnp.abs(b - b_m))
        else:
            f = jnp.exp(_dot(wl_ref[lvl], g16))
        lower = ((rowi >> lvl) & 1) == 1
        z = (jnp.where(lower, q, k) * f).astype(BF16)
        if 4 * hsz >= R:
            for h in range(HEADS):
                zt = tile(z, h)
                for r0 in range(0, R, 2 * hsz):
                    p_ref[h, r0 + hsz:r0 + 2 * hsz, r0:r0 + hsz] = _dot_nt(
                        own(zt[r0 + hsz:r0 + 2 * hsz], h), zt[r0:r0 + hsz])
        else:
            m = lv == lvl
            for h in range(HEADS):
                zt = tile(z, h)
                full = _dot_nt(own(zt, h), zt)
                if lvl == 0:
                    p_ref[h] = jnp.where(m, full, 0.0)
                else:
                    p_ref[h] = jnp.where(m, full, p_ref[h])
        _run_one(fill)
    q16 = q.astype(BF16)
    k16 = k.astype(BF16)
    m = lv == LV_DIAG
    for h in range(HEADS):
        p_ref[h] = jnp.where(m, _dot_nt(own(tile(q16, h), h), tile(k16, h)), p_ref[h])
    sfx = _dot(wl_ref[W_SFX], g16)
    qe = (q * jnp.exp(b)).astype(BF16)
    ke = (k * jnp.exp(sfx)).astype(BF16)
    dec = jnp.exp(b[R - 1:R, :])
    v16 = v.astype(BF16)
    for h in range(HEADS):
        sl = slice(h * HEAD_W, (h + 1) * HEAD_W)
        st = st_ref[h]
        o = _dot(p_ref[h].astype(BF16), v16[:, sl]) + _dot_nt(tile(qe, h), st.astype(BF16))
        st_ref[h] = st * tile(dec, h) + _dot_tn(v16[:, sl], own(tile(ke, h), h))
        y_ref[:, sl] = _head_out(o, gate[:, sl], nw)


def _residual_specs(hx, hm, block_of_step):
    hx_last = hx.shape[0] // R - 1
    hm_last = hm.shape[0] // R - 1
    return [pl.BlockSpec((R, D_MODEL), lambda i: (jnp.minimum(block_of_step(i)[0], hx_last), 0)),
            pl.BlockSpec((R, D_MODEL), lambda i: (hm_last, 0))]


def _mixer_call(kernel_fn, name, stream, stream_specs, n_blocks, operands, operand_specs, proj_cols, scratch,
                emit_u=False):
    y_spec = pl.BlockSpec((R, BRANCH_W), _seq_block(n_blocks, 1))
    y_shape = jax.ShapeDtypeStruct((n_blocks * R, BRANCH_W), BF16)
    u_spec = pl.BlockSpec((R, D_MODEL), _seq_block(n_blocks, 0))
    u_shape = jax.ShapeDtypeStruct((n_blocks * R, D_MODEL), BF16)
    return pl.pallas_call(
        kernel_fn,
        grid=(n_blocks + 1,),
        in_specs=stream_specs + operand_specs,
        out_specs=[y_spec, u_spec] if emit_u else y_spec,
        out_shape=[y_shape, u_shape] if emit_u else y_shape,
        scratch_shapes=scratch + [pltpu.VMEM((R, proj_cols), F32), pltpu.VMEM((R, proj_cols), F32)],
        compiler_params=_cparams(("arbitrary",)),
        name=name,
    )(*stream, *operands)


def _state_scratch():
    return [pltpu.VMEM((HEADS, HEAD_W, HEAD_W), F32), pltpu.VMEM((HEADS, HEAD_W, HEAD_W), F32)]


def _hgrn_kernel(nb, hx_ref, hm_ref, nm_ref, w_ref, lb_ref, nw_ref, wl_ref, lv_ref,
                 y_ref, u_ref, st_ref, stm_ref, p_ref, pa_ref, pb_ref):
    i = pl.program_id(0)
    j = i - 1
    _state_in(j, nb, st_ref, stm_ref)

    def body(rd, wr):
        a_q, f_in, a_i, a_g = [rd[:, c * BRANCH_W:(c + 1) * BRANCH_W] for c in range(4)]
        valid = _valid_rows(j == 0)
        lb_floor = lb_ref[0:1, :]
        one_m_lb = lb_ref[1:2, :]
        q = _silu(a_q) * (HEAD_W ** -0.5)
        e = jnp.exp(-jnp.abs(f_in))
        r = 1.0 / (1.0 + e)
        pos = f_in >= 0.0
        g = jnp.log(lb_floor + one_m_lb * jnp.where(pos, r, e * r))
        k = one_m_lb * jnp.where(pos, e * r, r) * valid
        u = _mixer_input(_residual_block(hx_ref, hm_ref, i == 0), nm_ref, _valid_rows(i == 0))
        u_ref[...] = u
        fill = _projection_steps(u, w_ref, wr)
        _gla_block(q, k, a_i, g, a_g, nw_ref[...], wl_ref, lv_ref[...], st_ref, p_ref, y_ref, fill)
        _run_all(fill)

    _skewed(i, pa_ref, pb_ref, body)
    _state_out(j, st_ref, stm_ref)


def _hgrn(hx, hm, n_blocks, norm_mix, w, lb_rows, norm_w, wl, lv, nb):
    return _mixer_call(
        functools.partial(_hgrn_kernel, nb), "hgrn2",
        [hx, hm], _residual_specs(hx, hm, _seq_block(n_blocks, 0)), n_blocks,
        [norm_mix, w, lb_rows, norm_w, wl, lv],
        [_const_spec((1, D_MODEL)), _const_spec(w.shape), _const_spec((8, BRANCH_W)),
         _const_spec((1, HEAD_W)), _const_spec(wl.shape), _const_spec((R, R))],
        w.shape[1], _state_scratch() + [pltpu.VMEM((HEADS, R, R), F32)], emit_u=True)


def _gla_kernel(nb, u_ref, w_ref, w2_ref, b2_ref, nw_ref, wl_ref, lv_ref,
                y_ref, st_ref, stm_ref, p_ref, pa_ref, pb_ref):
    i = pl.program_id(0)
    j = i - 1
    _state_in(j, nb, st_ref, stm_ref)

    def body(rd, wr):
        qk_w = HEADS * C_DK
        c_q = rd[:, :qk_w]
        c_k = rd[:, qk_w:2 * qk_w]
        c_v = rd[:, 2 * qk_w:2 * qk_w + BRANCH_W]
        c_g = rd[:, 2 * qk_w + BRANCH_W:2 * qk_w + 2 * BRANCH_W]
        gk_low = rd[:, 2 * qk_w + 2 * BRANCH_W:]
        z = _dot(gk_low.astype(BF16), w2_ref[...]) + b2_ref[...]
        g = _log_sigmoid(z) * (1.0 / C_GK_NORM)
        fill = _projection_steps(u_ref[...], w_ref, wr)
        _gla_block(c_q * (C_DK ** -0.5), c_k * _valid_rows(j == 0), c_v, g, c_g, nw_ref[...],
                   wl_ref, lv_ref[...], st_ref, p_ref, y_ref, fill)
        _run_all(fill)

    _skewed(i, pa_ref, pb_ref, body)
    _state_out(j, st_ref, stm_ref)


def _u_spec(n_blocks):
    return [pl.BlockSpec((R, D_MODEL), _seq_block(n_blocks, 0))]


def _gla(u, n_blocks, w, w2, b2, norm_w, wl, lv, nb):
    return _mixer_call(
        functools.partial(_gla_kernel, nb), "gla", [u], _u_spec(n_blocks), n_blocks,
        [w, w2, b2, norm_w, wl, lv],
        [_const_spec(w.shape), _const_spec(w2.shape),
         _const_spec(b2.shape), _const_spec((1, HEAD_W)), _const_spec(wl.shape), _const_spec((R, R))],
        w.shape[1], _state_scratch() + [pltpu.VMEM((HEADS, R, R), F32)])


TAIL = 8


def _gdn_kernel(nb, u_ref, w_ref, cw_ref, hp_ref, nw_ref, wl_ref, lv_ref,
                y_ref, st_ref, stm_ref, xx_ref, tailm_ref, x_ref, a_ref, pa_ref, pb_ref):
    i = pl.program_id(0)
    j = i - 1
    _state_in(j, nb, st_ref, stm_ref)

    @pl.when(j <= 0)
    def _():
        xx_ref[0:TAIL, :] = jnp.zeros((TAIL, 3 * BRANCH_W), F32)

    @pl.when((j >= 1) & ((j - 1) % nb == 0))
    def _():
        xx_ref[0:TAIL, :] = tailm_ref[...]

    def body(rd, wr):
        _gdn_body(j, rd, wr, u_ref, w_ref, cw_ref, hp_ref, nw_ref, wl_ref, lv_ref,
                  y_ref, st_ref, xx_ref, x_ref, a_ref)

    _skewed(i, pa_ref, pb_ref, body)

    @pl.when(j == 0)
    def _():
        tailm_ref[...] = xx_ref[0:TAIL, :]

    _state_out(j, st_ref, stm_ref)


def _gdn_body(j, rd, wr, u_ref, w_ref, cw_ref, hp_ref, nw_ref, wl_ref, lv_ref,
              y_ref, st_ref, xx_ref, x_ref, a_ref):
    fill = _projection_steps(u_ref[...], w_ref, wr)
    valid = _valid_rows(j == 0)
    lv = lv_ref[...]
    xx_ref[TAIL:, :] = rd[:, :3 * BRANCH_W]
    b_g = rd[:, 3 * BRANCH_W:4 * BRANCH_W]
    sm = rd[:, 4 * BRANCH_W:]
    conv = jnp.zeros((R, 3 * BRANCH_W), F32)
    for tap in range(CONV_W):
        off = TAIL - (CONV_W - 1) + tap
        conv = conv + xx_ref[off:off + R, :] * cw_ref[tap:tap + 1, :]
    qkv = _silu(conv)
    xx_ref[0:TAIL, :] = xx_ref[R:R + TAIL, :]

    a_neg = hp_ref[0:1, :]
    dt_b = hp_ref[1:2, :]
    log_a = a_neg * _softplus(sm + dt_b)
    la_hi = log_a.astype(BF16)
    la_lo = (log_a - la_hi.astype(F32)).astype(BF16)
    gcum = _dot(wl_ref[W_CUM], la_hi) + _dot(wl_ref[W_CUM], la_lo)
    gcum_t = gcum.T
    beta_all = _sigmoid(sm) * valid

    strict = (lv >= 0) & (lv < LV_DIAG)
    causal = lv >= 0
    eye = (lv == LV_DIAG).astype(F32)

    heads = []
    for h in range(HEADS):
        qh = qkv[:, h * HEAD_W:(h + 1) * HEAD_W]
        kh = qkv[:, BRANCH_W + h * HEAD_W:BRANCH_W + (h + 1) * HEAD_W] * valid
        vh = qkv[:, 2 * BRANCH_W + h * HEAD_W:2 * BRANCH_W + (h + 1) * HEAD_W]
        qh = qh * lax.rsqrt(jnp.sum(qh * qh, axis=-1, keepdims=True) + L2_EPS) * (HEAD_W ** -0.5)
        kh = kh * lax.rsqrt(jnp.sum(kh * kh, axis=-1, keepdims=True) + L2_EPS)
        beta = beta_all[:, SM_BETA + h:SM_BETA + h + 1]
        gcol = gcum[:, SM_DECAY + h:SM_DECAY + h + 1]
        grow = gcum_t[SM_DECAY + h:SM_DECAY + h + 1, :]
        dm = jnp.exp(jnp.minimum(gcol - grow, 0.0))
        q16 = qh.astype(BF16)
        k16 = kh.astype(BF16)
        a_ref[h] = jnp.where(strict, beta * _dot_nt(k16, k16) * dm, 0.0)
        qk = jnp.where(causal, _dot_nt(q16, k16) * dm, 0.0).astype(BF16)
        heads.append((qh, kh, vh, beta, gcol, qk))
        _run_one(fill)
        _run_one(fill)

    for lvl in range(N_LEVELS):
        m = lv == lvl
        for h in range(HEADS):
            l_lvl = jnp.where(m, a_ref[h], 0.0)
            if lvl == 0:
                x_ref[h] = eye - l_lvl
            else:
                xcur = x_ref[h].astype(BF16)
                x_ref[h] = x_ref[h] - _dot(_dot(xcur, l_lvl.astype(BF16)).astype(BF16), xcur)

    for h, (qh, kh, vh, beta, gcol, qk) in enumerate(heads):
        sl = slice(h * HEAD_W, (h + 1) * HEAD_W)
        eg = jnp.exp(gcol)
        rhs = jnp.concatenate([vh * beta, kh * (beta * eg)], axis=-1).astype(BF16)
        sol = _dot(x_ref[h].astype(BF16), rhs)
        u_h = sol[:, :HEAD_W]
        w_h = sol[:, HEAD_W:]
        st = st_ref[h]
        st16 = st.astype(BF16)
        v_new = u_h - _dot_nt(w_h.astype(BF16), st16)
        vn16 = v_new.astype(BF16)
        o = _dot_nt((qh * eg).astype(BF16), st16) + _dot(qk, vn16)
        g_last = gcol[R - 1:R, :]
        ke = (kh * jnp.exp(g_last - gcol)).astype(BF16)
        st_ref[h] = st * jnp.exp(g_last) + _dot_tn(vn16, ke)
        y_ref[:, sl] = _head_out(o, b_g[:, sl], nw_ref[...])
    _run_all(fill)


def _gdn(u, n_blocks, w, conv_w, head_params, norm_w, wl, lv, nb):
    return _mixer_call(
        functools.partial(_gdn_kernel, nb), "gdn", [u], _u_spec(n_blocks), n_blocks,
        [w, conv_w, head_params, norm_w, wl, lv],
        [_const_spec(w.shape), _const_spec((CONV_W, 3 * BRANCH_W)),
         _const_spec((8, LANES)), _const_spec((1, HEAD_W)), _const_spec(wl.shape), _const_spec((R, R))],
        w.shape[1], _state_scratch() + [pltpu.VMEM((R + TAIL, 3 * BRANCH_W), F32),
                            pltpu.VMEM((TAIL, 3 * BRANCH_W), F32),
                            pltpu.VMEM((HEADS, R, R), F32),
                            pltpu.VMEM((HEADS, R, R), F32)])


def _split_hi_lo(x):
    hi = x.astype(BF16)
    return hi, (x - hi.astype(F32)).astype(BF16)


def _merge_kernel(ya_ref, yb_ref, yc_ref, u_ref, hx_ref, hm_ref, wgate_ref, wb_ref, wo_ref, nf_ref,
                  wr_hi_ref, wr_lo_ref, br_ref, hn_ref, u2_ref, cmb_ref, lg_ref):
    i = pl.program_id(0)
    last = pl.num_programs(0) - 2

    @pl.when(i == 0)
    def _():
        lg_ref[...] = jnp.zeros(lg_ref.shape, lg_ref.dtype)

    cmb_ref[...] = _route(lg_ref[...])

    is_meta = i >= last
    h = _residual_block(hx_ref, hm_ref, is_meta)
    u = u_ref[...]
    merged = jnp.zeros((R, D_MODEL), F32)
    for n, y_ref in enumerate((ya_ref, yb_ref, yc_ref)):
        gate = _sigmoid(_dot(u, wgate_ref[:, n * D_MODEL:(n + 1) * D_MODEL]))
        merged = merged + gate * _dot(y_ref[...], wb_ref[0, n])
    hn = h + _dot(merged.astype(BF16), wo_ref[0])
    hn_ref[...] = hn
    u2 = _rmsnorm(hn, nf_ref[...])
    u2_ref[...] = u2.astype(BF16)

    u_hi, u_lo = _split_hi_lo(u2)
    lg_ref[...] = (_dot(u_hi, wr_hi_ref[...]) + _dot(u_hi, wr_lo_ref[...]) + _dot(u_lo, wr_hi_ref[...])
                   + br_ref[...])


def _route(logits):
    lane_i = lax.broadcasted_iota(jnp.int32, (R, LANES), 1)
    lane = lane_i.astype(F32)
    lane_grp = (lane_i // EXP_PER_GROUP).astype(F32)
    neg = jnp.float32(-jnp.inf)
    big = jnp.float32(1e9)
    is_g = (lane_i >= N_EXPERTS) & (lane_i < N_EXPERTS + N_GROUPS)
    lg = jnp.where(is_g, logits, neg)
    mg = jnp.max(lg, axis=-1, keepdims=True)
    zg = jnp.sum(jnp.exp(lg - mg), axis=-1, keepdims=True)
    g_val = 1.0 / zg
    g_idx = jnp.min(jnp.where(lg == mg, lane, big), axis=-1, keepdims=True) - N_EXPERTS
    in_grp = (lane_i < N_EXPERTS) & (lane_grp == g_idx)
    le = jnp.where(in_grp, logits, neg)
    m1 = jnp.max(le, axis=-1, keepdims=True)
    ze = jnp.sum(jnp.exp(le - m1), axis=-1, keepdims=True)
    i1 = jnp.min(jnp.where(le == m1, lane, big), axis=-1, keepdims=True)
    le2 = jnp.where(lane == i1, neg, le)
    m2 = jnp.max(le2, axis=-1, keepdims=True)
    i2 = jnp.min(jnp.where(le2 == m2, lane, big), axis=-1, keepdims=True)
    p1 = 1.0 / ze
    p2 = jnp.exp(m2 - m1) / ze
    den = p1 + p2
    cmb = g_val * jnp.where(lane == i1, p1 / den, jnp.where(lane == i2, p2 / den, 0.0))
    return jnp.where(lane_i == CMB_GROUP_LANE, g_idx, cmb)


def _merge(ya, yb, yc, u, hx, hm, w_gate, wb, wo, layer, nf, wr_hi, wr_lo, br):
    t = ya.shape[0]
    last = t // R - 1
    cur = lambda i: (jnp.minimum(i, last), 0)
    row = lambda w: pl.BlockSpec((R, w), cur)
    return pl.pallas_call(
        _merge_kernel,
        grid=(t // R + 1,),
        in_specs=[row(BRANCH_W), row(BRANCH_W), row(BRANCH_W), row(D_MODEL)]
        + _residual_specs(hx, hm, cur)
        + [_const_spec((D_MODEL, N_BRANCH * D_MODEL)),
                  pl.BlockSpec((1, N_BRANCH, BRANCH_W, D_MODEL), lambda i: (layer, 0, 0, 0)),
                  pl.BlockSpec((1, D_MODEL, D_MODEL), lambda i: (layer, 0, 0)),
                  _const_spec((1, D_MODEL)), _const_spec((D_MODEL, LANES)), _const_spec((D_MODEL, LANES)),
                  _const_spec((1, LANES))],
        out_specs=[row(D_MODEL), row(D_MODEL),
                   pl.BlockSpec((R, LANES), lambda i: (jnp.maximum(i - 1, 0), 0))],
        out_shape=[jax.ShapeDtypeStruct((t, D_MODEL), F32),
                   jax.ShapeDtypeStruct((t, D_MODEL), BF16),
                   jax.ShapeDtypeStruct((t, LANES), F32)],
        scratch_shapes=[pltpu.VMEM((R, LANES), F32)],
        compiler_params=_cparams(("arbitrary",)),
        name="merge_router",
    )(ya, yb, yc, u, hx, hm, w_gate, wb, wo, nf, wr_hi, wr_lo, br)


def _moe_kernel(final, cnt_ref, u_ref, c_ref, h_ref, wg_ref, wu_ref, wd_ref, tri_ref, nfin_ref, o_ref):
    w = pl.program_id(0)
    g = pl.program_id(1)
    u = u_ref[...]
    cmb = c_ref[...]
    gid_row = cmb.T[CMB_GROUP_LANE:CMB_GROUP_LANE + 1, :]
    in_row = jnp.where(gid_row == g.astype(F32), 1.0, 0.0)
    ranks = []
    before = jnp.zeros((1, 1), F32)
    for j in range(MOE_W // R):
        seg = in_row[:, j * R:(j + 1) * R]
        ranks.append(_dot(jnp.broadcast_to(seg, (16, R)).astype(BF16), tri_ref[...])[0:1, :] + before)
        before = before + jnp.sum(seg, axis=-1, keepdims=True)
    rank_row = jnp.concatenate(ranks, axis=-1)
    cmb_hi, cmb_lo = _split_hi_lo(cmb)

    @pl.when(g == 0)
    def _():
        o_ref[...] = h_ref[...]

    count = cnt_ref[w * N_GROUPS + g]
    n_pass = (count + MOE_CAPS[-1] - 1) // MOE_CAPS[-1]
    per_pass = (count + jnp.maximum(n_pass, 1) - 1) // jnp.maximum(n_pass, 1)

    def run(cap):
        slot = lax.broadcasted_iota(jnp.int32, (cap, 1), 0).astype(F32)
        lane = lax.broadcasted_iota(jnp.int32, (cap, LANES), 1)

        def one_pass(c, carry):
            base = (c * cap).astype(F32)
            sel = jnp.where((in_row > 0.0) & (rank_row - base == slot), 1.0, 0.0).astype(BF16)
            x = _dot(sel, u).astype(BF16)
            cw = _dot(sel, cmb_hi) + _dot(sel, cmb_lo)
            y = jnp.zeros((cap, D_MODEL), F32)
            for e in range(EXP_PER_GROUP):
                ce = jnp.sum(jnp.where(lane == g * EXP_PER_GROUP + e, cw, 0.0), axis=-1, keepdims=True)
                hid = _silu(_dot(x, wg_ref[0, 0, e])) * _dot(x, wu_ref[0, 0, e]) * ce
                y = y + _dot(hid.astype(BF16), wd_ref[0, 0, e])
            o_ref[...] += _dot_tn(sel, y.astype(BF16))
            return carry

        lax.fori_loop(0, n_pass, one_pass, 0)

    below = 0
    for cap in MOE_CAPS:
        pl.when((per_pass > below) & (per_pass <= cap))(functools.partial(run, cap))
        below = cap

    if final:
        @pl.when(g == N_GROUPS - 1)
        def _():
            o_ref[...] = _rmsnorm(o_ref[...], nfin_ref[...])


def _moe(u2, cmb, h, wg, wu, wd, layer, norm_final, final):
    t = h.shape[0]
    n_win = t // MOE_W
    assert t % MOE_W == 0
    gid = cmb[:, CMB_GROUP_LANE].astype(jnp.int32).reshape(n_win, MOE_W)
    counts = jnp.sum(gid[:, :, None] == jnp.arange(N_GROUPS, dtype=jnp.int32), axis=1, dtype=jnp.int32)
    tri = jnp.asarray(np.triu(np.ones((R, R), np.float32), 1), dtype=BF16)
    row = lambda width: pl.BlockSpec((MOE_W, width), lambda w, g, cnt: (w, 0))
    wspec = lambda a, b: pl.BlockSpec((1, 1, EXP_PER_GROUP, a, b), lambda w, g, cnt: (layer, g, 0, 0, 0))
    return pl.pallas_call(
        functools.partial(_moe_kernel, final),
        grid_spec=pltpu.PrefetchScalarGridSpec(
            num_scalar_prefetch=1,
            grid=(n_win, N_GROUPS),
            in_specs=[row(D_MODEL), row(LANES),
                      pl.BlockSpec((MOE_W, D_MODEL), lambda w, g, cnt: (w, 0), pipeline_mode=pl.Buffered(1)),
                      wspec(D_MODEL, EXPERT_HIDDEN), wspec(D_MODEL, EXPERT_HIDDEN),
                      wspec(EXPERT_HIDDEN, D_MODEL),
                      pl.BlockSpec((R, R), lambda w, g, cnt: (0, 0)),
                      pl.BlockSpec((1, D_MODEL), lambda w, g, cnt: (0, 0))],
            out_specs=row(D_MODEL)),
        out_shape=jax.ShapeDtypeStruct((t - R if final else t, D_MODEL), F32),
        compiler_params=_cparams(("arbitrary", "arbitrary")),
        name="moe",
    )(counts.reshape(-1), u2, cmb, h, wg, wu, wd, tri, norm_final)


IN_SPLITS = (512, 512, 512, 512, 1536, 4, 4, 512, 256, 256, 512, 16, 512, 3072)


def _inproj_weights(w_in):
    offs = np.cumsum((0,) + IN_SPLITS)
    (a_q, a_f, a_i, a_g, b_qkv, b_beta, b_decay, b_g, c_q, c_k, c_v, c_gk, c_g, gates) = [
        w_in[:, offs[j]:offs[j + 1]] for j in range(len(IN_SPLITS))]
    zeros = lambda n: jnp.zeros((D_MODEL, n), w_in.dtype)
    w_a = w_in[:, offs[0]:offs[4]]
    w_b = jnp.concatenate([b_qkv, b_g, b_beta, b_decay, zeros(LANES - SM_GK)], axis=-1)
    w_c = jnp.concatenate([c_q, c_k, c_v, c_g,
                           zeros(SM_GK), c_gk, zeros(LANES - SM_GK - C_GK_RANK)], axis=-1)
    return [w.astype(BF16) for w in (w_a, w_b, w_c, gates)]


def kernel(x, meta_tokens, norm_mix, w_in, hgrn_lb_logits, hgrn_norm, gdn_conv, gdn_a_log,
           gdn_dt_bias, gdn_norm, gla_gk_w2, gla_gk_b, gla_norm, w_branch, w_out, norm_ffn,
           router_group_w, router_group_b, router_expert_w, router_expert_b,
           expert_w_gate, expert_w_up, expert_w_down, norm_final):
    batch, seq, d = x.shape
    assert d == D_MODEL and seq % R == 0
    depth = w_in.shape[0]
    nb = seq // R
    lv_np, wl_np = _level_constants()
    lv = jnp.asarray(lv_np)
    wl = jnp.asarray(wl_np, dtype=BF16)

    n_blocks = batch * nb + 1
    hx = x.reshape(batch * seq, d)
    hm = jnp.concatenate([jnp.zeros((R - N_META, d), F32), meta_tokens.astype(F32)], axis=0)

    wg16, wu16, wd16, wb16, wo16 = (w.astype(BF16) for w in (
        expert_w_gate, expert_w_up, expert_w_down, w_branch, w_out))

    lb_p = jax.nn.softmax(hgrn_lb_logits.astype(F32), axis=0)
    lb_all = jnp.maximum(jnp.cumsum(lb_p, axis=0) - lb_p[0:1], 0.0)

    for layer in range(depth):
        w_a, w_b, w_c, w_gate = _inproj_weights(w_in[layer])
        nm = norm_mix[layer][None, :]

        lb = lb_all[layer]
        lb_rows = jnp.zeros((8, BRANCH_W), F32)
        lb_rows = lb_rows.at[0].set(jnp.maximum(lb, LB_FLOOR)).at[1].set(1.0 - lb)
        y_a, u = _hgrn(hx, hm, n_blocks, nm, w_a, lb_rows, hgrn_norm[layer][None, :], wl, lv, nb)

        head_params = jnp.zeros((8, LANES), F32)
        head_params = head_params.at[0, SM_DECAY:SM_DECAY + HEADS].set(-jnp.exp(gdn_a_log[layer]))
        head_params = head_params.at[1, SM_DECAY:SM_DECAY + HEADS].set(gdn_dt_bias[layer])
        y_b = _gdn(u, n_blocks, w_b, gdn_conv[layer], head_params, gdn_norm[layer][None, :], wl, lv, nb)

        w2 = jnp.zeros((LANES, HEADS * C_DK), F32).at[SM_GK:SM_GK + C_GK_RANK].set(gla_gk_w2[layer])
        y_c = _gla(u, n_blocks, w_c, w2.astype(BF16), gla_gk_b[layer][None, :],
                   gla_norm[layer][None, :], wl, lv, nb)

        wr = jnp.zeros((D_MODEL, LANES), F32)
        wr = wr.at[:, :N_EXPERTS].set(router_expert_w[layer])
        wr = wr.at[:, N_EXPERTS:N_EXPERTS + N_GROUPS].set(router_group_w[layer])
        br = jnp.zeros((1, LANES), F32)
        br = br.at[0, :N_EXPERTS].set(router_expert_b[layer])
        br = br.at[0, N_EXPERTS:N_EXPERTS + N_GROUPS].set(router_group_b[layer])
        wr_hi = wr.astype(BF16)
        wr_lo = (wr - wr_hi.astype(F32)).astype(BF16)
        hn, u2, cmb = _merge(y_a, y_b, y_c, u, hx, hm, w_gate, wb16, wo16, layer,
                             norm_ffn[layer][None, :], wr_hi, wr_lo, br)

        hx = hm = _moe(u2, cmb, hn, wg16, wu16, wd16, layer, norm_final[None, :], final=(layer == depth - 1))

    return hx.reshape(batch, seq, d)
```

```python
import functools

import numpy as np
import jax
import jax.numpy as jnp
from jax import lax
from jax.experimental import pallas as pl
from jax.experimental.pallas import tpu as pltpu

F32 = jnp.float32
BF16 = jnp.bfloat16

D_MODEL = 1024
N_META = 16
CONV_W = 4
RMS_EPS = 1e-6
L2_EPS = 1e-6
LB_FLOOR = 1e-30
HEADS = 4
HEAD_W = 128
BRANCH_W = HEADS * HEAD_W
C_DK = 64
C_GK_RANK = 16
C_GK_NORM = 16.0
N_BRANCH = 3
N_GROUPS = 4
EXP_PER_GROUP = 8
N_EXPERTS = N_GROUPS * EXP_PER_GROUP
EXPERT_HIDDEN = 256

R = 256
N_LEVELS = 8
LV_DIAG = N_LEVELS
W_CUM = N_LEVELS
W_SFX = N_LEVELS + 1
LANES = 128

SM_BETA, SM_DECAY, SM_GK = 0, 4, 8

VMEM_LIMIT = 60 * 1024 * 1024
MOE_W = 1280
MOE_CAPS = (320, 384, 448)
CMB_GROUP_LANE = N_EXPERTS


def _cparams(sem):
    return pltpu.CompilerParams(dimension_semantics=sem, vmem_limit_bytes=VMEM_LIMIT)


def _const_spec(shape):
    nd = len(shape)
    return pl.BlockSpec(shape, lambda *_: (0,) * nd)


@functools.lru_cache(maxsize=None)
def _level_constants():
    t = np.arange(R)[:, None]
    s = np.arange(R)[None, :]
    x = np.maximum(t ^ s, 1)
    lv = np.where(s < t, np.floor(np.log2(x)).astype(np.int32), np.where(s == t, LV_DIAG, -1)).astype(np.int32)
    w = np.zeros((N_LEVELS + 2, R, R), np.float32)
    for l in range(N_LEVELS):
        hsz = 1 << l
        for r in range(R):
            hb = (r // hsz) * hsz
            if (r >> l) & 1:
                w[l, r, hb:r + 1] = 1.0
            else:
                w[l, r, r + 1:hb + hsz] = 1.0
    w[W_CUM] = np.tril(np.ones((R, R), np.float32))
    w[W_SFX] = np.triu(np.ones((R, R), np.float32), 1)
    return lv, w


def _dot(a, b):
    return jnp.dot(a, b, preferred_element_type=F32)


def _dot_nt(a, b):
    return lax.dot_general(a, b, (((1,), (1,)), ((), ())), preferred_element_type=F32)


def _dot_tn(a, b):
    return lax.dot_general(a, b, (((0,), (0,)), ((), ())), preferred_element_type=F32)


def _sigmoid(x):
    return 1.0 / (1.0 + jnp.exp(-x))


def _silu(x):
    return x * _sigmoid(x)


def _softplus(x):
    return jnp.maximum(x, 0.0) + jnp.log1p(jnp.exp(-jnp.abs(x)))


def _log_sigmoid(x):
    return -_softplus(-x)


def _rmsnorm(x, w):
    return x * lax.rsqrt(jnp.mean(x * x, axis=-1, keepdims=True) + RMS_EPS) * w


def _valid_rows(is_meta):
    row = lax.broadcasted_iota(jnp.int32, (R, 1), 0)
    first_valid = jnp.where(is_meta, R - N_META, 0)
    return jnp.where(row >= first_valid, 1.0, 0.0).astype(F32)


def _residual_block(hx_ref, hm_ref, is_meta):
    return jnp.where(is_meta, hm_ref[...], hx_ref[...])


def _mixer_input(h, nw_ref, valid):
    return (_rmsnorm(h, nw_ref[...]) * valid).astype(BF16)


PROJ_CHUNK = 256


def _projection_steps(u, w_ref, out_ref):
    n_cols = w_ref.shape[1]

    def step(c0):
        sl = slice(c0, min(c0 + PROJ_CHUNK, n_cols))
        out_ref[:, sl] = _dot(u, w_ref[:, sl])

    return [functools.partial(step, c0) for c0 in range(0, n_cols, PROJ_CHUNK)]


def _run_share(steps, slots_left):
    for _ in range(-(-len(steps) // slots_left)):
        steps.pop(0)()


def _run_all(steps):
    while steps:
        steps.pop(0)()


def _seq_block(n_blocks, lag):
    return lambda i: ((jnp.clip(i - lag, 0, n_blocks - 1) + n_blocks - 1) % n_blocks, 0)


def _state_in(j, nb, st_ref, stm_ref):
    @pl.when(j <= 0)
    def _():
        st_ref[...] = jnp.zeros(st_ref.shape, st_ref.dtype)

    @pl.when((j >= 1) & ((j - 1) % nb == 0))
    def _():
        st_ref[...] = stm_ref[...]


def _state_out(j, st_ref, stm_ref):
    @pl.when(j == 0)
    def _():
        stm_ref[...] = st_ref[...]


def _skewed(i, pa_ref, pb_ref, body):
    @pl.when(i == 0)
    def _():
        pb_ref[...] = jnp.zeros(pb_ref.shape, pb_ref.dtype)

    @pl.when(i % 2 == 0)
    def _():
        body(pb_ref, pa_ref)

    @pl.when(i % 2 == 1)
    def _():
        body(pa_ref, pb_ref)


def _head_out(o, gate, nw):
    return (_rmsnorm(o, nw) * _silu(gate)).astype(BF16)


def _gla_block(q, k, v, g, gate, nw, wl_ref, lv, st_ref, p_ref, y_ref, fill):
    width = q.shape[1]
    per_tile = LANES * HEADS // width
    lane = lax.broadcasted_iota(jnp.int32, (1, LANES), 1)
    own_lanes = [jnp.where(lane // (LANES // per_tile) == j, 1.0, 0.0).astype(BF16) for j in range(per_tile)]

    def tile(a, h):
        t0 = (h // per_tile) * LANES
        return a[:, t0:t0 + LANES]

    def own(a_tile, h):
        return a_tile if per_tile == 1 else a_tile * own_lanes[h % per_tile]

    g16 = g.astype(BF16)
    b = _dot(wl_ref[W_CUM], g16)
    rowi = lax.broadcasted_iota(jnp.int32, (R, 1), 0)
    for lvl in range(N_LEVELS):
        hsz = 1 << lvl
        if hsz >= 8:
            b3 = b.reshape(R // (2 * hsz), 2 * hsz, width)
            b_m = jnp.broadcast_to(b3[:, hsz - 1:hsz, :], b3.shape).reshape(R, width)
            f = jnp.exp(-jnp.abs(b - b_m))
        else:
            f = jnp.exp(_dot(wl_ref[lvl], g16))
        lower = ((rowi >> lvl) & 1) == 1
        z = (jnp.where(lower, q, k) * f).astype(BF16)
        if 4 * hsz >= R:
            for h in range(HEADS):
                zt = tile(z, h)
                for r0 in range(0, R, 2 * hsz):
                    p_ref[h, r0 + hsz:r0 + 2 * hsz, r0:r0 + hsz] = _dot_nt(
                        own(zt[r0 + hsz:r0 + 2 * hsz], h), zt[r0:r0 + hsz])
        else:
            m = lv == lvl
            for h in range(HEADS):
                zt = tile(z, h)
                full = _dot_nt(own(zt, h), zt)
                if lvl == 0:
                    p_ref[h] = jnp.where(m, full, 0.0)
                else:
                    p_ref[h] = jnp.where(m, full, p_ref[h])
        _run_share(fill, N_LEVELS - lvl)
    q16 = q.astype(BF16)
    k16 = k.astype(BF16)
    m = lv == LV_DIAG
    for h in range(HEADS):
        p_ref[h] = jnp.where(m, _dot_nt(own(tile(q16, h), h), tile(k16, h)), p_ref[h])
    sfx = _dot(wl_ref[W_SFX], g16)
    qe = (q * jnp.exp(b)).astype(BF16)
    ke = (k * jnp.exp(sfx)).astype(BF16)
    dec = jnp.exp(b[R - 1:R, :])
    v16 = v.astype(BF16)
    for h in range(HEADS):
        sl = slice(h * HEAD_W, (h + 1) * HEAD_W)
        st = st_ref[h]
        o = _dot(p_ref[h].astype(BF16), v16[:, sl]) + _dot_nt(tile(qe, h), st.astype(BF16))
        st_ref[h] = st * tile(dec, h) + _dot_tn(v16[:, sl], own(tile(ke, h), h))
        y_ref[:, sl] = _head_out(o, gate[:, sl], nw)


def _residual_specs(hx, hm, block_of_step):
    hx_last = hx.shape[0] // R - 1
    hm_last = hm.shape[0] // R - 1
    return [pl.BlockSpec((R, D_MODEL), lambda i: (jnp.minimum(block_of_step(i)[0], hx_last), 0)),
            pl.BlockSpec((R, D_MODEL), lambda i: (hm_last, 0))]


def _mixer_call(kernel_fn, name, stream, stream_specs, n_blocks, operands, operand_specs, proj_cols, scratch,
                emit_u=False):
    y_spec = pl.BlockSpec((R, BRANCH_W), _seq_block(n_blocks, 1))
    y_shape = jax.ShapeDtypeStruct((n_blocks * R, BRANCH_W), BF16)
    u_spec = pl.BlockSpec((R, D_MODEL), _seq_block(n_blocks, 0))
    u_shape = jax.ShapeDtypeStruct((n_blocks * R, D_MODEL), BF16)
    return pl.pallas_call(
        kernel_fn,
        grid=(n_blocks + 1,),
        in_specs=stream_specs + operand_specs,
        out_specs=[y_spec, u_spec] if emit_u else y_spec,
        out_shape=[y_shape, u_shape] if emit_u else y_shape,
        scratch_shapes=scratch + [pltpu.VMEM((R, proj_cols), F32), pltpu.VMEM((R, proj_cols), F32)],
        compiler_params=_cparams(("arbitrary",)),
        name=name,
    )(*stream, *operands)


def _state_scratch():
    return [pltpu.VMEM((HEADS, HEAD_W, HEAD_W), F32), pltpu.VMEM((HEADS, HEAD_W, HEAD_W), F32)]


def _hgrn_kernel(nb, hx_ref, hm_ref, nm_ref, w_ref, lb_ref, nw_ref, wl_ref, lv_ref,
                 y_ref, u_ref, st_ref, stm_ref, p_ref, pa_ref, pb_ref):
    i = pl.program_id(0)
    j = i - 1
    _state_in(j, nb, st_ref, stm_ref)

    def body(rd, wr):
        a_q, f_in, a_i, a_g = [rd[:, c * BRANCH_W:(c + 1) * BRANCH_W] for c in range(4)]
        valid = _valid_rows(j == 0)
        lb_floor = lb_ref[0:1, :]
        one_m_lb = lb_ref[1:2, :]
        q = _silu(a_q) * (HEAD_W ** -0.5)
        e = jnp.exp(-jnp.abs(f_in))
        r = 1.0 / (1.0 + e)
        pos = f_in >= 0.0
        g = jnp.log(lb_floor + one_m_lb * jnp.where(pos, r, e * r))
        k = one_m_lb * jnp.where(pos, e * r, r) * valid
        u = _mixer_input(_residual_block(hx_ref, hm_ref, i == 0), nm_ref, _valid_rows(i == 0))
        u_ref[...] = u
        fill = _projection_steps(u, w_ref, wr)
        _gla_block(q, k, a_i, g, a_g, nw_ref[...], wl_ref, lv_ref[...], st_ref, p_ref, y_ref, fill)
        _run_all(fill)

    _skewed(i, pa_ref, pb_ref, body)
    _state_out(j, st_ref, stm_ref)


def _hgrn(hx, hm, n_blocks, norm_mix, w, lb_rows, norm_w, wl, lv, nb):
    return _mixer_call(
        functools.partial(_hgrn_kernel, nb), "hgrn2",
        [hx, hm], _residual_specs(hx, hm, _seq_block(n_blocks, 0)), n_blocks,
        [norm_mix, w, lb_rows, norm_w, wl, lv],
        [_const_spec((1, D_MODEL)), _const_spec(w.shape), _const_spec((8, BRANCH_W)),
         _const_spec((1, HEAD_W)), _const_spec(wl.shape), _const_spec((R, R))],
        w.shape[1], _state_scratch() + [pltpu.VMEM((HEADS, R, R), F32)], emit_u=True)


def _gla_kernel(nb, u_ref, w_ref, w2_ref, b2_ref, nw_ref, wl_ref, lv_ref,
                y_ref, st_ref, stm_ref, p_ref, pa_ref, pb_ref):
    i = pl.program_id(0)
    j = i - 1
    _state_in(j, nb, st_ref, stm_ref)

    def body(rd, wr):
        qk_w = HEADS * C_DK
        c_q = rd[:, :qk_w]
        c_k = rd[:, qk_w:2 * qk_w]
        c_v = rd[:, 2 * qk_w:2 * qk_w + BRANCH_W]
        c_g = rd[:, 2 * qk_w + BRANCH_W:2 * qk_w + 2 * BRANCH_W]
        gk_low = rd[:, 2 * qk_w + 2 * BRANCH_W:]
        z = _dot(gk_low.astype(BF16), w2_ref[...]) + b2_ref[...]
        g = _log_sigmoid(z) * (1.0 / C_GK_NORM)
        fill = _projection_steps(u_ref[...], w_ref, wr)
        _gla_block(c_q * (C_DK ** -0.5), c_k * _valid_rows(j == 0), c_v, g, c_g, nw_ref[...],
                   wl_ref, lv_ref[...], st_ref, p_ref, y_ref, fill)
        _run_all(fill)

    _skewed(i, pa_ref, pb_ref, body)
    _state_out(j, st_ref, stm_ref)


def _u_spec(n_blocks):
    return [pl.BlockSpec((R, D_MODEL), _seq_block(n_blocks, 0))]


def _gla(u, n_blocks, w, w2, b2, norm_w, wl, lv, nb):
    return _mixer_call(
        functools.partial(_gla_kernel, nb), "gla", [u], _u_spec(n_blocks), n_blocks,
        [w, w2, b2, norm_w, wl, lv],
        [_const_spec(w.shape), _const_spec(w2.shape),
         _const_spec(b2.shape), _const_spec((1, HEAD_W)), _const_spec(wl.shape), _const_spec((R, R))],
        w.shape[1], _state_scratch() + [pltpu.VMEM((HEADS, R, R), F32)])


TAIL = 8


def _gdn_kernel(nb, u_ref, w_ref, cw_ref, hp_ref, nw_ref, wl_ref, lv_ref,
                y_ref, st_ref, stm_ref, xx_ref, tailm_ref, x_ref, a_ref, pa_ref, pb_ref):
    i = pl.program_id(0)
    j = i - 1
    _state_in(j, nb, st_ref, stm_ref)

    @pl.when(j <= 0)
    def _():
        xx_ref[0:TAIL, :] = jnp.zeros((TAIL, 3 * BRANCH_W), F32)

    @pl.when((j >= 1) & ((j - 1) % nb == 0))
    def _():
        xx_ref[0:TAIL, :] = tailm_ref[...]

    def body(rd, wr):
        _gdn_body(j, rd, wr, u_ref, w_ref, cw_ref, hp_ref, nw_ref, wl_ref, lv_ref,
                  y_ref, st_ref, xx_ref, x_ref, a_ref)

    _skewed(i, pa_ref, pb_ref, body)

    @pl.when(j == 0)
    def _():
        tailm_ref[...] = xx_ref[0:TAIL, :]

    _state_out(j, st_ref, stm_ref)


def _gdn_body(j, rd, wr, u_ref, w_ref, cw_ref, hp_ref, nw_ref, wl_ref, lv_ref,
              y_ref, st_ref, xx_ref, x_ref, a_ref):
    fill = _projection_steps(u_ref[...], w_ref, wr)
    valid = _valid_rows(j == 0)
    lv = lv_ref[...]
    xx_ref[TAIL:, :] = rd[:, :3 * BRANCH_W]
    b_g = rd[:, 3 * BRANCH_W:4 * BRANCH_W]
    sm = rd[:, 4 * BRANCH_W:]
    conv = jnp.zeros((R, 3 * BRANCH_W), F32)
    for tap in range(CONV_W):
        off = TAIL - (CONV_W - 1) + tap
        conv = conv + xx_ref[off:off + R, :] * cw_ref[tap:tap + 1, :]
    qkv = _silu(conv)
    xx_ref[0:TAIL, :] = xx_ref[R:R + TAIL, :]

    a_neg = hp_ref[0:1, :]
    dt_b = hp_ref[1:2, :]
    log_a = a_neg * _softplus(sm + dt_b)
    la_hi = log_a.astype(BF16)
    la_lo = (log_a - la_hi.astype(F32)).astype(BF16)
    gcum = _dot(wl_ref[W_CUM], la_hi) + _dot(wl_ref[W_CUM], la_lo)
    gcum_t = gcum.T
    beta_all = _sigmoid(sm) * valid

    strict = (lv >= 0) & (lv < LV_DIAG)
    causal = lv >= 0
    eye = (lv == LV_DIAG).astype(F32)

    heads = []
    for h in range(HEADS):
        qh = qkv[:, h * HEAD_W:(h + 1) * HEAD_W]
        kh = qkv[:, BRANCH_W + h * HEAD_W:BRANCH_W + (h + 1) * HEAD_W] * valid
        vh = qkv[:, 2 * BRANCH_W + h * HEAD_W:2 * BRANCH_W + (h + 1) * HEAD_W]
        qh = qh * lax.rsqrt(jnp.sum(qh * qh, axis=-1, keepdims=True) + L2_EPS) * (HEAD_W ** -0.5)
        kh = kh * lax.rsqrt(jnp.sum(kh * kh, axis=-1, keepdims=True) + L2_EPS)
        beta = beta_all[:, SM_BETA + h:SM_BETA + h + 1]
        gcol = gcum[:, SM_DECAY + h:SM_DECAY + h + 1]
        grow = gcum_t[SM_DECAY + h:SM_DECAY + h + 1, :]
        dm = jnp.exp(jnp.minimum(gcol - grow, 0.0))
        q16 = qh.astype(BF16)
        k16 = kh.astype(BF16)
        a_ref[h] = jnp.where(strict, beta * _dot_nt(k16, k16) * dm, 0.0)
        qk = jnp.where(causal, _dot_nt(q16, k16) * dm, 0.0).astype(BF16)
        heads.append((qh, kh, vh, beta, gcol, qk))
        _run_share(fill, HEADS + 1 - h)

    for lvl in range(N_LEVELS):
        m = lv == lvl
        for h in range(HEADS):
            if 2 << lvl == R:
                hs = R // 2
                x_lo = x_ref[h, hs:, hs:].astype(BF16)
                x_up = x_ref[h, :hs, :hs].astype(BF16)
                x_ref[h, hs:, :hs] = -_dot(_dot(x_lo, a_ref[h, hs:, :hs].astype(BF16)).astype(BF16), x_up)
                continue
            l_lvl = jnp.where(m, a_ref[h], 0.0)
            if lvl == 0:
                x_ref[h] = eye - l_lvl
            else:
                xcur = x_ref[h].astype(BF16)
                x_ref[h] = x_ref[h] - _dot(_dot(xcur, l_lvl.astype(BF16)).astype(BF16), xcur)

    for h, (qh, kh, vh, beta, gcol, qk) in enumerate(heads):
        sl = slice(h * HEAD_W, (h + 1) * HEAD_W)
        eg = jnp.exp(gcol)
        rhs = jnp.concatenate([vh * beta, kh * (beta * eg)], axis=-1).astype(BF16)
        sol = _dot(x_ref[h].astype(BF16), rhs)
        u_h = sol[:, :HEAD_W]
        w_h = sol[:, HEAD_W:]
        st = st_ref[h]
        st16 = st.astype(BF16)
        v_new = u_h - _dot_nt(w_h.astype(BF16), st16)
        vn16 = v_new.astype(BF16)
        o = _dot_nt((qh * eg).astype(BF16), st16) + _dot(qk, vn16)
        g_last = gcol[R - 1:R, :]
        ke = (kh * jnp.exp(g_last - gcol)).astype(BF16)
        st_ref[h] = st * jnp.exp(g_last) + _dot_tn(vn16, ke)
        y_ref[:, sl] = _head_out(o, b_g[:, sl], nw_ref[...])
    _run_all(fill)


def _gdn(u, n_blocks, w, conv_w, head_params, norm_w, wl, lv, nb):
    return _mixer_call(
        functools.partial(_gdn_kernel, nb), "gdn", [u], _u_spec(n_blocks), n_blocks,
        [w, conv_w, head_params, norm_w, wl, lv],
        [_const_spec(w.shape), _const_spec((CONV_W, 3 * BRANCH_W)),
         _const_spec((8, LANES)), _const_spec((1, HEAD_W)), _const_spec(wl.shape), _const_spec((R, R))],
        w.shape[1], _state_scratch() + [pltpu.VMEM((R + TAIL, 3 * BRANCH_W), F32),
                            pltpu.VMEM((TAIL, 3 * BRANCH_W), F32),
                            pltpu.VMEM((HEADS, R, R), F32),
                            pltpu.VMEM((HEADS, R, R), F32)])


def _split_hi_lo(x):
    hi = x.astype(BF16)
    return hi, (x - hi.astype(F32)).astype(BF16)


def _merge_kernel(ya_ref, yb_ref, yc_ref, u_ref, hx_ref, hm_ref, wgate_ref, wb_ref, wo_ref, nf_ref,
                  wr_hi_ref, wr_lo_ref, br_ref, hn_ref, u2_ref, cmb_ref, lg_ref):
    i = pl.program_id(0)
    last = pl.num_programs(0) - 2

    @pl.when(i == 0)
    def _():
        lg_ref[...] = jnp.zeros(lg_ref.shape, lg_ref.dtype)

    cmb_ref[...] = _route(lg_ref[...])

    is_meta = i >= last
    h = _residual_block(hx_ref, hm_ref, is_meta)
    u = u_ref[...]
    merged = jnp.zeros((R, D_MODEL), F32)
    for n, y_ref in enumerate((ya_ref, yb_ref, yc_ref)):
        gate = _sigmoid(_dot(u, wgate_ref[:, n * D_MODEL:(n + 1) * D_MODEL]))
        merged = merged + gate * _dot(y_ref[...], wb_ref[0, n])
    hn = h + _dot(merged.astype(BF16), wo_ref[0])
    hn_ref[...] = hn
    u2 = _rmsnorm(hn, nf_ref[...])
    u2_ref[...] = u2.astype(BF16)

    u_hi, u_lo = _split_hi_lo(u2)
    lg_ref[...] = (_dot(u_hi, wr_hi_ref[...]) + _dot(u_hi, wr_lo_ref[...]) + _dot(u_lo, wr_hi_ref[...])
                   + br_ref[...])


def _route(logits):
    lane_i = lax.broadcasted_iota(jnp.int32, (R, LANES), 1)
    lane = lane_i.astype(F32)
    lane_grp = (lane_i // EXP_PER_GROUP).astype(F32)
    neg = jnp.float32(-jnp.inf)
    big = jnp.float32(1e9)
    is_g = (lane_i >= N_EXPERTS) & (lane_i < N_EXPERTS + N_GROUPS)
    lg = jnp.where(is_g, logits, neg)
    mg = jnp.max(lg, axis=-1, keepdims=True)
    zg = jnp.sum(jnp.exp(lg - mg), axis=-1, keepdims=True)
    g_val = 1.0 / zg
    g_idx = jnp.min(jnp.where(lg == mg, lane, big), axis=-1, keepdims=True) - N_EXPERTS
    in_grp = (lane_i < N_EXPERTS) & (lane_grp == g_idx)
    le = jnp.where(in_grp, logits, neg)
    m1 = jnp.max(le, axis=-1, keepdims=True)
    ze = jnp.sum(jnp.exp(le - m1), axis=-1, keepdims=True)
    i1 = jnp.min(jnp.where(le == m1, lane, big), axis=-1, keepdims=True)
    le2 = jnp.where(lane == i1, neg, le)
    m2 = jnp.max(le2, axis=-1, keepdims=True)
    i2 = jnp.min(jnp.where(le2 == m2, lane, big), axis=-1, keepdims=True)
    p1 = 1.0 / ze
    p2 = jnp.exp(m2 - m1) / ze
    den = p1 + p2
    cmb = g_val * jnp.where(lane == i1, p1 / den, jnp.where(lane == i2, p2 / den, 0.0))
    return jnp.where(lane_i == CMB_GROUP_LANE, g_idx, cmb)


def _merge(ya, yb, yc, u, hx, hm, w_gate, wb, wo, layer, nf, wr_hi, wr_lo, br):
    t = ya.shape[0]
    last = t // R - 1
    cur = lambda i: (jnp.minimum(i, last), 0)
    row = lambda w: pl.BlockSpec((R, w), cur)
    return pl.pallas_call(
        _merge_kernel,
        grid=(t // R + 1,),
        in_specs=[row(BRANCH_W), row(BRANCH_W), row(BRANCH_W), row(D_MODEL)]
        + _residual_specs(hx, hm, cur)
        + [_const_spec((D_MODEL, N_BRANCH * D_MODEL)),
                  pl.BlockSpec((1, N_BRANCH, BRANCH_W, D_MODEL), lambda i: (layer, 0, 0, 0)),
                  pl.BlockSpec((1, D_MODEL, D_MODEL), lambda i: (layer, 0, 0)),
                  _const_spec((1, D_MODEL)), _const_spec((D_MODEL, LANES)), _const_spec((D_MODEL, LANES)),
                  _const_spec((1, LANES))],
        out_specs=[row(D_MODEL), row(D_MODEL),
                   pl.BlockSpec((R, LANES), lambda i: (jnp.maximum(i - 1, 0), 0))],
        out_shape=[jax.ShapeDtypeStruct((t, D_MODEL), F32),
                   jax.ShapeDtypeStruct((t, D_MODEL), BF16),
                   jax.ShapeDtypeStruct((t, LANES), F32)],
        scratch_shapes=[pltpu.VMEM((R, LANES), F32)],
        compiler_params=_cparams(("arbitrary",)),
        name="merge_router",
    )(ya, yb, yc, u, hx, hm, w_gate, wb, wo, nf, wr_hi, wr_lo, br)


def _moe_kernel(final, cnt_ref, u_ref, c_ref, h_ref, wg_ref, wu_ref, wd_ref, tri_ref, nfin_ref, o_ref):
    w = pl.program_id(0)
    g = pl.program_id(1)
    u = u_ref[...]
    cmb = c_ref[...]
    gid_row = cmb.T[CMB_GROUP_LANE:CMB_GROUP_LANE + 1, :]
    in_row = jnp.where(gid_row == g.astype(F32), 1.0, 0.0)
    ranks = []
    before = jnp.zeros((1, 1), F32)
    for j in range(MOE_W // R):
        seg = in_row[:, j * R:(j + 1) * R]
        ranks.append(_dot(jnp.broadcast_to(seg, (16, R)).astype(BF16), tri_ref[...])[0:1, :] + before)
        before = before + jnp.sum(seg, axis=-1, keepdims=True)
    rank_row = jnp.concatenate(ranks, axis=-1)
    cmb_hi, cmb_lo = _split_hi_lo(cmb)

    @pl.when(g == 0)
    def _():
        o_ref[...] = h_ref[...]

    count = cnt_ref[w * N_GROUPS + g]
    n_pass = (count + MOE_CAPS[-1] - 1) // MOE_CAPS[-1]
    per_pass = (count + jnp.maximum(n_pass, 1) - 1) // jnp.maximum(n_pass, 1)

    def run(cap):
        slot = lax.broadcasted_iota(jnp.int32, (cap, 1), 0).astype(F32)
        lane = lax.broadcasted_iota(jnp.int32, (cap, LANES), 1)

        def one_pass(c, carry):
            base = (c * cap).astype(F32)
            sel = jnp.where((in_row > 0.0) & (rank_row - base == slot), 1.0, 0.0).astype(BF16)
            x = _dot(sel, u).astype(BF16)
            cw = _dot(sel, cmb_hi) + _dot(sel, cmb_lo)
            y = jnp.zeros((cap, D_MODEL), F32)
            for e in range(EXP_PER_GROUP):
                ce = jnp.sum(jnp.where(lane == g * EXP_PER_GROUP + e, cw, 0.0), axis=-1, keepdims=True)
                hid = _silu(_dot(x, wg_ref[0, 0, e])) * _dot(x, wu_ref[0, 0, e]) * ce
                y = y + _dot(hid.astype(BF16), wd_ref[0, 0, e])
            o_ref[...] += _dot_tn(sel, y.astype(BF16))
            return carry

        lax.fori_loop(0, n_pass, one_pass, 0)

    below = 0
    for cap in MOE_CAPS:
        pl.when((per_pass > below) & (per_pass <= cap))(functools.partial(run, cap))
        below = cap

    if final:
        @pl.when(g == N_GROUPS - 1)
        def _():
            o_ref[...] = _rmsnorm(o_ref[...], nfin_ref[...])


def _moe(u2, cmb, h, wg, wu, wd, layer, norm_final, final):
    t = h.shape[0]
    n_win = t // MOE_W
    assert t % MOE_W == 0
    gid = cmb[:, CMB_GROUP_LANE].astype(jnp.int32).reshape(n_win, MOE_W)
    counts = jnp.sum(gid[:, :, None] == jnp.arange(N_GROUPS, dtype=jnp.int32), axis=1, dtype=jnp.int32)
    tri = jnp.asarray(np.triu(np.ones((R, R), np.float32), 1), dtype=BF16)
    row = lambda width: pl.BlockSpec((MOE_W, width), lambda w, g, cnt: (w, 0))
    wspec = lambda a, b: pl.BlockSpec((1, 1, EXP_PER_GROUP, a, b), lambda w, g, cnt: (layer, g, 0, 0, 0))
    return pl.pallas_call(
        functools.partial(_moe_kernel, final),
        grid_spec=pltpu.PrefetchScalarGridSpec(
            num_scalar_prefetch=1,
            grid=(n_win, N_GROUPS),
            in_specs=[row(D_MODEL), row(LANES),
                      row(D_MODEL),
                      wspec(D_MODEL, EXPERT_HIDDEN), wspec(D_MODEL, EXPERT_HIDDEN),
                      wspec(EXPERT_HIDDEN, D_MODEL),
                      pl.BlockSpec((R, R), lambda w, g, cnt: (0, 0)),
                      pl.BlockSpec((1, D_MODEL), lambda w, g, cnt: (0, 0))],
            out_specs=row(D_MODEL)),
        out_shape=jax.ShapeDtypeStruct((t - R if final else t, D_MODEL), F32),
        compiler_params=_cparams(("arbitrary", "arbitrary")),
        name="moe",
    )(counts.reshape(-1), u2, cmb, h, wg, wu, wd, tri, norm_final)


IN_SPLITS = (512, 512, 512, 512, 1536, 4, 4, 512, 256, 256, 512, 16, 512, 3072)


def _inproj_weights(w_in):
    offs = np.cumsum((0,) + IN_SPLITS)
    (a_q, a_f, a_i, a_g, b_qkv, b_beta, b_decay, b_g, c_q, c_k, c_v, c_gk, c_g, gates) = [
        w_in[:, offs[j]:offs[j + 1]] for j in range(len(IN_SPLITS))]
    zeros = lambda n: jnp.zeros((D_MODEL, n), w_in.dtype)
    w_a = w_in[:, offs[0]:offs[4]]
    w_b = jnp.concatenate([b_qkv, b_g, b_beta, b_decay, zeros(LANES - SM_GK)], axis=-1)
    w_c = jnp.concatenate([c_q, c_k, c_v, c_g,
                           zeros(SM_GK), c_gk, zeros(LANES - SM_GK - C_GK_RANK)], axis=-1)
    return [w.astype(BF16) for w in (w_a, w_b, w_c, gates)]


def kernel(x, meta_tokens, norm_mix, w_in, hgrn_lb_logits, hgrn_norm, gdn_conv, gdn_a_log,
           gdn_dt_bias, gdn_norm, gla_gk_w2, gla_gk_b, gla_norm, w_branch, w_out, norm_ffn,
           router_group_w, router_group_b, router_expert_w, router_expert_b,
           expert_w_gate, expert_w_up, expert_w_down, norm_final):
    batch, seq, d = x.shape
    assert d == D_MODEL and seq % R == 0
    depth = w_in.shape[0]
    nb = seq // R
    lv_np, wl_np = _level_constants()
    lv = jnp.asarray(lv_np)
    wl = jnp.asarray(wl_np, dtype=BF16)

    n_blocks = batch * nb + 1
    hx = x.reshape(batch * seq, d)
    hm = jnp.concatenate([jnp.zeros((R - N_META, d), F32), meta_tokens.astype(F32)], axis=0)

    wg16, wu16, wd16, wb16, wo16 = (w.astype(BF16) for w in (
        expert_w_gate, expert_w_up, expert_w_down, w_branch, w_out))

    lb_p = jax.nn.softmax(hgrn_lb_logits.astype(F32), axis=0)
    lb_all = jnp.maximum(jnp.cumsum(lb_p, axis=0) - lb_p[0:1], 0.0)

    for layer in range(depth):
        w_a, w_b, w_c, w_gate = _inproj_weights(w_in[layer])
        nm = norm_mix[layer][None, :]

        lb = lb_all[layer]
        lb_rows = jnp.zeros((8, BRANCH_W), F32)
        lb_rows = lb_rows.at[0].set(jnp.maximum(lb, LB_FLOOR)).at[1].set(1.0 - lb)
        y_a, u = _hgrn(hx, hm, n_blocks, nm, w_a, lb_rows, hgrn_norm[layer][None, :], wl, lv, nb)

        head_params = jnp.zeros((8, LANES), F32)
        head_params = head_params.at[0, SM_DECAY:SM_DECAY + HEADS].set(-jnp.exp(gdn_a_log[layer]))
        head_params = head_params.at[1, SM_DECAY:SM_DECAY + HEADS].set(gdn_dt_bias[layer])
        y_b = _gdn(u, n_blocks, w_b, gdn_conv[layer], head_params, gdn_norm[layer][None, :], wl, lv, nb)

        w2 = jnp.zeros((LANES, HEADS * C_DK), F32).at[SM_GK:SM_GK + C_GK_RANK].set(gla_gk_w2[layer])
        y_c = _gla(u, n_blocks, w_c, w2.astype(BF16), gla_gk_b[layer][None, :],
                   gla_norm[layer][None, :], wl, lv, nb)

        wr = jnp.zeros((D_MODEL, LANES), F32)
        wr = wr.at[:, :N_EXPERTS].set(router_expert_w[layer])
        wr = wr.at[:, N_EXPERTS:N_EXPERTS + N_GROUPS].set(router_group_w[layer])
        br = jnp.zeros((1, LANES), F32)
        br = br.at[0, :N_EXPERTS].set(router_expert_b[layer])
        br = br.at[0, N_EXPERTS:N_EXPERTS + N_GROUPS].set(router_group_b[layer])
        wr_hi = wr.astype(BF16)
        wr_lo = (wr - wr_hi.astype(F32)).astype(BF16)
        hn, u2, cmb = _merge(y_a, y_b, y_c, u, hx, hm, w_gate, wb16, wo16, layer,
                             norm_ffn[layer][None, :], wr_hi, wr_lo, br)

        hx = hm = _moe(u2, cmb, hn, wg16, wu16, wd16, layer, norm_final[None, :], final=(layer == depth - 1))

    return hx.reshape(batch, seq, d)
```

```python
import functools

import numpy as np
import jax
import jax.numpy as jnp
from jax import lax
from jax.experimental import pallas as pl
from jax.experimental.pallas import tpu as pltpu

F32 = jnp.float32
BF16 = jnp.bfloat16

D_MODEL = 1024
N_META = 16
CONV_W = 4
RMS_EPS = 1e-6
L2_EPS = 1e-6
LB_FLOOR = 1e-30
HEADS = 4
HEAD_W = 128
BRANCH_W = HEADS * HEAD_W
C_DK = 64
C_GK_RANK = 16
C_GK_NORM = 16.0
N_BRANCH = 3
N_GROUPS = 4
EXP_PER_GROUP = 8
N_EXPERTS = N_GROUPS * EXP_PER_GROUP
EXPERT_HIDDEN = 256

R = 256
N_LEVELS = 8
LV_DIAG = N_LEVELS
W_CUM = N_LEVELS
W_SFX = N_LEVELS + 1
LANES = 128

SM_BETA, SM_DECAY, SM_GK = 0, 4, 8

VMEM_LIMIT = 60 * 1024 * 1024
MOE_W = 1280
MOE_CAPS = (320, 384, 448)
CMB_GROUP_LANE = N_EXPERTS


def _cparams(sem):
    return pltpu.CompilerParams(dimension_semantics=sem, vmem_limit_bytes=VMEM_LIMIT)


def _const_spec(shape):
    nd = len(shape)
    return pl.BlockSpec(shape, lambda *_: (0,) * nd)


@functools.lru_cache(maxsize=None)
def _level_constants():
    t = np.arange(R)[:, None]
    s = np.arange(R)[None, :]
    x = np.maximum(t ^ s, 1)
    lv = np.where(s < t, np.floor(np.log2(x)).astype(np.int32), np.where(s == t, LV_DIAG, -1)).astype(np.int32)
    w = np.zeros((N_LEVELS + 2, R, R), np.float32)
    for l in range(N_LEVELS):
        hsz = 1 << l
        for r in range(R):
            hb = (r // hsz) * hsz
            if (r >> l) & 1:
                w[l, r, hb:r + 1] = 1.0
            else:
                w[l, r, r + 1:hb + hsz] = 1.0
    w[W_CUM] = np.tril(np.ones((R, R), np.float32))
    w[W_SFX] = np.triu(np.ones((R, R), np.float32), 1)
    return lv, w


def _dot(a, b):
    return jnp.dot(a, b, preferred_element_type=F32)


def _dot_nt(a, b):
    return lax.dot_general(a, b, (((1,), (1,)), ((), ())), preferred_element_type=F32)


def _dot_tn(a, b):
    return lax.dot_general(a, b, (((0,), (0,)), ((), ())), preferred_element_type=F32)


def _sigmoid(x):
    return 1.0 / (1.0 + jnp.exp(-x))


def _silu(x):
    return x * _sigmoid(x)


def _softplus(x):
    return jnp.maximum(x, 0.0) + jnp.log1p(jnp.exp(-jnp.abs(x)))


def _log_sigmoid(x):
    return -_softplus(-x)


def _rmsnorm(x, w):
    return x * lax.rsqrt(jnp.mean(x * x, axis=-1, keepdims=True) + RMS_EPS) * w


def _valid_rows(is_meta):
    row = lax.broadcasted_iota(jnp.int32, (R, 1), 0)
    first_valid = jnp.where(is_meta, R - N_META, 0)
    return jnp.where(row >= first_valid, 1.0, 0.0).astype(F32)


def _residual_block(hx_ref, hm_ref, is_meta):
    return jnp.where(is_meta, hm_ref[...], hx_ref[...])


def _mixer_input(h, nw_ref, valid):
    return (_rmsnorm(h, nw_ref[...]) * valid).astype(BF16)


PROJ_CHUNK = 256


def _projection_steps(u, w_ref, out_ref):
    n_cols = w_ref.shape[1]

    def step(c0):
        sl = slice(c0, min(c0 + PROJ_CHUNK, n_cols))
        out_ref[:, sl] = _dot(u, w_ref[:, sl])

    return [functools.partial(step, c0) for c0 in range(0, n_cols, PROJ_CHUNK)]


def _run_share(steps, slots_left):
    for _ in range(-(-len(steps) // slots_left)):
        steps.pop(0)()


def _run_all(steps):
    while steps:
        steps.pop(0)()


def _seq_block(n_blocks, lag):
    return lambda i: ((jnp.clip(i - lag, 0, n_blocks - 1) + n_blocks - 1) % n_blocks, 0)


def _state_in(j, nb, st_ref, stm_ref):
    @pl.when(j <= 0)
    def _():
        st_ref[...] = jnp.zeros(st_ref.shape, st_ref.dtype)

    @pl.when((j >= 1) & ((j - 1) % nb == 0))
    def _():
        st_ref[...] = stm_ref[...]


def _state_out(j, st_ref, stm_ref):
    @pl.when(j == 0)
    def _():
        stm_ref[...] = st_ref[...]


def _skewed(i, pa_ref, pb_ref, body):
    @pl.when(i == 0)
    def _():
        pb_ref[...] = jnp.zeros(pb_ref.shape, pb_ref.dtype)

    @pl.when(i % 2 == 0)
    def _():
        body(pb_ref, pa_ref)

    @pl.when(i % 2 == 1)
    def _():
        body(pa_ref, pb_ref)


def _head_out(o, gate, nw):
    return (_rmsnorm(o, nw) * _silu(gate)).astype(BF16)


def _gla_block(q, k, v, g, gate, nw, wl_ref, lv, st_ref, p_ref, y_ref, fill):
    width = q.shape[1]
    per_tile = LANES * HEADS // width
    lane = lax.broadcasted_iota(jnp.int32, (1, LANES), 1)
    own_lanes = [jnp.where(lane // (LANES // per_tile) == j, 1.0, 0.0).astype(BF16) for j in range(per_tile)]

    def tile(a, h):
        t0 = (h // per_tile) * LANES
        return a[:, t0:t0 + LANES]

    def own(a_tile, h):
        return a_tile if per_tile == 1 else a_tile * own_lanes[h % per_tile]

    g16 = g.astype(BF16)
    b = _dot(wl_ref[W_CUM], g16)
    rowi = lax.broadcasted_iota(jnp.int32, (R, 1), 0)
    for lvl in range(N_LEVELS):
        hsz = 1 << lvl
        if hsz >= 8:
            b3 = b.reshape(R // (2 * hsz), 2 * hsz, width)
            b_m = jnp.broadcast_to(b3[:, hsz - 1:hsz, :], b3.shape).reshape(R, width)
            f = jnp.exp(-jnp.abs(b - b_m))
        else:
            f = jnp.exp(_dot(wl_ref[lvl], g16))
        lower = ((rowi >> lvl) & 1) == 1
        z = (jnp.where(lower, q, k) * f).astype(BF16)
        if 4 * hsz >= R:
            for h in range(HEADS):
                zt = tile(z, h)
                for r0 in range(0, R, 2 * hsz):
                    p_ref[h, r0 + hsz:r0 + 2 * hsz, r0:r0 + hsz] = _dot_nt(
                        own(zt[r0 + hsz:r0 + 2 * hsz], h), zt[r0:r0 + hsz])
        else:
            m = lv == lvl
            for h in range(HEADS):
                zt = tile(z, h)
                full = _dot_nt(own(zt, h), zt)
                if lvl == 0:
                    p_ref[h] = jnp.where(m, full, 0.0)
                else:
                    p_ref[h] = jnp.where(m, full, p_ref[h])
        _run_share(fill, N_LEVELS - lvl)
    q16 = q.astype(BF16)
    k16 = k.astype(BF16)
    m = lv == LV_DIAG
    for h in range(HEADS):
        p_ref[h] = jnp.where(m, _dot_nt(own(tile(q16, h), h), tile(k16, h)), p_ref[h])
    sfx = _dot(wl_ref[W_SFX], g16)
    qe = (q * jnp.exp(b)).astype(BF16)
    ke = (k * jnp.exp(sfx)).astype(BF16)
    dec = jnp.exp(b[R - 1:R, :])
    v16 = v.astype(BF16)
    for h in range(HEADS):
        sl = slice(h * HEAD_W, (h + 1) * HEAD_W)
        st = st_ref[h]
        o = _dot(p_ref[h].astype(BF16), v16[:, sl]) + _dot_nt(tile(qe, h), st.astype(BF16))
        st_ref[h] = st * tile(dec, h) + _dot_tn(v16[:, sl], own(tile(ke, h), h))
        y_ref[:, sl] = _head_out(o, gate[:, sl], nw)


def _residual_specs(hx, hm, block_of_step):
    hx_last = hx.shape[0] // R - 1
    hm_last = hm.shape[0] // R - 1
    return [pl.BlockSpec((R, D_MODEL), lambda i: (jnp.minimum(block_of_step(i)[0], hx_last), 0)),
            pl.BlockSpec((R, D_MODEL), lambda i: (hm_last, 0))]


def _mixer_call(kernel_fn, name, stream, stream_specs, n_blocks, operands, operand_specs, proj_cols, scratch,
                emit_u=False):
    y_spec = pl.BlockSpec((R, BRANCH_W), _seq_block(n_blocks, 1))
    y_shape = jax.ShapeDtypeStruct((n_blocks * R, BRANCH_W), BF16)
    u_spec = pl.BlockSpec((R, D_MODEL), _seq_block(n_blocks, 0))
    u_shape = jax.ShapeDtypeStruct((n_blocks * R, D_MODEL), BF16)
    return pl.pallas_call(
        kernel_fn,
        grid=(n_blocks + 1,),
        in_specs=stream_specs + operand_specs,
        out_specs=[y_spec, u_spec] if emit_u else y_spec,
        out_shape=[y_shape, u_shape] if emit_u else y_shape,
        scratch_shapes=scratch + [pltpu.VMEM((R, proj_cols), F32), pltpu.VMEM((R, proj_cols), F32)],
        compiler_params=_cparams(("arbitrary",)),
        name=name,
    )(*stream, *operands)


def _state_scratch():
    return [pltpu.VMEM((HEADS, HEAD_W, HEAD_W), F32), pltpu.VMEM((HEADS, HEAD_W, HEAD_W), F32)]


def _hgrn_kernel(nb, hx_ref, hm_ref, nm_ref, w_ref, lb_ref, nw_ref, wl_ref, lv_ref,
                 y_ref, u_ref, st_ref, stm_ref, p_ref, pa_ref, pb_ref):
    i = pl.program_id(0)
    j = i - 1
    _state_in(j, nb, st_ref, stm_ref)

    def body(rd, wr):
        a_q, f_in, a_i, a_g = [rd[:, c * BRANCH_W:(c + 1) * BRANCH_W] for c in range(4)]
        valid = _valid_rows(j == 0)
        lb_floor = lb_ref[0:1, :]
        one_m_lb = lb_ref[1:2, :]
        q = _silu(a_q) * (HEAD_W ** -0.5)
        e = jnp.exp(-jnp.abs(f_in))
        r = 1.0 / (1.0 + e)
        pos = f_in >= 0.0
        g = jnp.log(lb_floor + one_m_lb * jnp.where(pos, r, e * r))
        k = one_m_lb * jnp.where(pos, e * r, r) * valid
        u = _mixer_input(_residual_block(hx_ref, hm_ref, i == 0), nm_ref, _valid_rows(i == 0))
        u_ref[...] = u
        fill = _projection_steps(u, w_ref, wr)
        _gla_block(q, k, a_i, g, a_g, nw_ref[...], wl_ref, lv_ref[...], st_ref, p_ref, y_ref, fill)
        _run_all(fill)

    _skewed(i, pa_ref, pb_ref, body)
    _state_out(j, st_ref, stm_ref)


def _hgrn(hx, hm, n_blocks, norm_mix, w, lb_rows, norm_w, wl, lv, nb):
    return _mixer_call(
        functools.partial(_hgrn_kernel, nb), "hgrn2",
        [hx, hm], _residual_specs(hx, hm, _seq_block(n_blocks, 0)), n_blocks,
        [norm_mix, w, lb_rows, norm_w, wl, lv],
        [_const_spec((1, D_MODEL)), _const_spec(w.shape), _const_spec((8, BRANCH_W)),
         _const_spec((1, HEAD_W)), _const_spec(wl.shape), _const_spec((R, R))],
        w.shape[1], _state_scratch() + [pltpu.VMEM((HEADS, R, R), F32)], emit_u=True)


def _gla_kernel(nb, u_ref, w_ref, w2_ref, b2_ref, nw_ref, wl_ref, lv_ref,
                y_ref, st_ref, stm_ref, p_ref, pa_ref, pb_ref):
    i = pl.program_id(0)
    j = i - 1
    _state_in(j, nb, st_ref, stm_ref)

    def body(rd, wr):
        qk_w = HEADS * C_DK
        c_q = rd[:, :qk_w]
        c_k = rd[:, qk_w:2 * qk_w]
        c_v = rd[:, 2 * qk_w:2 * qk_w + BRANCH_W]
        c_g = rd[:, 2 * qk_w + BRANCH_W:2 * qk_w + 2 * BRANCH_W]
        gk_low = rd[:, 2 * qk_w + 2 * BRANCH_W:]
        z = _dot(gk_low.astype(BF16), w2_ref[...]) + b2_ref[...]
        g = _log_sigmoid(z) * (1.0 / C_GK_NORM)
        fill = _projection_steps(u_ref[...], w_ref, wr)
        _gla_block(c_q * (C_DK ** -0.5), c_k * _valid_rows(j == 0), c_v, g, c_g, nw_ref[...],
                   wl_ref, lv_ref[...], st_ref, p_ref, y_ref, fill)
        _run_all(fill)

    _skewed(i, pa_ref, pb_ref, body)
    _state_out(j, st_ref, stm_ref)


def _u_spec(n_blocks):
    return [pl.BlockSpec((R, D_MODEL), _seq_block(n_blocks, 0))]


def _gla(u, n_blocks, w, w2, b2, norm_w, wl, lv, nb):
    return _mixer_call(
        functools.partial(_gla_kernel, nb), "gla", [u], _u_spec(n_blocks), n_blocks,
        [w, w2, b2, norm_w, wl, lv],
        [_const_spec(w.shape), _const_spec(w2.shape),
         _const_spec(b2.shape), _const_spec((1, HEAD_W)), _const_spec(wl.shape), _const_spec((R, R))],
        w.shape[1], _state_scratch() + [pltpu.VMEM((HEADS, R, R), F32)])


TAIL = 8


def _gdn_kernel(nb, u_ref, w_ref, cw_ref, hp_ref, nw_ref, wl_ref, lv_ref,
                y_ref, st_ref, stm_ref, xx_ref, tailm_ref, x_ref, a_ref, pa_ref, pb_ref):
    i = pl.program_id(0)
    j = i - 1
    _state_in(j, nb, st_ref, stm_ref)

    @pl.when(j <= 0)
    def _():
        xx_ref[0:TAIL, :] = jnp.zeros((TAIL, 3 * BRANCH_W), F32)

    @pl.when((j >= 1) & ((j - 1) % nb == 0))
    def _():
        xx_ref[0:TAIL, :] = tailm_ref[...]

    def body(rd, wr):
        _gdn_body(j, rd, wr, u_ref, w_ref, cw_ref, hp_ref, nw_ref, wl_ref, lv_ref,
                  y_ref, st_ref, xx_ref, x_ref, a_ref)

    _skewed(i, pa_ref, pb_ref, body)

    @pl.when(j == 0)
    def _():
        tailm_ref[...] = xx_ref[0:TAIL, :]

    _state_out(j, st_ref, stm_ref)


def _gdn_body(j, rd, wr, u_ref, w_ref, cw_ref, hp_ref, nw_ref, wl_ref, lv_ref,
              y_ref, st_ref, xx_ref, x_ref, a_ref):
    fill = _projection_steps(u_ref[...], w_ref, wr)
    valid = _valid_rows(j == 0)
    lv = lv_ref[...]
    xx_ref[TAIL:, :] = rd[:, :3 * BRANCH_W]
    b_g = rd[:, 3 * BRANCH_W:4 * BRANCH_W]
    sm = rd[:, 4 * BRANCH_W:]
    conv = jnp.zeros((R, 3 * BRANCH_W), F32)
    for tap in range(CONV_W):
        off = TAIL - (CONV_W - 1) + tap
        conv = conv + xx_ref[off:off + R, :] * cw_ref[tap:tap + 1, :]
    qkv = _silu(conv)
    xx_ref[0:TAIL, :] = xx_ref[R:R + TAIL, :]

    a_neg = hp_ref[0:1, :]
    dt_b = hp_ref[1:2, :]
    log_a = a_neg * _softplus(sm + dt_b)
    la_hi = log_a.astype(BF16)
    la_lo = (log_a - la_hi.astype(F32)).astype(BF16)
    gcum = _dot(wl_ref[W_CUM], la_hi) + _dot(wl_ref[W_CUM], la_lo)
    gcum_t = gcum.T
    beta_all = _sigmoid(sm) * valid

    strict = (lv >= 0) & (lv < LV_DIAG)
    causal = lv >= 0
    eye = (lv == LV_DIAG).astype(F32)

    heads = []
    for h in range(HEADS):
        qh = qkv[:, h * HEAD_W:(h + 1) * HEAD_W]
        kh = qkv[:, BRANCH_W + h * HEAD_W:BRANCH_W + (h + 1) * HEAD_W] * valid
        vh = qkv[:, 2 * BRANCH_W + h * HEAD_W:2 * BRANCH_W + (h + 1) * HEAD_W]
        qh = qh * lax.rsqrt(jnp.sum(qh * qh, axis=-1, keepdims=True) + L2_EPS) * (HEAD_W ** -0.5)
        kh = kh * lax.rsqrt(jnp.sum(kh * kh, axis=-1, keepdims=True) + L2_EPS)
        beta = beta_all[:, SM_BETA + h:SM_BETA + h + 1]
        gcol = gcum[:, SM_DECAY + h:SM_DECAY + h + 1]
        grow = gcum_t[SM_DECAY + h:SM_DECAY + h + 1, :]
        dm = jnp.exp(jnp.minimum(gcol - grow, 0.0))
        q16 = qh.astype(BF16)
        k16 = kh.astype(BF16)
        a_ref[h] = jnp.where(strict, beta * _dot_nt(k16, k16) * dm, 0.0)
        qk = jnp.where(causal, _dot_nt(q16, k16) * dm, 0.0).astype(BF16)
        heads.append((qh, kh, vh, beta, gcol, qk))
        _run_share(fill, HEADS + 1 - h)

    for lvl in range(N_LEVELS):
        m = lv == lvl
        for h in range(HEADS):
            if 2 << lvl == R:
                hs = R // 2
                x_lo = x_ref[h, hs:, hs:].astype(BF16)
                x_up = x_ref[h, :hs, :hs].astype(BF16)
                x_ref[h, hs:, :hs] = -_dot(_dot(x_lo, a_ref[h, hs:, :hs].astype(BF16)).astype(BF16), x_up)
                continue
            l_lvl = jnp.where(m, a_ref[h], 0.0)
            if lvl == 0:
                x_ref[h] = eye - l_lvl
            else:
                xcur = x_ref[h].astype(BF16)
                x_ref[h] = x_ref[h] - _dot(_dot(xcur, l_lvl.astype(BF16)).astype(BF16), xcur)

    for h, (qh, kh, vh, beta, gcol, qk) in enumerate(heads):
        sl = slice(h * HEAD_W, (h + 1) * HEAD_W)
        eg = jnp.exp(gcol)
        rhs = jnp.concatenate([vh * beta, kh * (beta * eg)], axis=-1).astype(BF16)
        sol = _dot(x_ref[h].astype(BF16), rhs)
        u_h = sol[:, :HEAD_W]
        w_h = sol[:, HEAD_W:]
        st = st_ref[h]
        st16 = st.astype(BF16)
        v_new = u_h - _dot_nt(w_h.astype(BF16), st16)
        vn16 = v_new.astype(BF16)
        o = _dot_nt((qh * eg).astype(BF16), st16) + _dot(qk, vn16)
        g_last = gcol[R - 1:R, :]
        ke = (kh * jnp.exp(g_last - gcol)).astype(BF16)
        st_ref[h] = st * jnp.exp(g_last) + _dot_tn(vn16, ke)
        y_ref[:, sl] = _head_out(o, b_g[:, sl], nw_ref[...])
    _run_all(fill)


def _gdn(u, n_blocks, w, conv_w, head_params, norm_w, wl, lv, nb):
    return _mixer_call(
        functools.partial(_gdn_kernel, nb), "gdn", [u], _u_spec(n_blocks), n_blocks,
        [w, conv_w, head_params, norm_w, wl, lv],
        [_const_spec(w.shape), _const_spec((CONV_W, 3 * BRANCH_W)),
         _const_spec((8, LANES)), _const_spec((1, HEAD_W)), _const_spec(wl.shape), _const_spec((R, R))],
        w.shape[1], _state_scratch() + [pltpu.VMEM((R + TAIL, 3 * BRANCH_W), F32),
                            pltpu.VMEM((TAIL, 3 * BRANCH_W), F32),
                            pltpu.VMEM((HEADS, R, R), F32),
                            pltpu.VMEM((HEADS, R, R), F32)])


def _split_hi_lo(x):
    hi = x.astype(BF16)
    return hi, (x - hi.astype(F32)).astype(BF16)


def _merge_kernel(ya_ref, yb_ref, yc_ref, u_ref, hx_ref, hm_ref, wgate_ref, wb_ref, wo_ref, nf_ref,
                  wr_hi_ref, wr_lo_ref, br_ref, hn_ref, u2_ref, cmb_ref, lg_ref):
    i = pl.program_id(0)
    last = pl.num_programs(0) - 2

    @pl.when(i == 0)
    def _():
        lg_ref[...] = jnp.zeros(lg_ref.shape, lg_ref.dtype)

    cmb_ref[...] = _route(lg_ref[...])

    is_meta = i >= last
    h = _residual_block(hx_ref, hm_ref, is_meta)
    u = u_ref[...]
    merged = jnp.zeros((R, D_MODEL), F32)
    for n, y_ref in enumerate((ya_ref, yb_ref, yc_ref)):
        gate = _sigmoid(_dot(u, wgate_ref[:, n * D_MODEL:(n + 1) * D_MODEL]))
        merged = merged + gate * _dot(y_ref[...], wb_ref[0, n])
    hn = h + _dot(merged.astype(BF16), wo_ref[0])
    hn_ref[...] = hn
    u2 = _rmsnorm(hn, nf_ref[...])
    u2_ref[...] = u2.astype(BF16)

    u_hi, u_lo = _split_hi_lo(u2)
    lg_ref[...] = (_dot(u_hi, wr_hi_ref[...]) + _dot(u_hi, wr_lo_ref[...]) + _dot(u_lo, wr_hi_ref[...])
                   + br_ref[...])


def _route(logits):
    lane_i = lax.broadcasted_iota(jnp.int32, (R, LANES), 1)
    lane = lane_i.astype(F32)
    lane_grp = (lane_i // EXP_PER_GROUP).astype(F32)
    neg = jnp.float32(-jnp.inf)
    big = jnp.float32(1e9)
    is_g = (lane_i >= N_EXPERTS) & (lane_i < N_EXPERTS + N_GROUPS)
    lg = jnp.where(is_g, logits, neg)
    mg = jnp.max(lg, axis=-1, keepdims=True)
    zg = jnp.sum(jnp.exp(lg - mg), axis=-1, keepdims=True)
    g_val = 1.0 / zg
    g_idx = jnp.min(jnp.where(lg == mg, lane, big), axis=-1, keepdims=True) - N_EXPERTS
    in_grp = (lane_i < N_EXPERTS) & (lane_grp == g_idx)
    le = jnp.where(in_grp, logits, neg)
    m1 = jnp.max(le, axis=-1, keepdims=True)
    ze = jnp.sum(jnp.exp(le - m1), axis=-1, keepdims=True)
    i1 = jnp.min(jnp.where(le == m1, lane, big), axis=-1, keepdims=True)
    le2 = jnp.where(lane == i1, neg, le)
    m2 = jnp.max(le2, axis=-1, keepdims=True)
    i2 = jnp.min(jnp.where(le2 == m2, lane, big), axis=-1, keepdims=True)
    p1 = 1.0 / ze
    p2 = jnp.exp(m2 - m1) / ze
    den = p1 + p2
    cmb = g_val * jnp.where(lane == i1, p1 / den, jnp.where(lane == i2, p2 / den, 0.0))
    return jnp.where(lane_i == CMB_GROUP_LANE, g_idx, cmb)


def _merge(ya, yb, yc, u, hx, hm, w_gate, wb, wo, layer, nf, wr_hi, wr_lo, br):
    t = ya.shape[0]
    last = t // R - 1
    cur = lambda i: (jnp.minimum(i, last), 0)
    row = lambda w: pl.BlockSpec((R, w), cur)
    return pl.pallas_call(
        _merge_kernel,
        grid=(t // R + 1,),
        in_specs=[row(BRANCH_W), row(BRANCH_W), row(BRANCH_W), row(D_MODEL)]
        + _residual_specs(hx, hm, cur)
        + [_const_spec((D_MODEL, N_BRANCH * D_MODEL)),
                  pl.BlockSpec((1, N_BRANCH, BRANCH_W, D_MODEL), lambda i: (layer, 0, 0, 0)),
                  pl.BlockSpec((1, D_MODEL, D_MODEL), lambda i: (layer, 0, 0)),
                  _const_spec((1, D_MODEL)), _const_spec((D_MODEL, LANES)), _const_spec((D_MODEL, LANES)),
                  _const_spec((1, LANES))],
        out_specs=[row(D_MODEL), row(D_MODEL),
                   pl.BlockSpec((R, LANES), lambda i: (jnp.maximum(i - 1, 0), 0))],
        out_shape=[jax.ShapeDtypeStruct((t, D_MODEL), F32),
                   jax.ShapeDtypeStruct((t, D_MODEL), BF16),
                   jax.ShapeDtypeStruct((t, LANES), F32)],
        scratch_shapes=[pltpu.VMEM((R, LANES), F32)],
        compiler_params=_cparams(("arbitrary",)),
        name="merge_router",
    )(ya, yb, yc, u, hx, hm, w_gate, wb, wo, nf, wr_hi, wr_lo, br)


def _moe_kernel(final, cnt_ref, u_ref, c_ref, h_ref, wg_ref, wu_ref, wd_ref, tri_ref, nfin_ref, o_ref):
    w = pl.program_id(0)
    g = pl.program_id(1)
    u = u_ref[...]
    cmb = c_ref[...]
    gid_row = cmb.T[CMB_GROUP_LANE:CMB_GROUP_LANE + 1, :]
    in_row = jnp.where(gid_row == g.astype(F32), 1.0, 0.0)
    ranks = []
    before = jnp.zeros((1, 1), F32)
    for j in range(MOE_W // R):
        seg = in_row[:, j * R:(j + 1) * R]
        ranks.append(_dot(jnp.broadcast_to(seg, (16, R)).astype(BF16), tri_ref[...])[0:1, :] + before)
        before = before + jnp.sum(seg, axis=-1, keepdims=True)
    rank_row = jnp.concatenate(ranks, axis=-1)
    cmb_hi, cmb_lo = _split_hi_lo(cmb)

    @pl.when(g == 0)
    def _():
        o_ref[...] = h_ref[...]

    count = cnt_ref[w * N_GROUPS + g]
    n_pass = (count + MOE_CAPS[-1] - 1) // MOE_CAPS[-1]
    per_pass = (count + jnp.maximum(n_pass, 1) - 1) // jnp.maximum(n_pass, 1)

    def run(cap):
        slot = lax.broadcasted_iota(jnp.int32, (cap, 1), 0).astype(F32)
        lane = lax.broadcasted_iota(jnp.int32, (cap, LANES), 1)

        def one_pass(c, carry):
            base = (c * cap).astype(F32)
            sel = jnp.where((in_row > 0.0) & (rank_row - base == slot), 1.0, 0.0).astype(BF16)
            x = _dot(sel, u).astype(BF16)
            cw = _dot(sel, cmb_hi) + _dot(sel, cmb_lo)
            y = jnp.zeros((cap, D_MODEL), F32)
            for e in range(EXP_PER_GROUP):
                ce = jnp.sum(jnp.where(lane == g * EXP_PER_GROUP + e, cw, 0.0), axis=-1, keepdims=True)
                hid = _silu(_dot(x, wg_ref[0, 0, e])) * _dot(x, wu_ref[0, 0, e]) * ce
                y = y + _dot(hid.astype(BF16), wd_ref[0, 0, e])
            o_ref[...] += _dot_tn(sel, y.astype(BF16))
            return carry

        lax.fori_loop(0, n_pass, one_pass, 0)

    below = 0
    for cap in MOE_CAPS:
        pl.when((per_pass > below) & (per_pass <= cap))(functools.partial(run, cap))
        below = cap

    if final:
        @pl.when(g == N_GROUPS - 1)
        def _():
            o_ref[...] = _rmsnorm(o_ref[...], nfin_ref[...])


def _moe(u2, cmb, h, wg, wu, wd, layer, norm_final, final):
    t = h.shape[0]
    n_win = t // MOE_W
    assert t % MOE_W == 0
    gid = cmb[:, CMB_GROUP_LANE].astype(jnp.int32).reshape(n_win, MOE_W)
    counts = jnp.sum(gid[:, :, None] == jnp.arange(N_GROUPS, dtype=jnp.int32), axis=1, dtype=jnp.int32)
    tri = jnp.asarray(np.triu(np.ones((R, R), np.float32), 1), dtype=BF16)
    row = lambda width: pl.BlockSpec((MOE_W, width), lambda w, g, cnt: (w, 0))
    wspec = lambda a, b: pl.BlockSpec((1, 1, EXP_PER_GROUP, a, b), lambda w, g, cnt: (layer, g, 0, 0, 0))
    return pl.pallas_call(
        functools.partial(_moe_kernel, final),
        grid_spec=pltpu.PrefetchScalarGridSpec(
            num_scalar_prefetch=1,
            grid=(n_win, N_GROUPS),
            in_specs=[row(D_MODEL), row(LANES),
                      row(D_MODEL),
                      wspec(D_MODEL, EXPERT_HIDDEN), wspec(D_MODEL, EXPERT_HIDDEN),
                      wspec(EXPERT_HIDDEN, D_MODEL),
                      pl.BlockSpec((R, R), lambda w, g, cnt: (0, 0)),
                      pl.BlockSpec((1, D_MODEL), lambda w, g, cnt: (0, 0))],
            out_specs=row(D_MODEL)),
        out_shape=jax.ShapeDtypeStruct((t - R if final else t, D_MODEL), F32),
        compiler_params=_cparams(("arbitrary", "arbitrary")),
        name="moe",
    )(counts.reshape(-1), u2, cmb, h, wg, wu, wd, tri, norm_final)


IN_SPLITS = (512, 512, 512, 512, 1536, 4, 4, 512, 256, 256, 512, 16, 512, 3072)


IN_OFFS = tuple(int(v) for v in np.cumsum((0,) + IN_SPLITS))
(O_AQ, O_AF, O_AI, O_AG, O_BQKV, O_BBETA, O_BDECAY, O_BG, O_CQ, O_CK, O_CV, O_CGK, O_CG, O_GATES, O_END) = IN_OFFS
W_A_COLS = O_BQKV - O_AQ
W_B_COLS = (O_BBETA - O_BQKV) + (O_CQ - O_BG) + LANES
W_C_COLS = (O_CGK - O_CQ) + (O_GATES - O_CG) + LANES
W_G_COLS = O_END - O_GATES
WPREP_ROWS = 128
assert O_BBETA % LANES == SM_BETA and O_BDECAY - O_BBETA == SM_DECAY and O_CGK % LANES == SM_GK


def _wprep_kernel(w_ref, a_ref, b_ref, c_ref, g_ref):
    x = w_ref[0]
    lane = lax.broadcasted_iota(jnp.int32, (WPREP_ROWS, LANES), 1)
    a_ref[...] = x[:, O_AQ:O_BQKV].astype(BF16)
    b0 = O_BBETA - SM_BETA
    small_b = jnp.where(lane < SM_GK, x[:, b0:b0 + LANES], 0.0)
    b_ref[...] = jnp.concatenate([x[:, O_BQKV:O_BBETA], x[:, O_BG:O_CQ], small_b], axis=1).astype(BF16)
    c0 = O_CGK - SM_GK
    small_c = jnp.where((lane >= SM_GK) & (lane < SM_GK + C_GK_RANK), x[:, c0:c0 + LANES], 0.0)
    c_ref[...] = jnp.concatenate([x[:, O_CQ:O_CGK], x[:, O_CG:O_GATES], small_c], axis=1).astype(BF16)
    g_ref[...] = x[:, O_GATES:O_END].astype(BF16)


def _inproj_weights(w_in, layer):
    widths = (W_A_COLS, W_B_COLS, W_C_COLS, W_G_COLS)
    return pl.pallas_call(
        _wprep_kernel,
        grid=(D_MODEL // WPREP_ROWS,),
        in_specs=[pl.BlockSpec((1, WPREP_ROWS, O_END), lambda r: (layer, r, 0))],
        out_specs=[pl.BlockSpec((WPREP_ROWS, n), lambda r: (r, 0)) for n in widths],
        out_shape=[jax.ShapeDtypeStruct((D_MODEL, n), BF16) for n in widths],
        compiler_params=_cparams(("arbitrary",)),
        name="wprep",
    )(w_in)


def kernel(x, meta_tokens, norm_mix, w_in, hgrn_lb_logits, hgrn_norm, gdn_conv, gdn_a_log,
           gdn_dt_bias, gdn_norm, gla_gk_w2, gla_gk_b, gla_norm, w_branch, w_out, norm_ffn,
           router_group_w, router_group_b, router_expert_w, router_expert_b,
           expert_w_gate, expert_w_up, expert_w_down, norm_final):
    batch, seq, d = x.shape
    assert d == D_MODEL and seq % R == 0
    depth = w_in.shape[0]
    nb = seq // R
    lv_np, wl_np = _level_constants()
    lv = jnp.asarray(lv_np)
    wl = jnp.asarray(wl_np, dtype=BF16)

    n_blocks = batch * nb + 1
    hx = x.reshape(batch * seq, d)
    hm = jnp.concatenate([jnp.zeros((R - N_META, d), F32), meta_tokens.astype(F32)], axis=0)

    wg16, wu16, wd16, wb16, wo16 = (w.astype(BF16) for w in (
        expert_w_gate, expert_w_up, expert_w_down, w_branch, w_out))

    lb_p = jax.nn.softmax(hgrn_lb_logits.astype(F32), axis=0)
    lb_all = jnp.maximum(jnp.cumsum(lb_p, axis=0) - lb_p[0:1], 0.0)

    for layer in range(depth):
        w_a, w_b, w_c, w_gate = _inproj_weights(w_in, layer)
        nm = norm_mix[layer][None, :]

        lb = lb_all[layer]
        lb_rows = jnp.zeros((8, BRANCH_W), F32)
        lb_rows = lb_rows.at[0].set(jnp.maximum(lb, LB_FLOOR)).at[1].set(1.0 - lb)
        y_a, u = _hgrn(hx, hm, n_blocks, nm, w_a, lb_rows, hgrn_norm[layer][None, :], wl, lv, nb)

        head_params = jnp.zeros((8, LANES), F32)
        head_params = head_params.at[0, SM_DECAY:SM_DECAY + HEADS].set(-jnp.exp(gdn_a_log[layer]))
        head_params = head_params.at[1, SM_DECAY:SM_DECAY + HEADS].set(gdn_dt_bias[layer])
        y_b = _gdn(u, n_blocks, w_b, gdn_conv[layer], head_params, gdn_norm[layer][None, :], wl, lv, nb)

        w2 = jnp.zeros((LANES, HEADS * C_DK), F32).at[SM_GK:SM_GK + C_GK_RANK].set(gla_gk_w2[layer])
        y_c = _gla(u, n_blocks, w_c, w2.astype(BF16), gla_gk_b[layer][None, :],
                   gla_norm[layer][None, :], wl, lv, nb)

        wr = jnp.zeros((D_MODEL, LANES), F32)
        wr = wr.at[:, :N_EXPERTS].set(router_expert_w[layer])
        wr = wr.at[:, N_EXPERTS:N_EXPERTS + N_GROUPS].set(router_group_w[layer])
        br = jnp.zeros((1, LANES), F32)
        br = br.at[0, :N_EXPERTS].set(router_expert_b[layer])
        br = br.at[0, N_EXPERTS:N_EXPERTS + N_GROUPS].set(router_group_b[layer])
        wr_hi = wr.astype(BF16)
        wr_lo = (wr - wr_hi.astype(F32)).astype(BF16)
        hn, u2, cmb = _merge(y_a, y_b, y_c, u, hx, hm, w_gate, wb16, wo16, layer,
                             norm_ffn[layer][None, :], wr_hi, wr_lo, br)

        hx = hm = _moe(u2, cmb, hn, wg16, wu16, wd16, layer, norm_final[None, :], final=(layer == depth - 1))

    return hx.reshape(batch, seq, d)
```

```python
import functools

import numpy as np
import jax
import jax.numpy as jnp
from jax import lax
from jax.experimental import pallas as pl
from jax.experimental.pallas import tpu as pltpu

F32 = jnp.float32
BF16 = jnp.bfloat16

D_MODEL = 1024
N_META = 16
CONV_W = 4
RMS_EPS = 1e-6
L2_EPS = 1e-6
LB_FLOOR = 1e-30
HEADS = 4
HEAD_W = 128
BRANCH_W = HEADS * HEAD_W
C_DK = 64
C_GK_RANK = 16
C_GK_NORM = 16.0
N_BRANCH = 3
N_GROUPS = 4
EXP_PER_GROUP = 8
N_EXPERTS = N_GROUPS * EXP_PER_GROUP
EXPERT_HIDDEN = 256

R = 256
N_LEVELS = 8
LV_DIAG = N_LEVELS
W_CUM = N_LEVELS
W_SFX = N_LEVELS + 1
LANES = 128

SM_BETA, SM_DECAY, SM_GK = 0, 4, 8

VMEM_LIMIT = 60 * 1024 * 1024
MOE_W = 1280
MOE_CAPS = (320, 384, 448)
CMB_GROUP_LANE = N_EXPERTS


def _cparams(sem):
    return pltpu.CompilerParams(dimension_semantics=sem, vmem_limit_bytes=VMEM_LIMIT)


def _const_spec(shape):
    nd = len(shape)
    return pl.BlockSpec(shape, lambda *_: (0,) * nd)


@functools.lru_cache(maxsize=None)
def _level_constants():
    t = np.arange(R)[:, None]
    s = np.arange(R)[None, :]
    x = np.maximum(t ^ s, 1)
    lv = np.where(s < t, np.floor(np.log2(x)).astype(np.int32), np.where(s == t, LV_DIAG, -1)).astype(np.int32)
    w = np.zeros((N_LEVELS + 2, R, R), np.float32)
    for l in range(N_LEVELS):
        hsz = 1 << l
        for r in range(R):
            hb = (r // hsz) * hsz
            if (r >> l) & 1:
                w[l, r, hb:r + 1] = 1.0
            else:
                w[l, r, r + 1:hb + hsz] = 1.0
    w[W_CUM] = np.tril(np.ones((R, R), np.float32))
    w[W_SFX] = np.triu(np.ones((R, R), np.float32), 1)
    return lv, w


def _dot(a, b):
    return jnp.dot(a, b, preferred_element_type=F32)


def _dot_nt(a, b):
    return lax.dot_general(a, b, (((1,), (1,)), ((), ())), preferred_element_type=F32)


def _dot_tn(a, b):
    return lax.dot_general(a, b, (((0,), (0,)), ((), ())), preferred_element_type=F32)


def _sigmoid(x):
    return 1.0 / (1.0 + jnp.exp(-x))


def _silu(x):
    return x * _sigmoid(x)


def _softplus(x):
    return jnp.maximum(x, 0.0) + jnp.log1p(jnp.exp(-jnp.abs(x)))


def _log_sigmoid(x):
    return -_softplus(-x)


def _rmsnorm(x, w):
    return x * lax.rsqrt(jnp.mean(x * x, axis=-1, keepdims=True) + RMS_EPS) * w


def _valid_rows(is_meta):
    row = lax.broadcasted_iota(jnp.int32, (R, 1), 0)
    first_valid = jnp.where(is_meta, R - N_META, 0)
    return jnp.where(row >= first_valid, 1.0, 0.0).astype(F32)


def _residual_block(hx_ref, hm_ref, is_meta):
    return jnp.where(is_meta, hm_ref[...], hx_ref[...])


def _mixer_input(h, nw_ref, valid):
    return (_rmsnorm(h, nw_ref[...]) * valid).astype(BF16)


PROJ_CHUNK = 256


def _projection_steps(u, w_ref, out_ref):
    n_cols = w_ref.shape[1]

    def step(c0):
        sl = slice(c0, min(c0 + PROJ_CHUNK, n_cols))
        out_ref[:, sl] = _dot(u, w_ref[:, sl])

    return [functools.partial(step, c0) for c0 in range(0, n_cols, PROJ_CHUNK)]


def _run_share(steps, slots_left):
    for _ in range(-(-len(steps) // slots_left)):
        steps.pop(0)()


def _run_all(steps):
    while steps:
        steps.pop(0)()


def _seq_block(n_blocks, lag):
    return lambda i: ((jnp.clip(i - lag, 0, n_blocks - 1) + n_blocks - 1) % n_blocks, 0)


def _state_in(j, nb, st_ref, stm_ref):
    @pl.when(j <= 0)
    def _():
        st_ref[...] = jnp.zeros(st_ref.shape, st_ref.dtype)

    @pl.when((j >= 1) & ((j - 1) % nb == 0))
    def _():
        st_ref[...] = stm_ref[...]


def _state_out(j, st_ref, stm_ref):
    @pl.when(j == 0)
    def _():
        stm_ref[...] = st_ref[...]


def _skewed(i, pa_ref, pb_ref, body):
    @pl.when(i == 0)
    def _():
        pb_ref[...] = jnp.zeros(pb_ref.shape, pb_ref.dtype)

    @pl.when(i % 2 == 0)
    def _():
        body(pb_ref, pa_ref)

    @pl.when(i % 2 == 1)
    def _():
        body(pa_ref, pb_ref)


def _head_out(o, gate, nw):
    return (_rmsnorm(o, nw) * _silu(gate)).astype(BF16)


def _gla_block(q, k, v, g, gate, nw, wl_ref, lv, st_ref, p_ref, y_ref, fill):
    width = q.shape[1]
    per_tile = LANES * HEADS // width
    lane = lax.broadcasted_iota(jnp.int32, (1, LANES), 1)
    own_lanes = [jnp.where(lane // (LANES // per_tile) == j, 1.0, 0.0).astype(BF16) for j in range(per_tile)]

    def tile(a, h):
        t0 = (h // per_tile) * LANES
        return a[:, t0:t0 + LANES]

    def own(a_tile, h):
        return a_tile if per_tile == 1 else a_tile * own_lanes[h % per_tile]

    g16 = g.astype(BF16)
    b = _dot(wl_ref[W_CUM], g16)
    rowi = lax.broadcasted_iota(jnp.int32, (R, 1), 0)
    for lvl in range(N_LEVELS):
        hsz = 1 << lvl
        if hsz >= 8:
            b3 = b.reshape(R // (2 * hsz), 2 * hsz, width)
            b_m = jnp.broadcast_to(b3[:, hsz - 1:hsz, :], b3.shape).reshape(R, width)
            f = jnp.exp(-jnp.abs(b - b_m))
        else:
            f = jnp.exp(_dot(wl_ref[lvl], g16))
        lower = ((rowi >> lvl) & 1) == 1
        z = (jnp.where(lower, q, k) * f).astype(BF16)
        if 4 * hsz >= R:
            for h in range(HEADS):
                zt = tile(z, h)
                for r0 in range(0, R, 2 * hsz):
                    p_ref[h, r0 + hsz:r0 + 2 * hsz, r0:r0 + hsz] = _dot_nt(
                        own(zt[r0 + hsz:r0 + 2 * hsz], h), zt[r0:r0 + hsz])
        else:
            m = lv == lvl
            for h in range(HEADS):
                zt = tile(z, h)
                full = _dot_nt(own(zt, h), zt)
                if lvl == 0:
                    p_ref[h] = jnp.where(m, full, 0.0)
                else:
                    p_ref[h] = jnp.where(m, full, p_ref[h])
        _run_share(fill, N_LEVELS - lvl)
    q16 = q.astype(BF16)
    k16 = k.astype(BF16)
    m = lv == LV_DIAG
    for h in range(HEADS):
        p_ref[h] = jnp.where(m, _dot_nt(own(tile(q16, h), h), tile(k16, h)), p_ref[h])
    sfx = _dot(wl_ref[W_SFX], g16)
    qe = (q * jnp.exp(b)).astype(BF16)
    ke = (k * jnp.exp(sfx)).astype(BF16)
    dec = jnp.exp(b[R - 1:R, :])
    v16 = v.astype(BF16)
    for h in range(HEADS):
        sl = slice(h * HEAD_W, (h + 1) * HEAD_W)
        st = st_ref[h]
        o = _dot(p_ref[h].astype(BF16), v16[:, sl]) + _dot_nt(tile(qe, h), st.astype(BF16))
        st_ref[h] = st * tile(dec, h) + _dot_tn(v16[:, sl], own(tile(ke, h), h))
        y_ref[:, sl] = _head_out(o, gate[:, sl], nw)


def _residual_specs(hx, hm, block_of_step):
    hx_last = hx.shape[0] // R - 1
    hm_last = hm.shape[0] // R - 1
    return [pl.BlockSpec((R, D_MODEL), lambda i: (jnp.minimum(block_of_step(i)[0], hx_last), 0)),
            pl.BlockSpec((R, D_MODEL), lambda i: (hm_last, 0))]


def _mixer_call(kernel_fn, name, stream, stream_specs, n_blocks, operands, operand_specs, proj_cols, scratch,
                emit_u=False):
    y_spec = pl.BlockSpec((R, BRANCH_W), _seq_block(n_blocks, 1))
    y_shape = jax.ShapeDtypeStruct((n_blocks * R, BRANCH_W), BF16)
    u_spec = pl.BlockSpec((R, D_MODEL), _seq_block(n_blocks, 0))
    u_shape = jax.ShapeDtypeStruct((n_blocks * R, D_MODEL), BF16)
    return pl.pallas_call(
        kernel_fn,
        grid=(n_blocks + 1,),
        in_specs=stream_specs + operand_specs,
        out_specs=[y_spec, u_spec] if emit_u else y_spec,
        out_shape=[y_shape, u_shape] if emit_u else y_shape,
        scratch_shapes=scratch + [pltpu.VMEM((R, proj_cols), F32), pltpu.VMEM((R, proj_cols), F32)],
        compiler_params=_cparams(("arbitrary",)),
        name=name,
    )(*stream, *operands)


def _state_scratch():
    return [pltpu.VMEM((HEADS, HEAD_W, HEAD_W), F32), pltpu.VMEM((HEADS, HEAD_W, HEAD_W), F32)]


def _hgrn_kernel(nb, hx_ref, hm_ref, nm_ref, w_ref, lb_ref, nw_ref, wl_ref, lv_ref,
                 y_ref, u_ref, st_ref, stm_ref, p_ref, pa_ref, pb_ref):
    i = pl.program_id(0)
    j = i - 1
    _state_in(j, nb, st_ref, stm_ref)

    def body(rd, wr):
        a_q, f_in, a_i, a_g = [rd[:, c * BRANCH_W:(c + 1) * BRANCH_W] for c in range(4)]
        valid = _valid_rows(j == 0)
        lb_floor = lb_ref[0:1, :]
        one_m_lb = lb_ref[1:2, :]
        q = _silu(a_q) * (HEAD_W ** -0.5)
        e = jnp.exp(-jnp.abs(f_in))
        r = 1.0 / (1.0 + e)
        pos = f_in >= 0.0
        g = jnp.log(lb_floor + one_m_lb * jnp.where(pos, r, e * r))
        k = one_m_lb * jnp.where(pos, e * r, r) * valid
        u = _mixer_input(_residual_block(hx_ref, hm_ref, i == 0), nm_ref, _valid_rows(i == 0))
        u_ref[...] = u
        fill = _projection_steps(u, w_ref, wr)
        _gla_block(q, k, a_i, g, a_g, nw_ref[...], wl_ref, lv_ref[...], st_ref, p_ref, y_ref, fill)
        _run_all(fill)

    _skewed(i, pa_ref, pb_ref, body)
    _state_out(j, st_ref, stm_ref)


def _hgrn(hx, hm, n_blocks, norm_mix, w, lb_rows, norm_w, wl, lv, nb):
    return _mixer_call(
        functools.partial(_hgrn_kernel, nb), "hgrn2",
        [hx, hm], _residual_specs(hx, hm, _seq_block(n_blocks, 0)), n_blocks,
        [norm_mix, w, lb_rows, norm_w, wl, lv],
        [_const_spec((1, D_MODEL)), _window_spec(W_A_COL0, W_A_COLS), _const_spec((8, BRANCH_W)),
         _const_spec((1, HEAD_W)), _const_spec(wl.shape), _const_spec((R, R))],
        W_A_COLS, _state_scratch() + [pltpu.VMEM((HEADS, R, R), F32)], emit_u=True)


def _gla_kernel(nb, u_ref, w_ref, w2_ref, b2_ref, nw_ref, wl_ref, lv_ref,
                y_ref, st_ref, stm_ref, p_ref, pa_ref, pb_ref):
    i = pl.program_id(0)
    j = i - 1
    _state_in(j, nb, st_ref, stm_ref)

    def body(rd, wr):
        qk_w = HEADS * C_DK
        c_q = rd[:, :qk_w]
        c_k = rd[:, qk_w:2 * qk_w]
        c_v = rd[:, 2 * qk_w:2 * qk_w + BRANCH_W]
        c_g = rd[:, 2 * qk_w + BRANCH_W:2 * qk_w + 2 * BRANCH_W]
        gk_low = rd[:, 2 * qk_w + 2 * BRANCH_W:]
        z = _dot(gk_low.astype(BF16), w2_ref[...]) + b2_ref[...]
        g = _log_sigmoid(z) * (1.0 / C_GK_NORM)
        fill = _projection_steps(u_ref[...], w_ref, wr)
        _gla_block(c_q * (C_DK ** -0.5), c_k * _valid_rows(j == 0), c_v, g, c_g, nw_ref[...],
                   wl_ref, lv_ref[...], st_ref, p_ref, y_ref, fill)
        _run_all(fill)

    _skewed(i, pa_ref, pb_ref, body)
    _state_out(j, st_ref, stm_ref)


def _u_spec(n_blocks):
    return [pl.BlockSpec((R, D_MODEL), _seq_block(n_blocks, 0))]


def _gla(u, n_blocks, w, w2, b2, norm_w, wl, lv, nb):
    return _mixer_call(
        functools.partial(_gla_kernel, nb), "gla", [u], _u_spec(n_blocks), n_blocks,
        [w, w2, b2, norm_w, wl, lv],
        [_window_spec(W_C_COL0, W_C_COLS), _const_spec(w2.shape),
         _const_spec(b2.shape), _const_spec((1, HEAD_W)), _const_spec(wl.shape), _const_spec((R, R))],
        W_C_COLS, _state_scratch() + [pltpu.VMEM((HEADS, R, R), F32)])


TAIL = 8


def _gdn_kernel(nb, u_ref, w_ref, cw_ref, hp_ref, nw_ref, wl_ref, lv_ref,
                y_ref, st_ref, stm_ref, xx_ref, tailm_ref, x_ref, a_ref, pa_ref, pb_ref):
    i = pl.program_id(0)
    j = i - 1
    _state_in(j, nb, st_ref, stm_ref)

    @pl.when(j <= 0)
    def _():
        xx_ref[0:TAIL, :] = jnp.zeros((TAIL, 3 * BRANCH_W), F32)

    @pl.when((j >= 1) & ((j - 1) % nb == 0))
    def _():
        xx_ref[0:TAIL, :] = tailm_ref[...]

    def body(rd, wr):
        _gdn_body(j, rd, wr, u_ref, w_ref, cw_ref, hp_ref, nw_ref, wl_ref, lv_ref,
                  y_ref, st_ref, xx_ref, x_ref, a_ref)

    _skewed(i, pa_ref, pb_ref, body)

    @pl.when(j == 0)
    def _():
        tailm_ref[...] = xx_ref[0:TAIL, :]

    _state_out(j, st_ref, stm_ref)


def _gdn_body(j, rd, wr, u_ref, w_ref, cw_ref, hp_ref, nw_ref, wl_ref, lv_ref,
              y_ref, st_ref, xx_ref, x_ref, a_ref):
    fill = _projection_steps(u_ref[...], w_ref, wr)
    valid = _valid_rows(j == 0)
    lv = lv_ref[...]
    xx_ref[TAIL:, :] = rd[:, :3 * BRANCH_W]
    b_g = rd[:, 3 * BRANCH_W:4 * BRANCH_W]
    sm = rd[:, 4 * BRANCH_W:]
    conv = jnp.zeros((R, 3 * BRANCH_W), F32)
    for tap in range(CONV_W):
        off = TAIL - (CONV_W - 1) + tap
        conv = conv + xx_ref[off:off + R, :] * cw_ref[tap:tap + 1, :]
    qkv = _silu(conv)
    xx_ref[0:TAIL, :] = xx_ref[R:R + TAIL, :]

    a_neg = hp_ref[0:1, :]
    dt_b = hp_ref[1:2, :]
    log_a = a_neg * _softplus(sm + dt_b)
    la_hi = log_a.astype(BF16)
    la_lo = (log_a - la_hi.astype(F32)).astype(BF16)
    gcum = _dot(wl_ref[W_CUM], la_hi) + _dot(wl_ref[W_CUM], la_lo)
    gcum_t = gcum.T
    beta_all = _sigmoid(sm) * valid

    strict = (lv >= 0) & (lv < LV_DIAG)
    causal = lv >= 0
    eye = (lv == LV_DIAG).astype(F32)

    heads = []
    for h in range(HEADS):
        qh = qkv[:, h * HEAD_W:(h + 1) * HEAD_W]
        kh = qkv[:, BRANCH_W + h * HEAD_W:BRANCH_W + (h + 1) * HEAD_W] * valid
        vh = qkv[:, 2 * BRANCH_W + h * HEAD_W:2 * BRANCH_W + (h + 1) * HEAD_W]
        qh = qh * lax.rsqrt(jnp.sum(qh * qh, axis=-1, keepdims=True) + L2_EPS) * (HEAD_W ** -0.5)
        kh = kh * lax.rsqrt(jnp.sum(kh * kh, axis=-1, keepdims=True) + L2_EPS)
        beta = beta_all[:, SM_BETA + h:SM_BETA + h + 1]
        gcol = gcum[:, SM_DECAY + h:SM_DECAY + h + 1]
        grow = gcum_t[SM_DECAY + h:SM_DECAY + h + 1, :]
        dm = jnp.exp(jnp.minimum(gcol - grow, 0.0))
        q16 = qh.astype(BF16)
        k16 = kh.astype(BF16)
        a_ref[h] = jnp.where(strict, beta * _dot_nt(k16, k16) * dm, 0.0)
        qk = jnp.where(causal, _dot_nt(q16, k16) * dm, 0.0).astype(BF16)
        heads.append((qh, kh, vh, beta, gcol, qk))
        _run_share(fill, HEADS + 1 - h)

    for lvl in range(N_LEVELS):
        m = lv == lvl
        for h in range(HEADS):
            if 2 << lvl == R:
                hs = R // 2
                x_lo = x_ref[h, hs:, hs:].astype(BF16)
                x_up = x_ref[h, :hs, :hs].astype(BF16)
                x_ref[h, hs:, :hs] = -_dot(_dot(x_lo, a_ref[h, hs:, :hs].astype(BF16)).astype(BF16), x_up)
                continue
            l_lvl = jnp.where(m, a_ref[h], 0.0)
            if lvl == 0:
                x_ref[h] = eye - l_lvl
            else:
                xcur = x_ref[h].astype(BF16)
                x_ref[h] = x_ref[h] - _dot(_dot(xcur, l_lvl.astype(BF16)).astype(BF16), xcur)

    for h, (qh, kh, vh, beta, gcol, qk) in enumerate(heads):
        sl = slice(h * HEAD_W, (h + 1) * HEAD_W)
        eg = jnp.exp(gcol)
        rhs = jnp.concatenate([vh * beta, kh * (beta * eg)], axis=-1).astype(BF16)
        sol = _dot(x_ref[h].astype(BF16), rhs)
        u_h = sol[:, :HEAD_W]
        w_h = sol[:, HEAD_W:]
        st = st_ref[h]
        st16 = st.astype(BF16)
        v_new = u_h - _dot_nt(w_h.astype(BF16), st16)
        vn16 = v_new.astype(BF16)
        o = _dot_nt((qh * eg).astype(BF16), st16) + _dot(qk, vn16)
        g_last = gcol[R - 1:R, :]
        ke = (kh * jnp.exp(g_last - gcol)).astype(BF16)
        st_ref[h] = st * jnp.exp(g_last) + _dot_tn(vn16, ke)
        y_ref[:, sl] = _head_out(o, b_g[:, sl], nw_ref[...])
    _run_all(fill)


def _gdn(u, n_blocks, w, conv_w, head_params, norm_w, wl, lv, nb):
    return _mixer_call(
        functools.partial(_gdn_kernel, nb), "gdn", [u], _u_spec(n_blocks), n_blocks,
        [w, conv_w, head_params, norm_w, wl, lv],
        [_window_spec(W_B_COL0, W_B_COLS), _const_spec((CONV_W, 3 * BRANCH_W)),
         _const_spec((8, LANES)), _const_spec((1, HEAD_W)), _const_spec(wl.shape), _const_spec((R, R))],
        W_B_COLS, _state_scratch() + [pltpu.VMEM((R + TAIL, 3 * BRANCH_W), F32),
                            pltpu.VMEM((TAIL, 3 * BRANCH_W), F32),
                            pltpu.VMEM((HEADS, R, R), F32),
                            pltpu.VMEM((HEADS, R, R), F32)])


def _split_hi_lo(x):
    hi = x.astype(BF16)
    return hi, (x - hi.astype(F32)).astype(BF16)


def _merge_kernel(ya_ref, yb_ref, yc_ref, u_ref, hx_ref, hm_ref, wgate_ref, wb_ref, wo_ref, nf_ref,
                  wr_hi_ref, wr_lo_ref, br_ref, hn_ref, u2_ref, cmb_ref, lg_ref):
    i = pl.program_id(0)
    last = pl.num_programs(0) - 2

    @pl.when(i == 0)
    def _():
        lg_ref[...] = jnp.zeros(lg_ref.shape, lg_ref.dtype)

    cmb_ref[...] = _route(lg_ref[...])

    is_meta = i >= last
    h = _residual_block(hx_ref, hm_ref, is_meta)
    u = u_ref[...]
    merged = jnp.zeros((R, D_MODEL), F32)
    for n, y_ref in enumerate((ya_ref, yb_ref, yc_ref)):
        gate = _sigmoid(_dot(u, wgate_ref[:, n * D_MODEL:(n + 1) * D_MODEL]))
        merged = merged + gate * _dot(y_ref[...], wb_ref[0, n])
    hn = h + _dot(merged.astype(BF16), wo_ref[0])
    hn_ref[...] = hn
    u2 = _rmsnorm(hn, nf_ref[...])
    u2_ref[...] = u2.astype(BF16)

    u_hi, u_lo = _split_hi_lo(u2)
    lg_ref[...] = (_dot(u_hi, wr_hi_ref[...]) + _dot(u_hi, wr_lo_ref[...]) + _dot(u_lo, wr_hi_ref[...])
                   + br_ref[...])


def _route(logits):
    lane_i = lax.broadcasted_iota(jnp.int32, (R, LANES), 1)
    lane = lane_i.astype(F32)
    lane_grp = (lane_i // EXP_PER_GROUP).astype(F32)
    neg = jnp.float32(-jnp.inf)
    big = jnp.float32(1e9)
    is_g = (lane_i >= N_EXPERTS) & (lane_i < N_EXPERTS + N_GROUPS)
    lg = jnp.where(is_g, logits, neg)
    mg = jnp.max(lg, axis=-1, keepdims=True)
    zg = jnp.sum(jnp.exp(lg - mg), axis=-1, keepdims=True)
    g_val = 1.0 / zg
    g_idx = jnp.min(jnp.where(lg == mg, lane, big), axis=-1, keepdims=True) - N_EXPERTS
    in_grp = (lane_i < N_EXPERTS) & (lane_grp == g_idx)
    le = jnp.where(in_grp, logits, neg)
    m1 = jnp.max(le, axis=-1, keepdims=True)
    ze = jnp.sum(jnp.exp(le - m1), axis=-1, keepdims=True)
    i1 = jnp.min(jnp.where(le == m1, lane, big), axis=-1, keepdims=True)
    le2 = jnp.where(lane == i1, neg, le)
    m2 = jnp.max(le2, axis=-1, keepdims=True)
    i2 = jnp.min(jnp.where(le2 == m2, lane, big), axis=-1, keepdims=True)
    p1 = 1.0 / ze
    p2 = jnp.exp(m2 - m1) / ze
    den = p1 + p2
    cmb = g_val * jnp.where(lane == i1, p1 / den, jnp.where(lane == i2, p2 / den, 0.0))
    return jnp.where(lane_i == CMB_GROUP_LANE, g_idx, cmb)


def _merge(ya, yb, yc, u, hx, hm, w_gate, wb, wo, layer, nf, wr_hi, wr_lo, br):
    t = ya.shape[0]
    last = t // R - 1
    cur = lambda i: (jnp.minimum(i, last), 0)
    row = lambda w: pl.BlockSpec((R, w), cur)
    return pl.pallas_call(
        _merge_kernel,
        grid=(t // R + 1,),
        in_specs=[row(BRANCH_W), row(BRANCH_W), row(BRANCH_W), row(D_MODEL)]
        + _residual_specs(hx, hm, cur)
        + [_window_spec(W_G_COL0, W_G_COLS),
                  pl.BlockSpec((1, N_BRANCH, BRANCH_W, D_MODEL), lambda i: (layer, 0, 0, 0)),
                  pl.BlockSpec((1, D_MODEL, D_MODEL), lambda i: (layer, 0, 0)),
                  _const_spec((1, D_MODEL)), _const_spec((D_MODEL, LANES)), _const_spec((D_MODEL, LANES)),
                  _const_spec((1, LANES))],
        out_specs=[row(D_MODEL), row(D_MODEL),
                   pl.BlockSpec((R, LANES), lambda i: (jnp.maximum(i - 1, 0), 0))],
        out_shape=[jax.ShapeDtypeStruct((t, D_MODEL), F32),
                   jax.ShapeDtypeStruct((t, D_MODEL), BF16),
                   jax.ShapeDtypeStruct((t, LANES), F32)],
        scratch_shapes=[pltpu.VMEM((R, LANES), F32)],
        compiler_params=_cparams(("arbitrary",)),
        name="merge_router",
    )(ya, yb, yc, u, hx, hm, w_gate, wb, wo, nf, wr_hi, wr_lo, br)


def _moe_kernel(final, cnt_ref, u_ref, c_ref, h_ref, wg_ref, wu_ref, wd_ref, tri_ref, nfin_ref, o_ref):
    w = pl.program_id(0)
    g = pl.program_id(1)
    u = u_ref[...]
    cmb = c_ref[...]
    gid_row = cmb.T[CMB_GROUP_LANE:CMB_GROUP_LANE + 1, :]
    in_row = jnp.where(gid_row == g.astype(F32), 1.0, 0.0)
    ranks = []
    before = jnp.zeros((1, 1), F32)
    for j in range(MOE_W // R):
        seg = in_row[:, j * R:(j + 1) * R]
        ranks.append(_dot(jnp.broadcast_to(seg, (16, R)).astype(BF16), tri_ref[...])[0:1, :] + before)
        before = before + jnp.sum(seg, axis=-1, keepdims=True)
    rank_row = jnp.concatenate(ranks, axis=-1)
    cmb_hi, cmb_lo = _split_hi_lo(cmb)

    @pl.when(g == 0)
    def _():
        o_ref[...] = h_ref[...]

    count = cnt_ref[w * N_GROUPS + g]
    n_pass = (count + MOE_CAPS[-1] - 1) // MOE_CAPS[-1]
    per_pass = (count + jnp.maximum(n_pass, 1) - 1) // jnp.maximum(n_pass, 1)

    def run(cap):
        slot = lax.broadcasted_iota(jnp.int32, (cap, 1), 0).astype(F32)
        lane = lax.broadcasted_iota(jnp.int32, (cap, LANES), 1)

        def one_pass(c, carry):
            base = (c * cap).astype(F32)
            sel = jnp.where((in_row > 0.0) & (rank_row - base == slot), 1.0, 0.0).astype(BF16)
            x = _dot(sel, u).astype(BF16)
            cw = _dot(sel, cmb_hi) + _dot(sel, cmb_lo)
            y = jnp.zeros((cap, D_MODEL), F32)
            for e in range(EXP_PER_GROUP):
                ce = jnp.sum(jnp.where(lane == g * EXP_PER_GROUP + e, cw, 0.0), axis=-1, keepdims=True)
                hid = _silu(_dot(x, wg_ref[0, 0, e])) * _dot(x, wu_ref[0, 0, e]) * ce
                y = y + _dot(hid.astype(BF16), wd_ref[0, 0, e])
            o_ref[...] += _dot_tn(sel, y.astype(BF16))
            return carry

        lax.fori_loop(0, n_pass, one_pass, 0)

    below = 0
    for cap in MOE_CAPS:
        pl.when((per_pass > below) & (per_pass <= cap))(functools.partial(run, cap))
        below = cap

    if final:
        @pl.when(g == N_GROUPS - 1)
        def _():
            o_ref[...] = _rmsnorm(o_ref[...], nfin_ref[...])


def _moe(u2, cmb, h, wg, wu, wd, layer, norm_final, final):
    t = h.shape[0]
    n_win = t // MOE_W
    assert t % MOE_W == 0
    gid = cmb[:, CMB_GROUP_LANE].astype(jnp.int32).reshape(n_win, MOE_W)
    counts = jnp.sum(gid[:, :, None] == jnp.arange(N_GROUPS, dtype=jnp.int32), axis=1, dtype=jnp.int32)
    tri = jnp.asarray(np.triu(np.ones((R, R), np.float32), 1), dtype=BF16)
    row = lambda width: pl.BlockSpec((MOE_W, width), lambda w, g, cnt: (w, 0))
    wspec = lambda a, b: pl.BlockSpec((1, 1, EXP_PER_GROUP, a, b), lambda w, g, cnt: (layer, g, 0, 0, 0))
    return pl.pallas_call(
        functools.partial(_moe_kernel, final),
        grid_spec=pltpu.PrefetchScalarGridSpec(
            num_scalar_prefetch=1,
            grid=(n_win, N_GROUPS),
            in_specs=[row(D_MODEL), row(LANES),
                      row(D_MODEL),
                      wspec(D_MODEL, EXPERT_HIDDEN), wspec(D_MODEL, EXPERT_HIDDEN),
                      wspec(EXPERT_HIDDEN, D_MODEL),
                      pl.BlockSpec((R, R), lambda w, g, cnt: (0, 0)),
                      pl.BlockSpec((1, D_MODEL), lambda w, g, cnt: (0, 0))],
            out_specs=row(D_MODEL)),
        out_shape=jax.ShapeDtypeStruct((t - R if final else t, D_MODEL), F32),
        compiler_params=_cparams(("arbitrary", "arbitrary")),
        name="moe",
    )(counts.reshape(-1), u2, cmb, h, wg, wu, wd, tri, norm_final)


IN_SPLITS = (512, 512, 512, 512, 1536, 4, 4, 512, 256, 256, 512, 16, 512, 3072)


IN_OFFS = tuple(int(v) for v in np.cumsum((0,) + IN_SPLITS))
(O_AQ, O_AF, O_AI, O_AG, O_BQKV, O_BBETA, O_BDECAY, O_BG, O_CQ, O_CK, O_CV, O_CGK, O_CG, O_GATES, O_END) = IN_OFFS
W_A_COLS = O_BQKV - O_AQ
W_B_COLS = (O_BBETA - O_BQKV) + (O_CQ - O_BG) + LANES
W_C_COLS = (O_CGK - O_CQ) + (O_GATES - O_CG) + LANES
W_G_COLS = O_END - O_GATES
W_A_COL0 = 0
W_B_COL0 = W_A_COL0 + W_A_COLS
W_C_COL0 = W_B_COL0 + W_B_COLS
W_G_COL0 = W_C_COL0 + W_C_COLS
W_ALL_COLS = W_G_COL0 + W_G_COLS
assert O_BBETA % LANES == SM_BETA and O_BDECAY - O_BBETA == SM_DECAY and O_CGK % LANES == SM_GK


@functools.lru_cache(maxsize=None)
def _wprep_table():
    rows = []

    def run(src0, n_cols):
        rows.extend((src0 + c, 0, LANES) for c in range(0, n_cols, LANES))

    run(O_AQ, W_A_COLS)
    run(O_BQKV, O_BBETA - O_BQKV)
    run(O_BG, O_CQ - O_BG)
    rows.append((O_BBETA - SM_BETA, SM_BETA, SM_GK))
    run(O_CQ, O_CGK - O_CQ)
    run(O_CG, O_GATES - O_CG)
    rows.append((O_CGK - SM_GK, SM_GK, SM_GK + C_GK_RANK))
    run(O_GATES, W_G_COLS)
    table = np.asarray(rows, np.int32)
    assert table.shape[0] * LANES == W_ALL_COLS and (table[:, 0] % 8 == 0).all()
    return table


def _wprep_kernel(tab_ref, wt_ref, o_ref):
    j = pl.program_id(0)
    t = wt_ref[0].T
    lane = lax.broadcasted_iota(jnp.int32, t.shape, 1)
    keep = (lane >= tab_ref[3 * j + 1]) & (lane < tab_ref[3 * j + 2])
    o_ref[...] = jnp.where(keep, t, 0.0).astype(BF16)


def _inproj_weights(w_in_t, layer):
    table = jnp.asarray(_wprep_table().reshape(-1))
    return pl.pallas_call(
        _wprep_kernel,
        grid_spec=pltpu.PrefetchScalarGridSpec(
            num_scalar_prefetch=1,
            grid=(W_ALL_COLS // LANES,),
            in_specs=[pl.BlockSpec((pl.Element(1), pl.Element(LANES), pl.Element(D_MODEL)),
                                   lambda j, tab: (layer, pl.multiple_of(tab[3 * j], 8), 0))],
            out_specs=pl.BlockSpec((D_MODEL, LANES), lambda j, tab: (0, j))),
        out_shape=jax.ShapeDtypeStruct((D_MODEL, W_ALL_COLS), BF16),
        compiler_params=_cparams(("arbitrary",)),
        name="wprep",
    )(table, w_in_t)


def _window_spec(col0, width):
    return pl.BlockSpec((pl.Element(D_MODEL), pl.Element(width)), lambda *_: (0, col0))


def kernel(x, meta_tokens, norm_mix, w_in, hgrn_lb_logits, hgrn_norm, gdn_conv, gdn_a_log,
           gdn_dt_bias, gdn_norm, gla_gk_w2, gla_gk_b, gla_norm, w_branch, w_out, norm_ffn,
           router_group_w, router_group_b, router_expert_w, router_expert_b,
           expert_w_gate, expert_w_up, expert_w_down, norm_final):
    batch, seq, d = x.shape
    assert d == D_MODEL and seq % R == 0
    depth = w_in.shape[0]
    nb = seq // R
    lv_np, wl_np = _level_constants()
    lv = jnp.asarray(lv_np)
    wl = jnp.asarray(wl_np, dtype=BF16)

    n_blocks = batch * nb + 1
    hx = x.reshape(batch * seq, d)
    hm = jnp.concatenate([jnp.zeros((R - N_META, d), F32), meta_tokens.astype(F32)], axis=0)

    w_in_t = jnp.swapaxes(w_in, 1, 2)
    wg16, wu16, wd16, wb16, wo16 = (w.astype(BF16) for w in (
        expert_w_gate, expert_w_up, expert_w_down, w_branch, w_out))

    lb_p = jax.nn.softmax(hgrn_lb_logits.astype(F32), axis=0)
    lb_all = jnp.maximum(jnp.cumsum(lb_p, axis=0) - lb_p[0:1], 0.0)

    for layer in range(depth):
        w_a = w_b = w_c = w_gate = _inproj_weights(w_in_t, layer)
        nm = norm_mix[layer][None, :]

        lb = lb_all[layer]
        lb_rows = jnp.zeros((8, BRANCH_W), F32)
        lb_rows = lb_rows.at[0].set(jnp.maximum(lb, LB_FLOOR)).at[1].set(1.0 - lb)
        y_a, u = _hgrn(hx, hm, n_blocks, nm, w_a, lb_rows, hgrn_norm[layer][None, :], wl, lv, nb)

        head_params = jnp.zeros((8, LANES), F32)
        head_params = head_params.at[0, SM_DECAY:SM_DECAY + HEADS].set(-jnp.exp(gdn_a_log[layer]))
        head_params = head_params.at[1, SM_DECAY:SM_DECAY + HEADS].set(gdn_dt_bias[layer])
        y_b = _gdn(u, n_blocks, w_b, gdn_conv[layer], head_params, gdn_norm[layer][None, :], wl, lv, nb)

        w2 = jnp.zeros((LANES, HEADS * C_DK), F32).at[SM_GK:SM_GK + C_GK_RANK].set(gla_gk_w2[layer])
        y_c = _gla(u, n_blocks, w_c, w2.astype(BF16), gla_gk_b[layer][None, :],
                   gla_norm[layer][None, :], wl, lv, nb)

        wr = jnp.zeros((D_MODEL, LANES), F32)
        wr = wr.at[:, :N_EXPERTS].set(router_expert_w[layer])
        wr = wr.at[:, N_EXPERTS:N_EXPERTS + N_GROUPS].set(router_group_w[layer])
        br = jnp.zeros((1, LANES), F32)
        br = br.at[0, :N_EXPERTS].set(router_expert_b[layer])
        br = br.at[0, N_EXPERTS:N_EXPERTS + N_GROUPS].set(router_group_b[layer])
        wr_hi = wr.astype(BF16)
        wr_lo = (wr - wr_hi.astype(F32)).astype(BF16)
        hn, u2, cmb = _merge(y_a, y_b, y_c, u, hx, hm, w_gate, wb16, wo16, layer,
                             norm_ffn[layer][None, :], wr_hi, wr_lo, br)

        hx = hm = _moe(u2, cmb, hn, wg16, wu16, wd16, layer, norm_final[None, :], final=(layer == depth - 1))

    return hx.reshape(batch, seq, d)
```

```python
import functools

import numpy as np
import jax
import jax.numpy as jnp
from jax import lax
from jax.experimental import pallas as pl
from jax.experimental.pallas import tpu as pltpu

F32 = jnp.float32
BF16 = jnp.bfloat16

D_MODEL = 1024
N_META = 16
CONV_W = 4
RMS_EPS = 1e-6
L2_EPS = 1e-6
LB_FLOOR = 1e-30
HEADS = 4
HEAD_W = 128
BRANCH_W = HEADS * HEAD_W
C_DK = 64
C_GK_RANK = 16
C_GK_NORM = 16.0
N_BRANCH = 3
N_GROUPS = 4
EXP_PER_GROUP = 8
N_EXPERTS = N_GROUPS * EXP_PER_GROUP
EXPERT_HIDDEN = 256

R = 256
N_LEVELS = 8
LV_DIAG = N_LEVELS
W_CUM = N_LEVELS
W_SFX = N_LEVELS + 1
LANES = 128

SM_BETA, SM_DECAY, SM_GK = 0, 4, 8

VMEM_LIMIT = 60 * 1024 * 1024
MOE_W = 1280
MOE_CAPS = (320, 384, 448)
CMB_GROUP_LANE = N_EXPERTS


def _cparams(sem):
    return pltpu.CompilerParams(dimension_semantics=sem, vmem_limit_bytes=VMEM_LIMIT)


def _const_spec(shape):
    nd = len(shape)
    return pl.BlockSpec(shape, lambda *_: (0,) * nd)


@functools.lru_cache(maxsize=None)
def _level_constants():
    t = np.arange(R)[:, None]
    s = np.arange(R)[None, :]
    x = np.maximum(t ^ s, 1)
    lv = np.where(s < t, np.floor(np.log2(x)).astype(np.int32), np.where(s == t, LV_DIAG, -1)).astype(np.int32)
    w = np.zeros((N_LEVELS + 2, R, R), np.float32)
    for l in range(N_LEVELS):
        hsz = 1 << l
        for r in range(R):
            hb = (r // hsz) * hsz
            if (r >> l) & 1:
                w[l, r, hb:r + 1] = 1.0
            else:
                w[l, r, r + 1:hb + hsz] = 1.0
    w[W_CUM] = np.tril(np.ones((R, R), np.float32))
    w[W_SFX] = np.triu(np.ones((R, R), np.float32), 1)
    return lv, w


def _dot(a, b):
    return jnp.dot(a, b, preferred_element_type=F32)


def _dot_nt(a, b):
    return lax.dot_general(a, b, (((1,), (1,)), ((), ())), preferred_element_type=F32)


def _dot_tn(a, b):
    return lax.dot_general(a, b, (((0,), (0,)), ((), ())), preferred_element_type=F32)


def _sigmoid(x):
    return 1.0 / (1.0 + jnp.exp(-x))


def _silu(x):
    return x * _sigmoid(x)


def _softplus(x):
    return jnp.maximum(x, 0.0) + jnp.log1p(jnp.exp(-jnp.abs(x)))


def _log_sigmoid(x):
    return -_softplus(-x)


def _rmsnorm(x, w):
    return x * lax.rsqrt(jnp.mean(x * x, axis=-1, keepdims=True) + RMS_EPS) * w


def _valid_rows(is_meta):
    row = lax.broadcasted_iota(jnp.int32, (R, 1), 0)
    first_valid = jnp.where(is_meta, R - N_META, 0)
    return jnp.where(row >= first_valid, 1.0, 0.0).astype(F32)


def _residual_block(hx_ref, hm_ref, is_meta):
    return jnp.where(is_meta, hm_ref[...], hx_ref[...])


def _mixer_input(h, nw_ref, valid):
    return (_rmsnorm(h, nw_ref[...]) * valid).astype(BF16)


PROJ_CHUNK = 256


def _projection_steps(u, w_ref, out_ref):
    n_cols = w_ref.shape[1]

    def step(c0):
        sl = slice(c0, min(c0 + PROJ_CHUNK, n_cols))
        out_ref[:, sl] = _dot(u, w_ref[:, sl])

    return [functools.partial(step, c0) for c0 in range(0, n_cols, PROJ_CHUNK)]


def _run_share(steps, slots_left):
    for _ in range(-(-len(steps) // slots_left)):
        steps.pop(0)()


def _run_all(steps):
    while steps:
        steps.pop(0)()


def _seq_block(n_blocks, lag):
    return lambda i: ((jnp.clip(i - lag, 0, n_blocks - 1) + n_blocks - 1) % n_blocks, 0)


def _state_in(j, nb, st_ref, stm_ref):
    @pl.when(j <= 0)
    def _():
        st_ref[...] = jnp.zeros(st_ref.shape, st_ref.dtype)

    @pl.when((j >= 1) & ((j - 1) % nb == 0))
    def _():
        st_ref[...] = stm_ref[...]


def _state_out(j, st_ref, stm_ref):
    @pl.when(j == 0)
    def _():
        stm_ref[...] = st_ref[...]


def _skewed(i, pa_ref, pb_ref, body):
    @pl.when(i == 0)
    def _():
        pb_ref[...] = jnp.zeros(pb_ref.shape, pb_ref.dtype)

    @pl.when(i % 2 == 0)
    def _():
        body(pb_ref, pa_ref)

    @pl.when(i % 2 == 1)
    def _():
        body(pa_ref, pb_ref)


def _head_out(o, gate, nw):
    return (_rmsnorm(o, nw) * _silu(gate)).astype(BF16)


def _gla_block(q, k, v, g, gate, nw, wl_ref, lv, st_ref, p_ref, y_ref, fill):
    width = q.shape[1]
    per_tile = LANES * HEADS // width
    lane = lax.broadcasted_iota(jnp.int32, (1, LANES), 1)
    own_lanes = [jnp.where(lane // (LANES // per_tile) == j, 1.0, 0.0).astype(BF16) for j in range(per_tile)]

    def tile(a, h):
        t0 = (h // per_tile) * LANES
        return a[:, t0:t0 + LANES]

    def own(a_tile, h):
        return a_tile if per_tile == 1 else a_tile * own_lanes[h % per_tile]

    g16 = g.astype(BF16)
    b = _dot(wl_ref[W_CUM], g16)
    rowi = lax.broadcasted_iota(jnp.int32, (R, 1), 0)
    for lvl in range(N_LEVELS):
        hsz = 1 << lvl
        if hsz >= 8:
            b3 = b.reshape(R // (2 * hsz), 2 * hsz, width)
            b_m = jnp.broadcast_to(b3[:, hsz - 1:hsz, :], b3.shape).reshape(R, width)
            f = jnp.exp(-jnp.abs(b - b_m))
        else:
            f = jnp.exp(_dot(wl_ref[lvl], g16))
        lower = ((rowi >> lvl) & 1) == 1
        z = (jnp.where(lower, q, k) * f).astype(BF16)
        if 4 * hsz >= R:
            for h in range(HEADS):
                zt = tile(z, h)
                for r0 in range(0, R, 2 * hsz):
                    p_ref[h, r0 + hsz:r0 + 2 * hsz, r0:r0 + hsz] = _dot_nt(
                        own(zt[r0 + hsz:r0 + 2 * hsz], h), zt[r0:r0 + hsz])
        else:
            m = lv == lvl
            for h in range(HEADS):
                zt = tile(z, h)
                full = _dot_nt(own(zt, h), zt)
                if lvl == 0:
                    p_ref[h] = jnp.where(m, full, 0.0)
                else:
                    p_ref[h] = jnp.where(m, full, p_ref[h])
        _run_share(fill, N_LEVELS - lvl)
    q16 = q.astype(BF16)
    k16 = k.astype(BF16)
    m = lv == LV_DIAG
    for h in range(HEADS):
        p_ref[h] = jnp.where(m, _dot_nt(own(tile(q16, h), h), tile(k16, h)), p_ref[h])
    sfx = _dot(wl_ref[W_SFX], g16)
    qe = (q * jnp.exp(b)).astype(BF16)
    ke = (k * jnp.exp(sfx)).astype(BF16)
    dec = jnp.exp(b[R - 1:R, :])
    v16 = v.astype(BF16)
    for h in range(HEADS):
        sl = slice(h * HEAD_W, (h + 1) * HEAD_W)
        st = st_ref[h]
        o = _dot(p_ref[h].astype(BF16), v16[:, sl]) + _dot_nt(tile(qe, h), st.astype(BF16))
        st_ref[h] = st * tile(dec, h) + _dot_tn(v16[:, sl], own(tile(ke, h), h))
        y_ref[:, sl] = _head_out(o, gate[:, sl], nw)


def _residual_specs(hx, hm, block_of_step):
    hx_last = hx.shape[0] // R - 1
    hm_last = hm.shape[0] // R - 1
    return [pl.BlockSpec((R, D_MODEL), lambda i: (jnp.minimum(block_of_step(i)[0], hx_last), 0)),
            pl.BlockSpec((R, D_MODEL), lambda i: (hm_last, 0))]


def _mixer_call(kernel_fn, name, stream, stream_specs, n_blocks, operands, operand_specs, proj_cols, scratch,
                emit_u=False):
    y_spec = pl.BlockSpec((R, BRANCH_W), _seq_block(n_blocks, 1))
    y_shape = jax.ShapeDtypeStruct((n_blocks * R, BRANCH_W), BF16)
    u_spec = pl.BlockSpec((R, D_MODEL), _seq_block(n_blocks, 0))
    u_shape = jax.ShapeDtypeStruct((n_blocks * R, D_MODEL), BF16)
    return pl.pallas_call(
        kernel_fn,
        grid=(n_blocks + 1,),
        in_specs=stream_specs + operand_specs,
        out_specs=[y_spec, u_spec] if emit_u else y_spec,
        out_shape=[y_shape, u_shape] if emit_u else y_shape,
        scratch_shapes=scratch + [pltpu.VMEM((R, proj_cols), F32), pltpu.VMEM((R, proj_cols), F32)],
        compiler_params=_cparams(("arbitrary",)),
        name=name,
    )(*stream, *operands)


def _state_scratch():
    return [pltpu.VMEM((HEADS, HEAD_W, HEAD_W), F32), pltpu.VMEM((HEADS, HEAD_W, HEAD_W), F32)]


def _hgrn_kernel(nb, hx_ref, hm_ref, nm_ref, w_ref, lb_ref, nw_ref, wl_ref, lv_ref,
                 y_ref, u_ref, st_ref, stm_ref, p_ref, pa_ref, pb_ref):
    i = pl.program_id(0)
    j = i - 1
    _state_in(j, nb, st_ref, stm_ref)

    def body(rd, wr):
        a_q, f_in, a_i, a_g = [rd[:, c * BRANCH_W:(c + 1) * BRANCH_W] for c in range(4)]
        valid = _valid_rows(j == 0)
        lb_floor = lb_ref[0:1, :]
        one_m_lb = lb_ref[1:2, :]
        q = _silu(a_q) * (HEAD_W ** -0.5)
        e = jnp.exp(-jnp.abs(f_in))
        r = 1.0 / (1.0 + e)
        pos = f_in >= 0.0
        g = jnp.log(lb_floor + one_m_lb * jnp.where(pos, r, e * r))
        k = one_m_lb * jnp.where(pos, e * r, r) * valid
        u = _mixer_input(_residual_block(hx_ref, hm_ref, i == 0), nm_ref, _valid_rows(i == 0))
        u_ref[...] = u
        fill = _projection_steps(u, w_ref, wr)
        _gla_block(q, k, a_i, g, a_g, nw_ref[...], wl_ref, lv_ref[...], st_ref, p_ref, y_ref, fill)
        _run_all(fill)

    _skewed(i, pa_ref, pb_ref, body)
    _state_out(j, st_ref, stm_ref)


def _hgrn(hx, hm, n_blocks, norm_mix, w, lb_rows, norm_w, wl, lv, nb):
    return _mixer_call(
        functools.partial(_hgrn_kernel, nb), "hgrn2",
        [hx, hm], _residual_specs(hx, hm, _seq_block(n_blocks, 0)), n_blocks,
        [norm_mix, w, lb_rows, norm_w, wl, lv],
        [_const_spec((1, D_MODEL)), _window_spec(W_A_COL0, W_A_COLS), _const_spec((8, BRANCH_W)),
         _const_spec((1, HEAD_W)), _const_spec(wl.shape), _const_spec((R, R))],
        W_A_COLS, _state_scratch() + [pltpu.VMEM((HEADS, R, R), F32)], emit_u=True)


def _gla_kernel(nb, u_ref, w_ref, w2_ref, b2_ref, nw_ref, wl_ref, lv_ref,
                y_ref, st_ref, stm_ref, p_ref, pa_ref, pb_ref):
    i = pl.program_id(0)
    j = i - 1
    _state_in(j, nb, st_ref, stm_ref)

    def body(rd, wr):
        qk_w = HEADS * C_DK
        c_q = rd[:, :qk_w]
        c_k = rd[:, qk_w:2 * qk_w]
        c_v = rd[:, 2 * qk_w:2 * qk_w + BRANCH_W]
        c_g = rd[:, 2 * qk_w + BRANCH_W:2 * qk_w + 2 * BRANCH_W]
        gk_low = rd[:, 2 * qk_w + 2 * BRANCH_W:]
        z = _dot(gk_low.astype(BF16), w2_ref[...]) + b2_ref[...]
        g = _log_sigmoid(z) * (1.0 / C_GK_NORM)
        fill = _projection_steps(u_ref[...], w_ref, wr)
        _gla_block(c_q * (C_DK ** -0.5), c_k * _valid_rows(j == 0), c_v, g, c_g, nw_ref[...],
                   wl_ref, lv_ref[...], st_ref, p_ref, y_ref, fill)
        _run_all(fill)

    _skewed(i, pa_ref, pb_ref, body)
    _state_out(j, st_ref, stm_ref)


def _u_spec(n_blocks):
    return [pl.BlockSpec((R, D_MODEL), _seq_block(n_blocks, 0))]


def _gla(u, n_blocks, w, w2, b2, norm_w, wl, lv, nb):
    return _mixer_call(
        functools.partial(_gla_kernel, nb), "gla", [u], _u_spec(n_blocks), n_blocks,
        [w, w2, b2, norm_w, wl, lv],
        [_window_spec(W_C_COL0, W_C_COLS), _const_spec(w2.shape),
         _const_spec(b2.shape), _const_spec((1, HEAD_W)), _const_spec(wl.shape), _const_spec((R, R))],
        W_C_COLS, _state_scratch() + [pltpu.VMEM((HEADS, R, R), F32)])


TAIL = 8


def _gdn_kernel(nb, u_ref, w_ref, cw_ref, hp_ref, nw_ref, wl_ref, lv_ref,
                y_ref, st_ref, stm_ref, xx_ref, tailm_ref, x_ref, a_ref, pa_ref, pb_ref):
    i = pl.program_id(0)
    j = i - 1
    _state_in(j, nb, st_ref, stm_ref)

    @pl.when(j <= 0)
    def _():
        xx_ref[0:TAIL, :] = jnp.zeros((TAIL, 3 * BRANCH_W), F32)

    @pl.when((j >= 1) & ((j - 1) % nb == 0))
    def _():
        xx_ref[0:TAIL, :] = tailm_ref[...]

    def body(rd, wr):
        _gdn_body(j, rd, wr, u_ref, w_ref, cw_ref, hp_ref, nw_ref, wl_ref, lv_ref,
                  y_ref, st_ref, xx_ref, x_ref, a_ref)

    _skewed(i, pa_ref, pb_ref, body)

    @pl.when(j == 0)
    def _():
        tailm_ref[...] = xx_ref[0:TAIL, :]

    _state_out(j, st_ref, stm_ref)


def _gdn_body(j, rd, wr, u_ref, w_ref, cw_ref, hp_ref, nw_ref, wl_ref, lv_ref,
              y_ref, st_ref, xx_ref, x_ref, a_ref):
    fill = _projection_steps(u_ref[...], w_ref, wr)
    valid = _valid_rows(j == 0)
    lv = lv_ref[...]
    xx_ref[TAIL:, :] = rd[:, :3 * BRANCH_W]
    b_g = rd[:, 3 * BRANCH_W:4 * BRANCH_W]
    sm = rd[:, 4 * BRANCH_W:]
    conv = jnp.zeros((R, 3 * BRANCH_W), F32)
    for tap in range(CONV_W):
        off = TAIL - (CONV_W - 1) + tap
        conv = conv + xx_ref[off:off + R, :] * cw_ref[tap:tap + 1, :]
    qkv = _silu(conv)
    xx_ref[0:TAIL, :] = xx_ref[R:R + TAIL, :]

    a_neg = hp_ref[0:1, :]
    dt_b = hp_ref[1:2, :]
    log_a = a_neg * _softplus(sm + dt_b)
    la_hi = log_a.astype(BF16)
    la_lo = (log_a - la_hi.astype(F32)).astype(BF16)
    gcum = _dot(wl_ref[W_CUM], la_hi) + _dot(wl_ref[W_CUM], la_lo)
    gcum_t = gcum.T
    beta_all = _sigmoid(sm) * valid

    strict = (lv >= 0) & (lv < LV_DIAG)
    causal = lv >= 0
    eye = (lv == LV_DIAG).astype(F32)

    heads = []
    for h in range(HEADS):
        qh = qkv[:, h * HEAD_W:(h + 1) * HEAD_W]
        kh = qkv[:, BRANCH_W + h * HEAD_W:BRANCH_W + (h + 1) * HEAD_W] * valid
        vh = qkv[:, 2 * BRANCH_W + h * HEAD_W:2 * BRANCH_W + (h + 1) * HEAD_W]
        qh = qh * lax.rsqrt(jnp.sum(qh * qh, axis=-1, keepdims=True) + L2_EPS) * (HEAD_W ** -0.5)
        kh = kh * lax.rsqrt(jnp.sum(kh * kh, axis=-1, keepdims=True) + L2_EPS)
        beta = beta_all[:, SM_BETA + h:SM_BETA + h + 1]
        gcol = gcum[:, SM_DECAY + h:SM_DECAY + h + 1]
        grow = gcum_t[SM_DECAY + h:SM_DECAY + h + 1, :]
        dm = jnp.exp(jnp.minimum(gcol - grow, 0.0))
        q16 = qh.astype(BF16)
        k16 = kh.astype(BF16)
        a_ref[h] = jnp.where(strict, beta * _dot_nt(k16, k16) * dm, 0.0)
        qk = jnp.where(causal, _dot_nt(q16, k16) * dm, 0.0).astype(BF16)
        heads.append((qh, kh, vh, beta, gcol, qk))
        _run_share(fill, HEADS + 1 - h)

    for lvl in range(N_LEVELS):
        m = lv == lvl
        for h in range(HEADS):
            if 2 << lvl == R:
                hs = R // 2
                x_lo = x_ref[h, hs:, hs:].astype(BF16)
                x_up = x_ref[h, :hs, :hs].astype(BF16)
                x_ref[h, hs:, :hs] = -_dot(_dot(x_lo, a_ref[h, hs:, :hs].astype(BF16)).astype(BF16), x_up)
                continue
            l_lvl = jnp.where(m, a_ref[h], 0.0)
            if lvl == 0:
                x_ref[h] = eye - l_lvl
            else:
                xcur = x_ref[h].astype(BF16)
                x_ref[h] = x_ref[h] - _dot(_dot(xcur, l_lvl.astype(BF16)).astype(BF16), xcur)

    for h, (qh, kh, vh, beta, gcol, qk) in enumerate(heads):
        sl = slice(h * HEAD_W, (h + 1) * HEAD_W)
        eg = jnp.exp(gcol)
        rhs = jnp.concatenate([vh * beta, kh * (beta * eg)], axis=-1).astype(BF16)
        sol = _dot(x_ref[h].astype(BF16), rhs)
        u_h = sol[:, :HEAD_W]
        w_h = sol[:, HEAD_W:]
        st = st_ref[h]
        st16 = st.astype(BF16)
        v_new = u_h - _dot_nt(w_h.astype(BF16), st16)
        vn16 = v_new.astype(BF16)
        o = _dot_nt((qh * eg).astype(BF16), st16) + _dot(qk, vn16)
        g_last = gcol[R - 1:R, :]
        ke = (kh * jnp.exp(g_last - gcol)).astype(BF16)
        st_ref[h] = st * jnp.exp(g_last) + _dot_tn(vn16, ke)
        y_ref[:, sl] = _head_out(o, b_g[:, sl], nw_ref[...])
    _run_all(fill)


def _gdn(u, n_blocks, w, conv_w, head_params, norm_w, wl, lv, nb):
    return _mixer_call(
        functools.partial(_gdn_kernel, nb), "gdn", [u], _u_spec(n_blocks), n_blocks,
        [w, conv_w, head_params, norm_w, wl, lv],
        [_window_spec(W_B_COL0, W_B_COLS), _const_spec((CONV_W, 3 * BRANCH_W)),
         _const_spec((8, LANES)), _const_spec((1, HEAD_W)), _const_spec(wl.shape), _const_spec((R, R))],
        W_B_COLS, _state_scratch() + [pltpu.VMEM((R + TAIL, 3 * BRANCH_W), F32),
                            pltpu.VMEM((TAIL, 3 * BRANCH_W), F32),
                            pltpu.VMEM((HEADS, R, R), F32),
                            pltpu.VMEM((HEADS, R, R), F32)])


def _split_hi_lo(x):
    hi = x.astype(BF16)
    return hi, (x - hi.astype(F32)).astype(BF16)


def _merge_kernel(ya_ref, yb_ref, yc_ref, u_ref, hx_ref, hm_ref, wgate_ref, wb_ref, wo_ref, nf_ref,
                  wr_hi_ref, wr_lo_ref, br_ref, hn_ref, u2_ref, cmb_ref, lg_ref):
    i = pl.program_id(0)
    last = pl.num_programs(0) - 2

    @pl.when(i == 0)
    def _():
        lg_ref[...] = jnp.zeros(lg_ref.shape, lg_ref.dtype)

    cmb_ref[...] = _route(lg_ref[...])

    is_meta = i >= last
    h = _residual_block(hx_ref, hm_ref, is_meta)
    u = u_ref[...]
    merged = jnp.zeros((R, D_MODEL), F32)
    for n, y_ref in enumerate((ya_ref, yb_ref, yc_ref)):
        gate = _sigmoid(_dot(u, wgate_ref[:, n * D_MODEL:(n + 1) * D_MODEL]))
        merged = merged + gate * _dot(y_ref[...], wb_ref[0, n])
    hn = h + _dot(merged.astype(BF16), wo_ref[0])
    hn_ref[...] = hn
    u2 = _rmsnorm(hn, nf_ref[...])
    u2_ref[...] = u2.astype(BF16)

    u_hi, u_lo = _split_hi_lo(u2)
    lg_ref[...] = (_dot(u_hi, wr_hi_ref[...]) + _dot(u_hi, wr_lo_ref[...]) + _dot(u_lo, wr_hi_ref[...])
                   + br_ref[...])


def _route(logits):
    lane_i = lax.broadcasted_iota(jnp.int32, (R, LANES), 1)
    lane = lane_i.astype(F32)
    lane_grp = (lane_i // EXP_PER_GROUP).astype(F32)
    neg = jnp.float32(-jnp.inf)
    big = jnp.float32(1e9)
    is_g = (lane_i >= N_EXPERTS) & (lane_i < N_EXPERTS + N_GROUPS)
    lg = jnp.where(is_g, logits, neg)
    mg = jnp.max(lg, axis=-1, keepdims=True)
    zg = jnp.sum(jnp.exp(lg - mg), axis=-1, keepdims=True)
    g_val = 1.0 / zg
    g_idx = jnp.min(jnp.where(lg == mg, lane, big), axis=-1, keepdims=True) - N_EXPERTS
    in_grp = (lane_i < N_EXPERTS) & (lane_grp == g_idx)
    le = jnp.where(in_grp, logits, neg)
    m1 = jnp.max(le, axis=-1, keepdims=True)
    ze = jnp.sum(jnp.exp(le - m1), axis=-1, keepdims=True)
    i1 = jnp.min(jnp.where(le == m1, lane, big), axis=-1, keepdims=True)
    le2 = jnp.where(lane == i1, neg, le)
    m2 = jnp.max(le2, axis=-1, keepdims=True)
    i2 = jnp.min(jnp.where(le2 == m2, lane, big), axis=-1, keepdims=True)
    p1 = 1.0 / ze
    p2 = jnp.exp(m2 - m1) / ze
    den = p1 + p2
    cmb = g_val * jnp.where(lane == i1, p1 / den, jnp.where(lane == i2, p2 / den, 0.0))
    return jnp.where(lane_i == CMB_GROUP_LANE, g_idx, cmb)


def _merge(ya, yb, yc, u, hx, hm, w_gate, wb, wo, layer, nf, wr_hi, wr_lo, br):
    t = ya.shape[0]
    last = t // R - 1
    cur = lambda i: (jnp.minimum(i, last), 0)
    row = lambda w: pl.BlockSpec((R, w), cur)
    return pl.pallas_call(
        _merge_kernel,
        grid=(t // R + 1,),
        in_specs=[row(BRANCH_W), row(BRANCH_W), row(BRANCH_W), row(D_MODEL)]
        + _residual_specs(hx, hm, cur)
        + [_window_spec(W_G_COL0, W_G_COLS),
                  pl.BlockSpec((1, N_BRANCH, BRANCH_W, D_MODEL), lambda i: (layer, 0, 0, 0)),
                  pl.BlockSpec((1, D_MODEL, D_MODEL), lambda i: (layer, 0, 0)),
                  _const_spec((1, D_MODEL)), _const_spec((D_MODEL, LANES)), _const_spec((D_MODEL, LANES)),
                  _const_spec((1, LANES))],
        out_specs=[row(D_MODEL), row(D_MODEL),
                   pl.BlockSpec((R, LANES), lambda i: (jnp.maximum(i - 1, 0), 0))],
        out_shape=[jax.ShapeDtypeStruct((t, D_MODEL), F32),
                   jax.ShapeDtypeStruct((t, D_MODEL), BF16),
                   jax.ShapeDtypeStruct((t, LANES), F32)],
        scratch_shapes=[pltpu.VMEM((R, LANES), F32)],
        compiler_params=_cparams(("arbitrary",)),
        name="merge_router",
    )(ya, yb, yc, u, hx, hm, w_gate, wb, wo, nf, wr_hi, wr_lo, br)


def _moe_kernel(final, cnt_ref, u_ref, c_ref, h_ref, wg_ref, wu_ref, wd_ref, tri_ref, nfin_ref, o_ref):
    w = pl.program_id(0)
    g = pl.program_id(1)
    u = u_ref[...]
    cmb = c_ref[...]
    gid_row = cmb.T[CMB_GROUP_LANE:CMB_GROUP_LANE + 1, :]
    in_row = jnp.where(gid_row == g.astype(F32), 1.0, 0.0)
    ranks = []
    before = jnp.zeros((1, 1), F32)
    for j in range(MOE_W // R):
        seg = in_row[:, j * R:(j + 1) * R]
        ranks.append(_dot(jnp.broadcast_to(seg, (16, R)).astype(BF16), tri_ref[...])[0:1, :] + before)
        before = before + jnp.sum(seg, axis=-1, keepdims=True)
    rank_row = jnp.concatenate(ranks, axis=-1)
    cmb_hi, cmb_lo = _split_hi_lo(cmb)

    @pl.when(g == 0)
    def _():
        o_ref[...] = h_ref[...]

    count = cnt_ref[w * N_GROUPS + g]
    n_pass = (count + MOE_CAPS[-1] - 1) // MOE_CAPS[-1]
    per_pass = (count + jnp.maximum(n_pass, 1) - 1) // jnp.maximum(n_pass, 1)

    def run(cap):
        slot = lax.broadcasted_iota(jnp.int32, (cap, 1), 0).astype(F32)
        lane = lax.broadcasted_iota(jnp.int32, (cap, LANES), 1)

        def one_pass(c, carry):
            base = (c * cap).astype(F32)
            sel = jnp.where((in_row > 0.0) & (rank_row - base == slot), 1.0, 0.0).astype(BF16)
            x = _dot(sel, u).astype(BF16)
            cw = _dot(sel, cmb_hi) + _dot(sel, cmb_lo)
            y = jnp.zeros((cap, D_MODEL), F32)
            for e in range(EXP_PER_GROUP):
                ce = jnp.sum(jnp.where(lane == g * EXP_PER_GROUP + e, cw, 0.0), axis=-1, keepdims=True)
                hid = _silu(_dot(x, wg_ref[0, 0, e])) * _dot(x, wu_ref[0, 0, e]) * ce
                y = y + _dot(hid.astype(BF16), wd_ref[0, 0, e])
            o_ref[...] += _dot_tn(sel, y.astype(BF16))
            return carry

        lax.fori_loop(0, n_pass, one_pass, 0)

    below = 0
    for cap in MOE_CAPS:
        pl.when((per_pass > below) & (per_pass <= cap))(functools.partial(run, cap))
        below = cap

    if final:
        @pl.when(g == N_GROUPS - 1)
        def _():
            o_ref[...] = _rmsnorm(o_ref[...], nfin_ref[...])


def _moe(u2, cmb, h, wg, wu, wd, layer, norm_final, final):
    t = h.shape[0]
    n_win = t // MOE_W
    assert t % MOE_W == 0
    gid = cmb[:, CMB_GROUP_LANE].astype(jnp.int32).reshape(n_win, MOE_W)
    counts = jnp.sum(gid[:, :, None] == jnp.arange(N_GROUPS, dtype=jnp.int32), axis=1, dtype=jnp.int32)
    tri = jnp.asarray(np.triu(np.ones((R, R), np.float32), 1), dtype=BF16)
    row = lambda width: pl.BlockSpec((MOE_W, width), lambda w, g, cnt: (w, 0))
    wspec = lambda a, b: pl.BlockSpec((1, 1, EXP_PER_GROUP, a, b), lambda w, g, cnt: (layer, g, 0, 0, 0))
    return pl.pallas_call(
        functools.partial(_moe_kernel, final),
        grid_spec=pltpu.PrefetchScalarGridSpec(
            num_scalar_prefetch=1,
            grid=(n_win, N_GROUPS),
            in_specs=[row(D_MODEL), row(LANES),
                      row(D_MODEL),
                      wspec(D_MODEL, EXPERT_HIDDEN), wspec(D_MODEL, EXPERT_HIDDEN),
                      wspec(EXPERT_HIDDEN, D_MODEL),
                      pl.BlockSpec((R, R), lambda w, g, cnt: (0, 0)),
                      pl.BlockSpec((1, D_MODEL), lambda w, g, cnt: (0, 0))],
            out_specs=row(D_MODEL)),
        out_shape=jax.ShapeDtypeStruct((t - R if final else t, D_MODEL), F32),
        compiler_params=_cparams(("arbitrary", "arbitrary")),
        name="moe",
    )(counts.reshape(-1), u2, cmb, h, wg, wu, wd, tri, norm_final)


IN_SPLITS = (512, 512, 512, 512, 1536, 4, 4, 512, 256, 256, 512, 16, 512, 3072)


IN_OFFS = tuple(int(v) for v in np.cumsum((0,) + IN_SPLITS))
(O_AQ, O_AF, O_AI, O_AG, O_BQKV, O_BBETA, O_BDECAY, O_BG, O_CQ, O_CK, O_CV, O_CGK, O_CG, O_GATES, O_END) = IN_OFFS
W_A_COLS = O_BQKV - O_AQ
W_B_COLS = (O_BBETA - O_BQKV) + (O_CQ - O_BG) + LANES
W_C_COLS = (O_CGK - O_CQ) + (O_GATES - O_CG) + LANES
W_G_COLS = O_END - O_GATES
W_A_COL0 = 0
W_B_COL0 = W_A_COL0 + W_A_COLS
W_C_COL0 = W_B_COL0 + W_B_COLS
W_G_COL0 = W_C_COL0 + W_C_COLS
W_ALL_COLS = W_G_COL0 + W_G_COLS
assert O_BBETA % LANES == SM_BETA and O_BDECAY - O_BBETA == SM_DECAY and O_CGK % LANES == SM_GK


@functools.lru_cache(maxsize=None)
def _wprep_table():
    rows = []

    def run(src0, n_cols):
        rows.extend((src0 + c, 0, LANES) for c in range(0, n_cols, LANES))

    run(O_AQ, W_A_COLS)
    run(O_BQKV, O_BBETA - O_BQKV)
    run(O_BG, O_CQ - O_BG)
    rows.append((O_BBETA - SM_BETA, SM_BETA, SM_GK))
    run(O_CQ, O_CGK - O_CQ)
    run(O_CG, O_GATES - O_CG)
    rows.append((O_CGK - SM_GK, SM_GK, SM_GK + C_GK_RANK))
    run(O_GATES, W_G_COLS)
    assert len(rows) * LANES == W_ALL_COLS
    rows.extend([(0, 0, 0)] * (-len(rows) % WPREP_TILES))
    table = np.asarray(rows, np.int32)
    assert (table[:, 0] % 8 == 0).all()
    return table


WPREP_TILES = 4


def _wprep_kernel(tab_ref, *refs):
    o_ref = refs[-1]
    j = pl.program_id(0)
    for k, wt_ref in enumerate(refs[:-1]):
        tile = WPREP_TILES * j + k
        t = wt_ref[0].T
        lane = lax.broadcasted_iota(jnp.int32, t.shape, 1)
        keep = (lane >= tab_ref[3 * tile + 1]) & (lane < tab_ref[3 * tile + 2])
        o_ref[:, k * LANES:(k + 1) * LANES] = jnp.where(keep, t, 0.0).astype(BF16)


def _inproj_weights(w_in_t, layer):
    table = _wprep_table()
    n_steps = table.shape[0] // WPREP_TILES

    def source_tile(k):
        return pl.BlockSpec(
            (pl.Element(1), pl.Element(LANES), pl.Element(D_MODEL)),
            lambda j, tab: (layer, pl.multiple_of(tab[3 * (WPREP_TILES * j + k)], 8), 0))

    return pl.pallas_call(
        _wprep_kernel,
        grid_spec=pltpu.PrefetchScalarGridSpec(
            num_scalar_prefetch=1,
            grid=(n_steps,),
            in_specs=[source_tile(k) for k in range(WPREP_TILES)],
            out_specs=pl.BlockSpec((D_MODEL, WPREP_TILES * LANES), lambda j, tab: (0, j))),
        out_shape=jax.ShapeDtypeStruct((D_MODEL, table.shape[0] * LANES), BF16),
        compiler_params=_cparams(("arbitrary",)),
        name="wprep",
    )(jnp.asarray(table.reshape(-1)), *([w_in_t] * WPREP_TILES))


def _window_spec(col0, width):
    return pl.BlockSpec((pl.Element(D_MODEL), pl.Element(width)), lambda *_: (0, col0))


def kernel(x, meta_tokens, norm_mix, w_in, hgrn_lb_logits, hgrn_norm, gdn_conv, gdn_a_log,
           gdn_dt_bias, gdn_norm, gla_gk_w2, gla_gk_b, gla_norm, w_branch, w_out, norm_ffn,
           router_group_w, router_group_b, router_expert_w, router_expert_b,
           expert_w_gate, expert_w_up, expert_w_down, norm_final):
    batch, seq, d = x.shape
    assert d == D_MODEL and seq % R == 0
    depth = w_in.shape[0]
    nb = seq // R
    lv_np, wl_np = _level_constants()
    lv = jnp.asarray(lv_np)
    wl = jnp.asarray(wl_np, dtype=BF16)

    n_blocks = batch * nb + 1
    hx = x.reshape(batch * seq, d)
    hm = jnp.concatenate([jnp.zeros((R - N_META, d), F32), meta_tokens.astype(F32)], axis=0)

    w_in_t = jnp.swapaxes(w_in, 1, 2)
    wg16, wu16, wd16, wb16, wo16 = (w.astype(BF16) for w in (
        expert_w_gate, expert_w_up, expert_w_down, w_branch, w_out))

    lb_p = jax.nn.softmax(hgrn_lb_logits.astype(F32), axis=0)
    lb_all = jnp.maximum(jnp.cumsum(lb_p, axis=0) - lb_p[0:1], 0.0)

    for layer in range(depth):
        w_a = w_b = w_c = w_gate = _inproj_weights(w_in_t, layer)
        nm = norm_mix[layer][None, :]

        lb = lb_all[layer]
        lb_rows = jnp.zeros((8, BRANCH_W), F32)
        lb_rows = lb_rows.at[0].set(jnp.maximum(lb, LB_FLOOR)).at[1].set(1.0 - lb)
        y_a, u = _hgrn(hx, hm, n_blocks, nm, w_a, lb_rows, hgrn_norm[layer][None, :], wl, lv, nb)

        head_params = jnp.zeros((8, LANES), F32)
        head_params = head_params.at[0, SM_DECAY:SM_DECAY + HEADS].set(-jnp.exp(gdn_a_log[layer]))
        head_params = head_params.at[1, SM_DECAY:SM_DECAY + HEADS].set(gdn_dt_bias[layer])
        y_b = _gdn(u, n_blocks, w_b, gdn_conv[layer], head_params, gdn_norm[layer][None, :], wl, lv, nb)

        w2 = jnp.zeros((LANES, HEADS * C_DK), F32).at[SM_GK:SM_GK + C_GK_RANK].set(gla_gk_w2[layer])
        y_c = _gla(u, n_blocks, w_c, w2.astype(BF16), gla_gk_b[layer][None, :],
                   gla_norm[layer][None, :], wl, lv, nb)

        wr = jnp.zeros((D_MODEL, LANES), F32)
        wr = wr.at[:, :N_EXPERTS].set(router_expert_w[layer])
        wr = wr.at[:, N_EXPERTS:N_EXPERTS + N_GROUPS].set(router_group_w[layer])
        br = jnp.zeros((1, LANES), F32)
        br = br.at[0, :N_EXPERTS].set(router_expert_b[layer])
        br = br.at[0, N_EXPERTS:N_EXPERTS + N_GROUPS].set(router_group_b[layer])
        wr_hi = wr.astype(BF16)
        wr_lo = (wr - wr_hi.astype(F32)).astype(BF16)
        hn, u2, cmb = _merge(y_a, y_b, y_c, u, hx, hm, w_gate, wb16, wo16, layer,
                             norm_ffn[layer][None, :], wr_hi, wr_lo, br)

        hx = hm = _moe(u2, cmb, hn, wg16, wu16, wd16, layer, norm_final[None, :], final=(layer == depth - 1))

    return hx.reshape(batch, seq, d)
```

```python
import functools

import numpy as np
import jax
import jax.numpy as jnp
from jax import lax
from jax.experimental import pallas as pl
from jax.experimental.pallas import tpu as pltpu

F32 = jnp.float32
BF16 = jnp.bfloat16

D_MODEL = 1024
N_META = 16
CONV_W = 4
RMS_EPS = 1e-6
L2_EPS = 1e-6
LB_FLOOR = 1e-30
HEADS = 4
HEAD_W = 128
BRANCH_W = HEADS * HEAD_W
C_DK = 64
C_GK_RANK = 16
C_GK_NORM = 16.0
N_BRANCH = 3
N_GROUPS = 4
EXP_PER_GROUP = 8
N_EXPERTS = N_GROUPS * EXP_PER_GROUP
EXPERT_HIDDEN = 256

R = 256
N_LEVELS = 8
LV_DIAG = N_LEVELS
W_CUM = N_LEVELS
W_SFX = N_LEVELS + 1
LANES = 128

SM_BETA, SM_DECAY, SM_GK = 0, 4, 8

VMEM_LIMIT = 60 * 1024 * 1024
MOE_W = 1280
MOE_CAPS = (320, 384, 448)
CMB_GROUP_LANE = N_EXPERTS


def _cparams(sem):
    return pltpu.CompilerParams(dimension_semantics=sem, vmem_limit_bytes=VMEM_LIMIT)


def _const_spec(shape):
    nd = len(shape)
    return pl.BlockSpec(shape, lambda *_: (0,) * nd)


@functools.lru_cache(maxsize=None)
def _level_constants():
    t = np.arange(R)[:, None]
    s = np.arange(R)[None, :]
    x = np.maximum(t ^ s, 1)
    lv = np.where(s < t, np.floor(np.log2(x)).astype(np.int32), np.where(s == t, LV_DIAG, -1)).astype(np.int32)
    w = np.zeros((N_LEVELS + 2, R, R), np.float32)
    for l in range(N_LEVELS):
        hsz = 1 << l
        for r in range(R):
            hb = (r // hsz) * hsz
            if (r >> l) & 1:
                w[l, r, hb:r + 1] = 1.0
            else:
                w[l, r, r + 1:hb + hsz] = 1.0
    w[W_CUM] = np.tril(np.ones((R, R), np.float32))
    w[W_SFX] = np.triu(np.ones((R, R), np.float32), 1)
    return lv, w


def _dot(a, b):
    return jnp.dot(a, b, preferred_element_type=F32)


def _dot_nt(a, b):
    return lax.dot_general(a, b, (((1,), (1,)), ((), ())), preferred_element_type=F32)


def _dot_tn(a, b):
    return lax.dot_general(a, b, (((0,), (0,)), ((), ())), preferred_element_type=F32)


def _sigmoid(x):
    return 1.0 / (1.0 + jnp.exp(-x))


def _silu(x):
    return x * _sigmoid(x)


def _softplus(x):
    return jnp.maximum(x, 0.0) + jnp.log1p(jnp.exp(-jnp.abs(x)))


def _log_sigmoid(x):
    return -_softplus(-x)


def _rmsnorm(x, w):
    return x * lax.rsqrt(jnp.mean(x * x, axis=-1, keepdims=True) + RMS_EPS) * w


def _valid_rows(is_meta):
    row = lax.broadcasted_iota(jnp.int32, (R, 1), 0)
    first_valid = jnp.where(is_meta, R - N_META, 0)
    return jnp.where(row >= first_valid, 1.0, 0.0).astype(F32)


def _residual_block(hx_ref, hm_ref, is_meta):
    return jnp.where(is_meta, hm_ref[...], hx_ref[...])


def _mixer_input(h, nw_ref, valid):
    return (_rmsnorm(h, nw_ref[...]) * valid).astype(BF16)


PROJ_CHUNK = 256


def _projection_steps(u, w_ref, out_ref):
    n_cols = w_ref.shape[1]

    def step(c0):
        sl = slice(c0, min(c0 + PROJ_CHUNK, n_cols))
        out_ref[:, sl] = _dot(u, w_ref[:, sl])

    return [functools.partial(step, c0) for c0 in range(0, n_cols, PROJ_CHUNK)]


def _run_share(steps, slots_left):
    for _ in range(-(-len(steps) // slots_left)):
        steps.pop(0)()


def _run_all(steps):
    while steps:
        steps.pop(0)()


def _seq_block(n_blocks, lag):
    return lambda i: ((jnp.clip(i - lag, 0, n_blocks - 1) + n_blocks - 1) % n_blocks, 0)


def _state_in(j, nb, st_ref, stm_ref):
    @pl.when(j <= 0)
    def _():
        st_ref[...] = jnp.zeros(st_ref.shape, st_ref.dtype)

    @pl.when((j >= 1) & ((j - 1) % nb == 0))
    def _():
        st_ref[...] = stm_ref[...]


def _state_out(j, st_ref, stm_ref):
    @pl.when(j == 0)
    def _():
        stm_ref[...] = st_ref[...]


def _skewed(i, pa_ref, pb_ref, body):
    @pl.when(i == 0)
    def _():
        pb_ref[...] = jnp.zeros(pb_ref.shape, pb_ref.dtype)

    @pl.when(i % 2 == 0)
    def _():
        body(pb_ref, pa_ref)

    @pl.when(i % 2 == 1)
    def _():
        body(pa_ref, pb_ref)


def _head_out(o, gate, nw):
    return (_rmsnorm(o, nw) * _silu(gate)).astype(BF16)


def _gla_block(q, k, v, g, gate, nw, wl_ref, lv, st_ref, p_ref, y_ref, fill):
    width = q.shape[1]
    per_tile = LANES * HEADS // width
    lane = lax.broadcasted_iota(jnp.int32, (1, LANES), 1)
    own_lanes = [jnp.where(lane // (LANES // per_tile) == j, 1.0, 0.0).astype(BF16) for j in range(per_tile)]

    def tile(a, h):
        t0 = (h // per_tile) * LANES
        return a[:, t0:t0 + LANES]

    def own(a_tile, h):
        return a_tile if per_tile == 1 else a_tile * own_lanes[h % per_tile]

    g16 = g.astype(BF16)
    b = _dot(wl_ref[W_CUM], g16)
    rowi = lax.broadcasted_iota(jnp.int32, (R, 1), 0)
    for lvl in range(N_LEVELS):
        hsz = 1 << lvl
        if hsz >= 8:
            b3 = b.reshape(R // (2 * hsz), 2 * hsz, width)
            b_m = jnp.broadcast_to(b3[:, hsz - 1:hsz, :], b3.shape).reshape(R, width)
            f = jnp.exp(-jnp.abs(b - b_m))
        else:
            f = jnp.exp(_dot(wl_ref[lvl], g16))
        lower = ((rowi >> lvl) & 1) == 1
        z = (jnp.where(lower, q, k) * f).astype(BF16)
        if 4 * hsz >= R:
            for h in range(HEADS):
                zt = tile(z, h)
                for r0 in range(0, R, 2 * hsz):
                    p_ref[h, r0 + hsz:r0 + 2 * hsz, r0:r0 + hsz] = _dot_nt(
                        own(zt[r0 + hsz:r0 + 2 * hsz], h), zt[r0:r0 + hsz])
        else:
            m = lv == lvl
            for h in range(HEADS):
                zt = tile(z, h)
                full = _dot_nt(own(zt, h), zt)
                if lvl == 0:
                    p_ref[h] = jnp.where(m, full, 0.0)
                else:
                    p_ref[h] = jnp.where(m, full, p_ref[h])
        _run_share(fill, N_LEVELS - lvl)
    q16 = q.astype(BF16)
    k16 = k.astype(BF16)
    m = lv == LV_DIAG
    for h in range(HEADS):
        p_ref[h] = jnp.where(m, _dot_nt(own(tile(q16, h), h), tile(k16, h)), p_ref[h])
    sfx = _dot(wl_ref[W_SFX], g16)
    qe = (q * jnp.exp(b)).astype(BF16)
    ke = (k * jnp.exp(sfx)).astype(BF16)
    dec = jnp.exp(b[R - 1:R, :])
    v16 = v.astype(BF16)
    for h in range(HEADS):
        sl = slice(h * HEAD_W, (h + 1) * HEAD_W)
        st = st_ref[h]
        o = _dot(p_ref[h].astype(BF16), v16[:, sl]) + _dot_nt(tile(qe, h), st.astype(BF16))
        st_ref[h] = st * tile(dec, h) + _dot_tn(v16[:, sl], own(tile(ke, h), h))
        y_ref[:, sl] = _head_out(o, gate[:, sl], nw)


def _residual_specs(hx, hm, block_of_step):
    hx_last = hx.shape[0] // R - 1
    hm_last = hm.shape[0] // R - 1
    return [pl.BlockSpec((R, D_MODEL), lambda i: (jnp.minimum(block_of_step(i)[0], hx_last), 0)),
            pl.BlockSpec((R, D_MODEL), lambda i: (hm_last, 0))]


def _mixer_call(kernel_fn, name, stream, stream_specs, n_blocks, operands, operand_specs, proj_cols, scratch,
                emit_u=False):
    y_spec = pl.BlockSpec((R, BRANCH_W), _seq_block(n_blocks, 1))
    y_shape = jax.ShapeDtypeStruct((n_blocks * R, BRANCH_W), BF16)
    u_spec = pl.BlockSpec((R, D_MODEL), _seq_block(n_blocks, 0))
    u_shape = jax.ShapeDtypeStruct((n_blocks * R, D_MODEL), BF16)
    return pl.pallas_call(
        kernel_fn,
        grid=(n_blocks + 1,),
        in_specs=stream_specs + operand_specs,
        out_specs=[y_spec, u_spec] if emit_u else y_spec,
        out_shape=[y_shape, u_shape] if emit_u else y_shape,
        scratch_shapes=scratch + [pltpu.VMEM((R, proj_cols), F32), pltpu.VMEM((R, proj_cols), F32)],
        compiler_params=_cparams(("arbitrary",)),
        name=name,
    )(*stream, *operands)


def _state_scratch():
    return [pltpu.VMEM((HEADS, HEAD_W, HEAD_W), F32), pltpu.VMEM((HEADS, HEAD_W, HEAD_W), F32)]


def _hgrn_kernel(nb, hx_ref, hm_ref, nm_ref, w_ref, lb_ref, nw_ref, wl_ref, lv_ref,
                 y_ref, u_ref, st_ref, stm_ref, p_ref, pa_ref, pb_ref):
    i = pl.program_id(0)
    j = i - 1
    _state_in(j, nb, st_ref, stm_ref)

    def body(rd, wr):
        a_q, f_in, a_i, a_g = [rd[:, c * BRANCH_W:(c + 1) * BRANCH_W] for c in range(4)]
        valid = _valid_rows(j == 0)
        lb_floor = lb_ref[0:1, :]
        one_m_lb = lb_ref[1:2, :]
        q = _silu(a_q) * (HEAD_W ** -0.5)
        e = jnp.exp(-jnp.abs(f_in))
        r = 1.0 / (1.0 + e)
        pos = f_in >= 0.0
        g = jnp.log(lb_floor + one_m_lb * jnp.where(pos, r, e * r))
        k = one_m_lb * jnp.where(pos, e * r, r) * valid
        u = _mixer_input(_residual_block(hx_ref, hm_ref, i == 0), nm_ref, _valid_rows(i == 0))
        u_ref[...] = u
        fill = _projection_steps(u, w_ref, wr)
        _gla_block(q, k, a_i, g, a_g, nw_ref[...], wl_ref, lv_ref[...], st_ref, p_ref, y_ref, fill)
        _run_all(fill)

    _skewed(i, pa_ref, pb_ref, body)
    _state_out(j, st_ref, stm_ref)


def _hgrn(hx, hm, n_blocks, norm_mix, w, lb_rows, norm_w, wl, lv, nb):
    return _mixer_call(
        functools.partial(_hgrn_kernel, nb), "hgrn2",
        [hx, hm], _residual_specs(hx, hm, _seq_block(n_blocks, 0)), n_blocks,
        [norm_mix, w, lb_rows, norm_w, wl, lv],
        [_const_spec((1, D_MODEL)), _window_spec(W_A_COL0, W_A_COLS), _const_spec((8, BRANCH_W)),
         _const_spec((1, HEAD_W)), _const_spec(wl.shape), _const_spec((R, R))],
        W_A_COLS, _state_scratch() + [pltpu.VMEM((HEADS, R, R), F32)], emit_u=True)


def _gla_kernel(nb, u_ref, w_ref, w2_ref, b2_ref, nw_ref, wl_ref, lv_ref,
                y_ref, st_ref, stm_ref, p_ref, pa_ref, pb_ref):
    i = pl.program_id(0)
    j = i - 1
    _state_in(j, nb, st_ref, stm_ref)

    def body(rd, wr):
        qk_w = HEADS * C_DK
        c_q = rd[:, :qk_w]
        c_k = rd[:, qk_w:2 * qk_w]
        c_v = rd[:, 2 * qk_w:2 * qk_w + BRANCH_W]
        c_g = rd[:, 2 * qk_w + BRANCH_W:2 * qk_w + 2 * BRANCH_W]
        gk_low = rd[:, 2 * qk_w + 2 * BRANCH_W:]
        z = _dot(gk_low.astype(BF16), w2_ref[...]) + b2_ref[...]
        g = _log_sigmoid(z) * (1.0 / C_GK_NORM)
        fill = _projection_steps(u_ref[...], w_ref, wr)
        _gla_block(c_q * (C_DK ** -0.5), c_k * _valid_rows(j == 0), c_v, g, c_g, nw_ref[...],
                   wl_ref, lv_ref[...], st_ref, p_ref, y_ref, fill)
        _run_all(fill)

    _skewed(i, pa_ref, pb_ref, body)
    _state_out(j, st_ref, stm_ref)


def _u_spec(n_blocks):
    return [pl.BlockSpec((R, D_MODEL), _seq_block(n_blocks, 0))]


def _gla(u, n_blocks, w, w2, b2, norm_w, wl, lv, nb):
    return _mixer_call(
        functools.partial(_gla_kernel, nb), "gla", [u], _u_spec(n_blocks), n_blocks,
        [w, w2, b2, norm_w, wl, lv],
        [_window_spec(W_C_COL0, W_C_COLS), _const_spec(w2.shape),
         _const_spec(b2.shape), _const_spec((1, HEAD_W)), _const_spec(wl.shape), _const_spec((R, R))],
        W_C_COLS, _state_scratch() + [pltpu.VMEM((HEADS, R, R), F32)])


TAIL = 8


def _gdn_kernel(nb, u_ref, w_ref, cw_ref, hp_ref, nw_ref, wl_ref, lv_ref,
                y_ref, st_ref, stm_ref, xx_ref, tailm_ref, x_ref, a_ref, pa_ref, pb_ref):
    i = pl.program_id(0)
    j = i - 1
    _state_in(j, nb, st_ref, stm_ref)

    @pl.when(j <= 0)
    def _():
        xx_ref[0:TAIL, :] = jnp.zeros((TAIL, 3 * BRANCH_W), F32)

    @pl.when((j >= 1) & ((j - 1) % nb == 0))
    def _():
        xx_ref[0:TAIL, :] = tailm_ref[...]

    def body(rd, wr):
        _gdn_body(j, rd, wr, u_ref, w_ref, cw_ref, hp_ref, nw_ref, wl_ref, lv_ref,
                  y_ref, st_ref, xx_ref, x_ref, a_ref)

    _skewed(i, pa_ref, pb_ref, body)

    @pl.when(j == 0)
    def _():
        tailm_ref[...] = xx_ref[0:TAIL, :]

    _state_out(j, st_ref, stm_ref)


def _gdn_body(j, rd, wr, u_ref, w_ref, cw_ref, hp_ref, nw_ref, wl_ref, lv_ref,
              y_ref, st_ref, xx_ref, x_ref, a_ref):
    fill = _projection_steps(u_ref[...], w_ref, wr)
    valid = _valid_rows(j == 0)
    lv = lv_ref[...]
    xx_ref[TAIL:, :] = rd[:, :3 * BRANCH_W]
    b_g = rd[:, 3 * BRANCH_W:4 * BRANCH_W]
    sm = rd[:, 4 * BRANCH_W:]
    conv = jnp.zeros((R, 3 * BRANCH_W), F32)
    for tap in range(CONV_W):
        off = TAIL - (CONV_W - 1) + tap
        conv = conv + xx_ref[off:off + R, :] * cw_ref[tap:tap + 1, :]
    qkv = _silu(conv)
    xx_ref[0:TAIL, :] = xx_ref[R:R + TAIL, :]

    a_neg = hp_ref[0:1, :]
    dt_b = hp_ref[1:2, :]
    log_a = a_neg * _softplus(sm + dt_b)
    la_hi = log_a.astype(BF16)
    la_lo = (log_a - la_hi.astype(F32)).astype(BF16)
    gcum = _dot(wl_ref[W_CUM], la_hi) + _dot(wl_ref[W_CUM], la_lo)
    gcum_t = gcum.T
    beta_all = _sigmoid(sm) * valid

    strict = (lv >= 0) & (lv < LV_DIAG)
    causal = lv >= 0
    eye = (lv == LV_DIAG).astype(F32)

    heads = []
    for h in range(HEADS):
        qh = qkv[:, h * HEAD_W:(h + 1) * HEAD_W]
        kh = qkv[:, BRANCH_W + h * HEAD_W:BRANCH_W + (h + 1) * HEAD_W] * valid
        vh = qkv[:, 2 * BRANCH_W + h * HEAD_W:2 * BRANCH_W + (h + 1) * HEAD_W]
        qh = qh * lax.rsqrt(jnp.sum(qh * qh, axis=-1, keepdims=True) + L2_EPS) * (HEAD_W ** -0.5)
        kh = kh * lax.rsqrt(jnp.sum(kh * kh, axis=-1, keepdims=True) + L2_EPS)
        beta = beta_all[:, SM_BETA + h:SM_BETA + h + 1]
        gcol = gcum[:, SM_DECAY + h:SM_DECAY + h + 1]
        grow = gcum_t[SM_DECAY + h:SM_DECAY + h + 1, :]
        dm = jnp.exp(jnp.minimum(gcol - grow, 0.0))
        q16 = qh.astype(BF16)
        k16 = kh.astype(BF16)
        a_ref[h] = jnp.where(strict, beta * _dot_nt(k16, k16) * dm, 0.0)
        qk = jnp.where(causal, _dot_nt(q16, k16) * dm, 0.0).astype(BF16)
        heads.append((qh, kh, vh, beta, gcol, qk))
        _run_share(fill, HEADS + 1 - h)

    for lvl in range(N_LEVELS):
        m = lv == lvl
        for h in range(HEADS):
            if 2 << lvl == R:
                hs = R // 2
                x_lo = x_ref[h, hs:, hs:].astype(BF16)
                x_up = x_ref[h, :hs, :hs].astype(BF16)
                x_ref[h, hs:, :hs] = -_dot(_dot(x_lo, a_ref[h, hs:, :hs].astype(BF16)).astype(BF16), x_up)
                continue
            l_lvl = jnp.where(m, a_ref[h], 0.0)
            if lvl == 0:
                x_ref[h] = eye - l_lvl
            else:
                xcur = x_ref[h].astype(BF16)
                x_ref[h] = x_ref[h] - _dot(_dot(xcur, l_lvl.astype(BF16)).astype(BF16), xcur)

    for h, (qh, kh, vh, beta, gcol, qk) in enumerate(heads):
        sl = slice(h * HEAD_W, (h + 1) * HEAD_W)
        eg = jnp.exp(gcol)
        rhs = jnp.concatenate([vh * beta, kh * (beta * eg)], axis=-1).astype(BF16)
        sol = _dot(x_ref[h].astype(BF16), rhs)
        u_h = sol[:, :HEAD_W]
        w_h = sol[:, HEAD_W:]
        st = st_ref[h]
        st16 = st.astype(BF16)
        v_new = u_h - _dot_nt(w_h.astype(BF16), st16)
        vn16 = v_new.astype(BF16)
        o = _dot_nt((qh * eg).astype(BF16), st16) + _dot(qk, vn16)
        g_last = gcol[R - 1:R, :]
        ke = (kh * jnp.exp(g_last - gcol)).astype(BF16)
        st_ref[h] = st * jnp.exp(g_last) + _dot_tn(vn16, ke)
        y_ref[:, sl] = _head_out(o, b_g[:, sl], nw_ref[...])
    _run_all(fill)


def _gdn(u, n_blocks, w, conv_w, head_params, norm_w, wl, lv, nb):
    return _mixer_call(
        functools.partial(_gdn_kernel, nb), "gdn", [u], _u_spec(n_blocks), n_blocks,
        [w, conv_w, head_params, norm_w, wl, lv],
        [_window_spec(W_B_COL0, W_B_COLS), _const_spec((CONV_W, 3 * BRANCH_W)),
         _const_spec((8, LANES)), _const_spec((1, HEAD_W)), _const_spec(wl.shape), _const_spec((R, R))],
        W_B_COLS, _state_scratch() + [pltpu.VMEM((R + TAIL, 3 * BRANCH_W), F32),
                            pltpu.VMEM((TAIL, 3 * BRANCH_W), F32),
                            pltpu.VMEM((HEADS, R, R), F32),
                            pltpu.VMEM((HEADS, R, R), F32)])


def _split_hi_lo(x):
    hi = x.astype(BF16)
    return hi, (x - hi.astype(F32)).astype(BF16)


def _merge_kernel(ya_ref, yb_ref, yc_ref, u_ref, hx_ref, hm_ref, wgate_ref, wb_ref, wo_ref, nf_ref,
                  wr_ref, br_ref, hn_ref, u2_ref, cmb_ref, lg_ref):
    i = pl.program_id(0)
    last = pl.num_programs(0) - 2

    @pl.when(i == 0)
    def _():
        lg_ref[...] = jnp.zeros(lg_ref.shape, lg_ref.dtype)

    cmb_ref[...] = _route(lg_ref[...])

    is_meta = i >= last
    h = _residual_block(hx_ref, hm_ref, is_meta)
    u = u_ref[...]
    merged = jnp.zeros((R, D_MODEL), F32)
    for n, y_ref in enumerate((ya_ref, yb_ref, yc_ref)):
        gate = _sigmoid(_dot(u, wgate_ref[:, n * D_MODEL:(n + 1) * D_MODEL]))
        merged = merged + gate * _dot(y_ref[...], wb_ref[0, n])
    hn = h + _dot(merged.astype(BF16), wo_ref[0])
    hn_ref[...] = hn
    u2 = _rmsnorm(hn, nf_ref[...])
    u2_ref[...] = u2.astype(BF16)

    u_hi, u_lo = _split_hi_lo(u2)
    both = _dot(u_hi, wr_ref[...])
    lg_ref[...] = both[:, :LANES] + both[:, LANES:] + _dot(u_lo, wr_ref[:, :LANES]) + br_ref[...]


def _route(logits):
    lane_i = lax.broadcasted_iota(jnp.int32, (R, LANES), 1)
    lane = lane_i.astype(F32)
    lane_grp = (lane_i // EXP_PER_GROUP).astype(F32)
    neg = jnp.float32(-jnp.inf)
    big = jnp.float32(1e9)
    is_g = (lane_i >= N_EXPERTS) & (lane_i < N_EXPERTS + N_GROUPS)
    lg = jnp.where(is_g, logits, neg)
    mg = jnp.max(lg, axis=-1, keepdims=True)
    zg = jnp.sum(jnp.exp(lg - mg), axis=-1, keepdims=True)
    g_val = 1.0 / zg
    g_idx = jnp.min(jnp.where(lg == mg, lane, big), axis=-1, keepdims=True) - N_EXPERTS
    in_grp = (lane_i < N_EXPERTS) & (lane_grp == g_idx)
    le = jnp.where(in_grp, logits, neg)
    m1 = jnp.max(le, axis=-1, keepdims=True)
    ze = jnp.sum(jnp.exp(le - m1), axis=-1, keepdims=True)
    i1 = jnp.min(jnp.where(le == m1, lane, big), axis=-1, keepdims=True)
    le2 = jnp.where(lane == i1, neg, le)
    m2 = jnp.max(le2, axis=-1, keepdims=True)
    i2 = jnp.min(jnp.where(le2 == m2, lane, big), axis=-1, keepdims=True)
    p1 = 1.0 / ze
    p2 = jnp.exp(m2 - m1) / ze
    den = p1 + p2
    cmb = g_val * jnp.where(lane == i1, p1 / den, jnp.where(lane == i2, p2 / den, 0.0))
    return jnp.where(lane_i == CMB_GROUP_LANE, g_idx, cmb)


def _merge(ya, yb, yc, u, hx, hm, w_gate, wb, wo, layer, nf, wr, br):
    t = ya.shape[0]
    last = t // R - 1
    cur = lambda i: (jnp.minimum(i, last), 0)
    row = lambda w: pl.BlockSpec((R, w), cur)
    return pl.pallas_call(
        _merge_kernel,
        grid=(t // R + 1,),
        in_specs=[row(BRANCH_W), row(BRANCH_W), row(BRANCH_W), row(D_MODEL)]
        + _residual_specs(hx, hm, cur)
        + [_window_spec(W_G_COL0, W_G_COLS),
                  pl.BlockSpec((1, N_BRANCH, BRANCH_W, D_MODEL), lambda i: (layer, 0, 0, 0)),
                  pl.BlockSpec((1, D_MODEL, D_MODEL), lambda i: (layer, 0, 0)),
                  _const_spec((1, D_MODEL)), _const_spec((D_MODEL, 2 * LANES)), _const_spec((1, LANES))],
        out_specs=[row(D_MODEL), row(D_MODEL),
                   pl.BlockSpec((R, LANES), lambda i: (jnp.maximum(i - 1, 0), 0))],
        out_shape=[jax.ShapeDtypeStruct((t, D_MODEL), F32),
                   jax.ShapeDtypeStruct((t, D_MODEL), BF16),
                   jax.ShapeDtypeStruct((t, LANES), F32)],
        scratch_shapes=[pltpu.VMEM((R, LANES), F32)],
        compiler_params=_cparams(("arbitrary",)),
        name="merge_router",
    )(ya, yb, yc, u, hx, hm, w_gate, wb, wo, nf, wr, br)


def _moe_kernel(final, cnt_ref, u_ref, c_ref, h_ref, wg_ref, wu_ref, wd_ref, tri_ref, nfin_ref, o_ref):
    w = pl.program_id(0)
    g = pl.program_id(1)
    u = u_ref[...]
    cmb = c_ref[...]
    gid_row = cmb.T[CMB_GROUP_LANE:CMB_GROUP_LANE + 1, :]
    in_row = jnp.where(gid_row == g.astype(F32), 1.0, 0.0)
    ranks = []
    before = jnp.zeros((1, 1), F32)
    for j in range(MOE_W // R):
        seg = in_row[:, j * R:(j + 1) * R]
        ranks.append(_dot(jnp.broadcast_to(seg, (16, R)).astype(BF16), tri_ref[...])[0:1, :] + before)
        before = before + jnp.sum(seg, axis=-1, keepdims=True)
    rank_row = jnp.concatenate(ranks, axis=-1)
    cmb_hi, cmb_lo = _split_hi_lo(cmb)

    @pl.when(g == 0)
    def _():
        o_ref[...] = h_ref[...]

    count = cnt_ref[w * N_GROUPS + g]
    n_pass = (count + MOE_CAPS[-1] - 1) // MOE_CAPS[-1]
    per_pass = (count + jnp.maximum(n_pass, 1) - 1) // jnp.maximum(n_pass, 1)

    def run(cap):
        slot = lax.broadcasted_iota(jnp.int32, (cap, 1), 0).astype(F32)
        lane = lax.broadcasted_iota(jnp.int32, (cap, LANES), 1)

        def one_pass(c, carry):
            base = (c * cap).astype(F32)
            sel = jnp.where((in_row > 0.0) & (rank_row - base == slot), 1.0, 0.0).astype(BF16)
            x = _dot(sel, u).astype(BF16)
            cw = _dot(sel, cmb_hi) + _dot(sel, cmb_lo)
            y = jnp.zeros((cap, D_MODEL), F32)
            for e in range(EXP_PER_GROUP):
                ce = jnp.sum(jnp.where(lane == g * EXP_PER_GROUP + e, cw, 0.0), axis=-1, keepdims=True)
                hid = _silu(_dot(x, wg_ref[0, 0, e])) * _dot(x, wu_ref[0, 0, e]) * ce
                y = y + _dot(hid.astype(BF16), wd_ref[0, 0, e])
            o_ref[...] += _dot_tn(sel, y.astype(BF16))
            return carry

        lax.fori_loop(0, n_pass, one_pass, 0)

    below = 0
    for cap in MOE_CAPS:
        pl.when((per_pass > below) & (per_pass <= cap))(functools.partial(run, cap))
        below = cap

    if final:
        @pl.when(g == N_GROUPS - 1)
        def _():
            o_ref[...] = _rmsnorm(o_ref[...], nfin_ref[...])


def _moe(u2, cmb, h, wg, wu, wd, layer, norm_final, final):
    t = h.shape[0]
    n_win = t // MOE_W
    assert t % MOE_W == 0
    gid = cmb[:, CMB_GROUP_LANE].astype(jnp.int32).reshape(n_win, MOE_W)
    counts = jnp.sum(gid[:, :, None] == jnp.arange(N_GROUPS, dtype=jnp.int32), axis=1, dtype=jnp.int32)
    tri = jnp.asarray(np.triu(np.ones((R, R), np.float32), 1), dtype=BF16)
    row = lambda width: pl.BlockSpec((MOE_W, width), lambda w, g, cnt: (w, 0))
    wspec = lambda a, b: pl.BlockSpec((1, 1, EXP_PER_GROUP, a, b), lambda w, g, cnt: (layer, g, 0, 0, 0))
    return pl.pallas_call(
        functools.partial(_moe_kernel, final),
        grid_spec=pltpu.PrefetchScalarGridSpec(
            num_scalar_prefetch=1,
            grid=(n_win, N_GROUPS),
            in_specs=[row(D_MODEL), row(LANES),
                      row(D_MODEL),
                      wspec(D_MODEL, EXPERT_HIDDEN), wspec(D_MODEL, EXPERT_HIDDEN),
                      wspec(EXPERT_HIDDEN, D_MODEL),
                      pl.BlockSpec((R, R), lambda w, g, cnt: (0, 0)),
                      pl.BlockSpec((1, D_MODEL), lambda w, g, cnt: (0, 0))],
            out_specs=row(D_MODEL)),
        out_shape=jax.ShapeDtypeStruct((t - R if final else t, D_MODEL), F32),
        compiler_params=_cparams(("arbitrary", "arbitrary")),
        name="moe",
    )(counts.reshape(-1), u2, cmb, h, wg, wu, wd, tri, norm_final)


IN_SPLITS = (512, 512, 512, 512, 1536, 4, 4, 512, 256, 256, 512, 16, 512, 3072)


IN_OFFS = tuple(int(v) for v in np.cumsum((0,) + IN_SPLITS))
(O_AQ, O_AF, O_AI, O_AG, O_BQKV, O_BBETA, O_BDECAY, O_BG, O_CQ, O_CK, O_CV, O_CGK, O_CG, O_GATES, O_END) = IN_OFFS
W_A_COLS = O_BQKV - O_AQ
W_B_COLS = (O_BBETA - O_BQKV) + (O_CQ - O_BG) + LANES
W_C_COLS = (O_CGK - O_CQ) + (O_GATES - O_CG) + LANES
W_G_COLS = O_END - O_GATES
W_A_COL0 = 0
W_B_COL0 = W_A_COL0 + W_A_COLS
W_C_COL0 = W_B_COL0 + W_B_COLS
W_G_COL0 = W_C_COL0 + W_C_COLS
W_ALL_COLS = W_G_COL0 + W_G_COLS
assert O_BBETA % LANES == SM_BETA and O_BDECAY - O_BBETA == SM_DECAY and O_CGK % LANES == SM_GK


@functools.lru_cache(maxsize=None)
def _wprep_table():
    rows = []

    def run(src0, n_cols):
        rows.extend((src0 + c, 0, LANES) for c in range(0, n_cols, LANES))

    run(O_AQ, W_A_COLS)
    run(O_BQKV, O_BBETA - O_BQKV)
    run(O_BG, O_CQ - O_BG)
    rows.append((O_BBETA - SM_BETA, SM_BETA, SM_GK))
    run(O_CQ, O_CGK - O_CQ)
    run(O_CG, O_GATES - O_CG)
    rows.append((O_CGK - SM_GK, SM_GK, SM_GK + C_GK_RANK))
    run(O_GATES, W_G_COLS)
    assert len(rows) * LANES == W_ALL_COLS
    rows.extend([(0, 0, 0)] * (-len(rows) % WPREP_TILES))
    table = np.asarray(rows, np.int32)
    assert (table[:, 0] % 8 == 0).all()
    return table


WPREP_TILES = 8


def _wprep_kernel(tab_ref, *refs):
    o_ref = refs[-1]
    j = pl.program_id(0)
    for k, wt_ref in enumerate(refs[:-1]):
        tile = WPREP_TILES * j + k
        t = wt_ref[0].T
        lane = lax.broadcasted_iota(jnp.int32, t.shape, 1)
        keep = (lane >= tab_ref[3 * tile + 1]) & (lane < tab_ref[3 * tile + 2])
        o_ref[:, k * LANES:(k + 1) * LANES] = jnp.where(keep, t, 0.0).astype(BF16)


def _inproj_weights(w_in_t, layer):
    table = _wprep_table()
    n_steps = table.shape[0] // WPREP_TILES

    def source_tile(k):
        return pl.BlockSpec(
            (pl.Element(1), pl.Element(LANES), pl.Element(D_MODEL)),
            lambda j, tab: (layer, pl.multiple_of(tab[3 * (WPREP_TILES * j + k)], 8), 0))

    return pl.pallas_call(
        _wprep_kernel,
        grid_spec=pltpu.PrefetchScalarGridSpec(
            num_scalar_prefetch=1,
            grid=(n_steps,),
            in_specs=[source_tile(k) for k in range(WPREP_TILES)],
            out_specs=pl.BlockSpec((D_MODEL, WPREP_TILES * LANES), lambda j, tab: (0, j))),
        out_shape=jax.ShapeDtypeStruct((D_MODEL, table.shape[0] * LANES), BF16),
        compiler_params=_cparams(("arbitrary",)),
        name="wprep",
    )(jnp.asarray(table.reshape(-1)), *([w_in_t] * WPREP_TILES))


def _window_spec(col0, width):
    return pl.BlockSpec((pl.Element(D_MODEL), pl.Element(width)), lambda *_: (0, col0))


def kernel(x, meta_tokens, norm_mix, w_in, hgrn_lb_logits, hgrn_norm, gdn_conv, gdn_a_log,
           gdn_dt_bias, gdn_norm, gla_gk_w2, gla_gk_b, gla_norm, w_branch, w_out, norm_ffn,
           router_group_w, router_group_b, router_expert_w, router_expert_b,
           expert_w_gate, expert_w_up, expert_w_down, norm_final):
    batch, seq, d = x.shape
    assert d == D_MODEL and seq % R == 0
    depth = w_in.shape[0]
    nb = seq // R
    lv_np, wl_np = _level_constants()
    lv = jnp.asarray(lv_np)
    wl = jnp.asarray(wl_np, dtype=BF16)

    n_blocks = batch * nb + 1
    hx = x.reshape(batch * seq, d)
    hm = jnp.concatenate([jnp.zeros((R - N_META, d), F32), meta_tokens.astype(F32)], axis=0)

    w_in_t = jnp.swapaxes(w_in, 1, 2)
    wg16, wu16, wd16, wb16, wo16 = (w.astype(BF16) for w in (
        expert_w_gate, expert_w_up, expert_w_down, w_branch, w_out))

    lb_p = jax.nn.softmax(hgrn_lb_logits.astype(F32), axis=0)
    lb_all = jnp.maximum(jnp.cumsum(lb_p, axis=0) - lb_p[0:1], 0.0)

    for layer in range(depth):
        w_a = w_b = w_c = w_gate = _inproj_weights(w_in_t, layer)
        nm = norm_mix[layer][None, :]

        lb = lb_all[layer]
        lb_rows = jnp.zeros((8, BRANCH_W), F32)
        lb_rows = lb_rows.at[0].set(jnp.maximum(lb, LB_FLOOR)).at[1].set(1.0 - lb)
        y_a, u = _hgrn(hx, hm, n_blocks, nm, w_a, lb_rows, hgrn_norm[layer][None, :], wl, lv, nb)

        head_params = jnp.zeros((8, LANES), F32)
        head_params = head_params.at[0, SM_DECAY:SM_DECAY + HEADS].set(-jnp.exp(gdn_a_log[layer]))
        head_params = head_params.at[1, SM_DECAY:SM_DECAY + HEADS].set(gdn_dt_bias[layer])
        y_b = _gdn(u, n_blocks, w_b, gdn_conv[layer], head_params, gdn_norm[layer][None, :], wl, lv, nb)

        w2 = jnp.zeros((LANES, HEADS * C_DK), F32).at[SM_GK:SM_GK + C_GK_RANK].set(gla_gk_w2[layer])
        y_c = _gla(u, n_blocks, w_c, w2.astype(BF16), gla_gk_b[layer][None, :],
                   gla_norm[layer][None, :], wl, lv, nb)

        wr = jnp.zeros((D_MODEL, LANES), F32)
        wr = wr.at[:, :N_EXPERTS].set(router_expert_w[layer])
        wr = wr.at[:, N_EXPERTS:N_EXPERTS + N_GROUPS].set(router_group_w[layer])
        br = jnp.zeros((1, LANES), F32)
        br = br.at[0, :N_EXPERTS].set(router_expert_b[layer])
        br = br.at[0, N_EXPERTS:N_EXPERTS + N_GROUPS].set(router_group_b[layer])
        wr_hi = wr.astype(BF16)
        wr_lo = (wr - wr_hi.astype(F32)).astype(BF16)
        hn, u2, cmb = _merge(y_a, y_b, y_c, u, hx, hm, w_gate, wb16, wo16, layer,
                             norm_ffn[layer][None, :], jnp.concatenate([wr_hi, wr_lo], axis=1), br)

        hx = hm = _moe(u2, cmb, hn, wg16, wu16, wd16, layer, norm_final[None, :], final=(layer == depth - 1))

    return hx.reshape(batch, seq, d)
```
